```python
import jax, jax.numpy as jnp
from jax import lax
import numpy as np

D_MODEL = 1024
BATCH = 16
SEQ = 4096
DEPTH = 1
DEC_BATCH = 4
DEC_SEQ = 8192
PAST_LEN = 128

HEAD_DIM = 64
HQ_A = 8
HKV_A = 2
H_B = 8
D_MIX = (HQ_A + H_B) * HEAD_DIM
SPLITS = (HQ_A * HEAD_DIM,
          HQ_A * HEAD_DIM + HKV_A * HEAD_DIM,
          HQ_A * HEAD_DIM + 2 * HKV_A * HEAD_DIM,
          HQ_A * HEAD_DIM + 2 * HKV_A * HEAD_DIM + H_B * HEAD_DIM,
          HQ_A * HEAD_DIM + 2 * HKV_A * HEAD_DIM + 2 * H_B * HEAD_DIM)
D_IN = HQ_A * HEAD_DIM + 2 * HKV_A * HEAD_DIM + 3 * H_B * HEAD_DIM
WINDOW = 128
BLK = 128
ROPE_THETA = 500000.0
ROT_DIM = HEAD_DIM // 4
GRID_W = 64
NA_ROWS = 8
NA_COLS = 16
NA_QROWS = 2
N_COLBLK = GRID_W // NA_COLS
KEY_COLS = 2 * NA_COLS
N_EXPERTS = 16
CAP_FACTOR = 2
D_EXPERT = 2 * D_MODEL
RMS_EPS = 1e-6
NEG_INF = -1e30

kernel_name = "hymba_window_natten_ec_moe_encoder"


def _rmsnorm(x, g):
    xf = x.astype(jnp.float32)
    y = xf * lax.rsqrt(jnp.mean(xf * xf, axis=-1, keepdims=True) + RMS_EPS)
    return (y * g.astype(jnp.float32)).astype(x.dtype)


def _partial_rotary(x):
    S = x.shape[1]
    half = ROT_DIM // 2
    inv_freq = jnp.float32(ROPE_THETA) ** (-(jnp.arange(half, dtype=jnp.float32) * 2.0) / ROT_DIM)
    ang = jnp.arange(S, dtype=jnp.float32)[:, None] * inv_freq[None, :]
    cos = jnp.cos(ang)[None, :, None, :]
    sin = jnp.sin(ang)[None, :, None, :]
    xf = x.astype(jnp.float32)
    x1 = xf[..., :half]
    x2 = xf[..., half:ROT_DIM]
    out = jnp.concatenate([x1 * cos - x2 * sin, x2 * cos + x1 * sin, xf[..., ROT_DIM:]], axis=-1)
    return out.astype(x.dtype)


def _window_attention(q, k, v, sink):
    B, S = q.shape[0], q.shape[1]
    nb = S // BLK
    grp = HQ_A // HKV_A
    qb = q.reshape(B, nb, BLK, HKV_A, grp, HEAD_DIM)
    pad = ((0, 0), (BLK, BLK), (0, 0), (0, 0))
    kp = jnp.pad(k, pad).reshape(B, nb + 2, BLK, HKV_A, HEAD_DIM)
    vp = jnp.pad(v, pad).reshape(B, nb + 2, BLK, HKV_A, HEAD_DIM)
    kw = jnp.concatenate([kp[:, :-2], kp[:, 1:-1], kp[:, 2:]], axis=2)
    vw = jnp.concatenate([vp[:, :-2], vp[:, 1:-1], vp[:, 2:]], axis=2)
    s = jnp.einsum('bnqhgd,bnkhd->bnhgqk', qb, kw).astype(jnp.float32) * (HEAD_DIM ** -0.5)
    qi = jnp.arange(BLK)[:, None]
    kj = jnp.arange(3 * BLK)[None, :]
    kpos = (jnp.arange(nb)[:, None, None] - 1) * BLK + kj[None]
    valid = (jnp.abs(kj - BLK - qi) <= WINDOW)[None] & (kpos >= 0) & (kpos < S)
    s = jnp.where(valid[None, :, None, None], s, NEG_INF)
    sb = sink.astype(jnp.float32).reshape(1, 1, HKV_A, grp, 1, 1)
    m = jnp.maximum(jnp.max(s, axis=-1, keepdims=True), sb)
    p = jnp.exp(s - m)
    denom = jnp.sum(p, axis=-1, keepdims=True) + jnp.exp(sb - m)
    o = jnp.einsum('bnhgqk,bnkhd->bnqhgd', (p / denom).astype(v.dtype), vw)
    return o.reshape(B, S, HQ_A * HEAD_DIM)


def _neighbourhood_attention(q, k, v, rpb):
    B, S = q.shape[0], q.shape[1]
    rows = S // GRID_W
    kr = min(NA_ROWS, rows)
    qr = NA_QROWS
    krb = min(kr + qr - 1, rows)
    nrb = rows // qr
    qg = q.reshape(B, rows, GRID_W, H_B, HEAD_DIM)
    kg = k.reshape(B, rows, GRID_W, H_B, HEAD_DIM)
    vg = v.reshape(B, rows, GRID_W, H_B, HEAD_DIM)
    qc = np.arange(GRID_W).reshape(N_COLBLK, NA_COLS)
    kc0 = np.clip(np.arange(N_COLBLK) * NA_COLS - NA_COLS // 2, 0, GRID_W - KEY_COLS)
    kc = kc0[:, None] + np.arange(KEY_COLS)[None, :]
    cs = np.clip(qc - NA_COLS // 2, 0, GRID_W - NA_COLS)
    col_ok = (kc[:, None, :] >= cs[:, :, None]) & (kc[:, None, :] < cs[:, :, None] + NA_COLS)
    cidx = np.clip(kc[:, None, :] - qc[:, :, None] + NA_COLS - 1, 0, 2 * NA_COLS - 2)
    nkeys = krb * KEY_COLS
    scale = HEAD_DIM ** -0.5

    def row_block(rb):
        r0 = rb * qr
        qrows = r0 + jnp.arange(qr)
        rs = jnp.clip(qrows - kr // 2, 0, rows - kr)
        kstart = jnp.clip(r0 - kr // 2, 0, rows - krb)
        krows = kstart + jnp.arange(krb)
        qb = lax.dynamic_slice_in_dim(qg, r0, qr, axis=1).reshape(B, qr, N_COLBLK, NA_COLS, H_B, HEAD_DIM)
        kb = lax.dynamic_slice_in_dim(kg, kstart, krb, axis=1)[:, :, kc]
        vb = lax.dynamic_slice_in_dim(vg, kstart, krb, axis=1)[:, :, kc]
        kb = kb.transpose(0, 2, 1, 3, 4, 5).reshape(B, N_COLBLK, nkeys, H_B, HEAD_DIM)
        vb = vb.transpose(0, 2, 1, 3, 4, 5).reshape(B, N_COLBLK, nkeys, H_B, HEAD_DIM)
        row_ok = (krows[None, :] >= rs[:, None]) & (krows[None, :] < rs[:, None] + kr)
        ridx = jnp.clip(krows[None, :] - qrows[:, None] + NA_ROWS - 1, 0, 2 * NA_ROWS - 2)
        valid = (row_ok[None, :, None, :, None] & col_ok[:, None, :, None, :]).reshape(
            N_COLBLK, qr, NA_COLS, nkeys)
        bias = rpb.astype(jnp.float32)[:, ridx[None, :, None, :, None], cidx[:, None, :, None, :]]
        bias = bias.reshape(H_B, N_COLBLK, qr, NA_COLS, nkeys)
        s = jnp.einsum('brjchd,bjkhd->bhjrck', qb, kb).astype(jnp.float32) * scale + bias[None]
        s = jnp.where(valid[None, None], s, NEG_INF)
        p = jax.nn.softmax(s, axis=-1).astype(v.dtype)
        o = jnp.einsum('bhjrck,bjkhd->brjchd', p, vb)
        return o.reshape(B, qr, GRID_W, H_B * HEAD_DIM)

    out = lax.map(row_block, jnp.arange(nrb))
    return out.transpose(1, 0, 2, 3, 4).reshape(B, S, H_B * HEAD_DIM)


def _mixers(hn, w_in, g_out_a, g_out_b, sink_a, rpb_b, w_out):
    B, S = hn.shape[0], hn.shape[1]
    proj = jnp.einsum('bsd,de->bse', hn, w_in)
    qa, ka, va, qb, kb, vb = jnp.split(proj, list(SPLITS), axis=-1)
    qa = _partial_rotary(qa.reshape(B, S, HQ_A, HEAD_DIM))
    ka = _partial_rotary(ka.reshape(B, S, HKV_A, HEAD_DIM))
    va = va.reshape(B, S, HKV_A, HEAD_DIM)
    oa = _window_attention(qa, ka, va, sink_a)
    ob = _neighbourhood_attention(qb.reshape(B, S, H_B, HEAD_DIM), kb.reshape(B, S, H_B, HEAD_DIM),
                                  vb.reshape(B, S, H_B, HEAD_DIM), rpb_b)
    merged = jnp.concatenate([_rmsnorm(oa, g_out_a), _rmsnorm(ob, g_out_b)], axis=-1)
    return jnp.einsum('bse,ed->bsd', merged, w_out)


def _expert_choice_ffn(h, w_router, w_gate, w_up, w_down):
    N, D = h.shape
    cap = CAP_FACTOR * N // N_EXPERTS
    aff = jax.nn.softmax(jnp.einsum('nd,de->ne', h, w_router).astype(jnp.float32), axis=-1)
    gate, idx = lax.top_k(aff.T, cap)
    xe = h[idx]
    hid = jax.nn.silu(jnp.einsum('ecd,edf->ecf', xe, w_gate)) * jnp.einsum('ecd,edf->ecf', xe, w_up)
    ye = jnp.einsum('ecf,efd->ecd', hid, w_down) * gate[..., None].astype(h.dtype)
    return jnp.zeros_like(h).at[idx.reshape(-1)].add(ye.reshape(-1, D))


def _encoder_layer(x, g_attn, w_in, g_out_a, g_out_b, sink_a, rpb_b, w_out,
                   g_ffn, w_router, w_gate, w_up, w_down):
    B, S, D = x.shape
    x = x + _mixers(_rmsnorm(x, g_attn), w_in, g_out_a, g_out_b, sink_a, rpb_b, w_out)
    h = _rmsnorm(x, g_ffn).reshape(B * S, D)
    return x + _expert_choice_ffn(h, w_router, w_gate, w_up, w_down).reshape(B, S, D)


def setup_inputs(seed: int = 0) -> dict:
    key = jax.random.key(seed)
    ks = jax.random.split(key, 16)
    f32 = jnp.float32
    nrm = lambda k, shape, s: jax.random.normal(k, shape, f32) * s
    return {
        "x_prompt": nrm(ks[0], (BATCH, SEQ, D_MODEL), 1.0),
        "x_sample": nrm(ks[1], (DEC_BATCH, DEC_SEQ, D_MODEL), 1.0),
        "g_attn": 1.0 + nrm(ks[2], (DEPTH, D_MODEL), 0.01),
        "w_in": nrm(ks[3], (DEPTH, D_MODEL, D_IN), D_MODEL ** -0.5),
        "g_out_a": 1.0 + nrm(ks[4], (DEPTH, HQ_A * HEAD_DIM), 0.01),
        "g_out_b": 1.0 + nrm(ks[5], (DEPTH, H_B * HEAD_DIM), 0.01),
        "sink_a": nrm(ks[6], (DEPTH, HQ_A), 0.5),
        "rpb_b": nrm(ks[7], (DEPTH, H_B, 2 * NA_ROWS - 1, 2 * NA_COLS - 1), 0.1),
        "w_out": nrm(ks[8], (DEPTH, D_MIX, D_MODEL), D_MIX ** -0.5),
        "g_ffn": 1.0 + nrm(ks[9], (DEPTH, D_MODEL), 0.01),
        "w_router": nrm(ks[10], (DEPTH, D_MODEL, N_EXPERTS), D_MODEL ** -0.5),
        "w_gate": nrm(ks[11], (DEPTH, N_EXPERTS, D_MODEL, D_EXPERT), D_MODEL ** -0.5),
        "w_up": nrm(ks[12], (DEPTH, N_EXPERTS, D_MODEL, D_EXPERT), D_MODEL ** -0.5),
        "w_down": nrm(ks[13], (DEPTH, N_EXPERTS, D_EXPERT, D_MODEL), D_EXPERT ** -0.5),
        "g_final": 1.0 + nrm(ks[14], (D_MODEL,), 0.01),
    }


def reference(x_prompt, x_sample, g_attn, w_in, g_out_a, g_out_b, sink_a, rpb_b, w_out,
              g_ffn, w_router, w_gate, w_up, w_down, g_final):
    h_p = x_prompt
    h_s = x_sample
    for l in range(DEPTH):
        h_p = _encoder_layer(h_p, g_attn[l], w_in[l], g_out_a[l], g_out_b[l], sink_a[l], rpb_b[l], w_out[l],
                             g_ffn[l], w_router[l], w_gate[l], w_up[l], w_down[l])
        h_s = _encoder_layer(h_s, g_attn[l], w_in[l], g_out_a[l], g_out_b[l], sink_a[l], rpb_b[l], w_out[l],
                             g_ffn[l], w_router[l], w_gate[l], w_up[l], w_down[l])
    y_prompt = _rmsnorm(h_p, g_final)
    y_sample = _rmsnorm(h_s, g_final)
    return (y_prompt, y_sample)
```

```python
import functools

import jax
import jax.numpy as jnp
import numpy as np
from jax import lax
from jax.experimental import pallas as pl
from jax.experimental.pallas import tpu as pltpu

D_MODEL = 1024
HEAD_DIM = 64
HQ_A = 8
HKV_A = 2
H_B = 8
GRP = HQ_A // HKV_A
DQA = HQ_A * HEAD_DIM
DKA = HKV_A * HEAD_DIM
DB = H_B * HEAD_DIM
D_IN = DQA + 2 * DKA + 3 * DB
WINDOW = 128
BLK = 128
ROPE_THETA = 500000.0
ROT_DIM = HEAD_DIM // 4
GRID_W = 64
NA_ROWS = 8
NA_COLS = 16
NA_QROWS = 2
NA_KROWS = NA_ROWS + NA_QROWS - 1
NA_KEYS = NA_KROWS * GRID_W
NA_Q = NA_QROWS * GRID_W
N_EXPERTS = 16
CAP_FACTOR = 2
D_EXPERT = 2 * D_MODEL
RMS_EPS = 1e-6
NEG_INF = -1e30
SCALE = HEAD_DIM ** -0.5

LANES = 128
TM = 512
TQ = 512
TC = 512
FC = 512
TS = 256
SB = 256
VMEM_LIMIT = 56 * 1024 * 1024

BF16 = jnp.bfloat16
F32 = jnp.float32
_NT = (((1,), (1,)), ((), ()))


def _cparams(sem):
    return pltpu.CompilerParams(dimension_semantics=sem, vmem_limit_bytes=VMEM_LIMIT)


def _inproj_kernel(x_ref, g_ref, w_ref, cos_ref, sa_ref, sb_ref,
                   qa_ref, ka_ref, va_ref, qb_ref, kb_ref, vb_ref):
    x = x_ref[...]
    ms = jnp.mean(x * x, axis=-1, keepdims=True)
    hn = (x * lax.rsqrt(ms + RMS_EPS) * g_ref[...]).astype(BF16)
    c, sa, sb = cos_ref[...], sa_ref[...], sb_ref[...]

    def proj(lo, n):
        return jnp.dot(hn, w_ref[:, lo:lo + n], preferred_element_type=F32)

    def rot(blk):
        return blk * c + pltpu.roll(blk, LANES - ROT_DIM // 2, 1) * sa + pltpu.roll(blk, ROT_DIM // 2, 1) * sb

    for j in range(DQA // LANES):
        qa_ref[:, j * LANES:(j + 1) * LANES] = (rot(proj(j * LANES, LANES)) * SCALE).astype(BF16)
    ka_ref[...] = rot(proj(DQA, DKA)).astype(BF16)
    va_ref[...] = proj(DQA + DKA, DKA).astype(BF16)
    qb_ref[...] = (proj(DQA + 2 * DKA, DB) * SCALE).astype(BF16)
    kb_ref[...] = proj(DQA + 2 * DKA + DB, DB).astype(BF16)
    vb_ref[...] = proj(DQA + 2 * DKA + 2 * DB, DB).astype(BF16)


def _inproj(x2, g, w_bf, cos_t, sa_t, sb_t, seq):
    n = x2.shape[0]
    per_seq = seq // TM
    tok = lambda w: pl.BlockSpec((TM, w), lambda i: (i, 0))
    full = lambda a: pl.BlockSpec(a.shape, lambda i: (0,) * a.ndim)
    pos = pl.BlockSpec((TM, LANES), lambda i: (i % per_seq, 0))
    widths = (DQA, DKA, DKA, DB, DB, DB)
    return pl.pallas_call(
        _inproj_kernel,
        grid=(n // TM,),
        in_specs=[tok(D_MODEL), full(g), full(w_bf), pos, pos, pos],
        out_specs=[tok(w) for w in widths],
        out_shape=[jax.ShapeDtypeStruct((n, w), BF16) for w in widths],
        compiler_params=_cparams(("parallel",)),
        name="inproj",
    )(x2, g, w_bf, cos_t, sa_t, sb_t)


def _rotary_tables(seq):
    half = ROT_DIM // 2
    inv_freq = jnp.float32(ROPE_THETA) ** (-(jnp.arange(half, dtype=F32) * 2.0) / ROT_DIM)
    ang = jnp.arange(seq, dtype=F32)[:, None] * inv_freq[None, :]
    cos, sin = jnp.cos(ang), jnp.sin(ang)
    ones = jnp.ones((seq, HEAD_DIM - ROT_DIM), F32)
    zeros = jnp.zeros((seq, HEAD_DIM - ROT_DIM), F32)
    zh = jnp.zeros((seq, half), F32)
    rep = LANES // HEAD_DIM
    cos_t = jnp.tile(jnp.concatenate([cos, cos, ones], axis=1), (1, rep))
    sa_t = jnp.tile(jnp.concatenate([-sin, zh, zeros], axis=1), (1, rep))
    sb_t = jnp.tile(jnp.concatenate([zh, sin, zeros], axis=1), (1, rep))
    return cos_t, sa_t, sb_t


def _window_kernel(sink_ref, q_ref, kp_ref, kc_ref, kn_ref, vp_ref, vc_ref, vn_ref, g_ref, o_ref, *, nb):
    i = pl.program_id(1)
    nsub = TQ // BLK
    k_all = jnp.concatenate([kp_ref[0], kc_ref[0], kn_ref[0]], axis=0)
    v_all = jnp.concatenate([vp_ref[0], vc_ref[0], vn_ref[0]], axis=0)
    rows = GRP * BLK
    qi = lax.broadcasted_iota(jnp.int32, (rows, 3 * BLK), 0) % BLK
    kj = lax.broadcasted_iota(jnp.int32, (rows, 3 * BLK), 1)
    band = jnp.abs(kj - BLK - qi) <= WINDOW
    for j in range(nsub):
        n = i * nsub + j
        k_lo = jnp.where(n >= 1, 0, BLK)
        k_hi = jnp.where(n <= nb - 2, 3 * BLK, 2 * BLK)
        valid = band & (kj >= k_lo) & (kj < k_hi)
        outs = []
        for h in range(HKV_A):
            kh = k_all[j * BLK:(j + 3) * BLK, h * HEAD_DIM:(h + 1) * HEAD_DIM]
            vh = v_all[j * BLK:(j + 3) * BLK, h * HEAD_DIM:(h + 1) * HEAD_DIM]
            heads = [h * GRP + g for g in range(GRP)]
            qs = jnp.concatenate(
                [q_ref[0, j * BLK:(j + 1) * BLK, hd * HEAD_DIM:(hd + 1) * HEAD_DIM] for hd in heads], axis=0)
            s = lax.dot_general(qs, kh, _NT, preferred_element_type=F32)
            s = jnp.where(valid, s, NEG_INF)
            sk = jnp.concatenate([jnp.full((BLK, 1), sink_ref[hd], F32) for hd in heads], axis=0)
            m = jnp.maximum(jnp.max(s, axis=-1, keepdims=True), sk)
            p = jnp.exp(s - m)
            denom = jnp.sum(p, axis=-1, keepdims=True) + jnp.exp(sk - m)
            o = jnp.dot((p / denom).astype(BF16), vh, preferred_element_type=F32)
            outs += [o[g * BLK:(g + 1) * BLK] for g in range(GRP)]
        ob = jnp.concatenate(outs, axis=1)
        ms = jnp.mean(ob * ob, axis=-1, keepdims=True)
        o_ref[0, j * BLK:(j + 1) * BLK, :] = (ob * lax.rsqrt(ms + RMS_EPS) * g_ref[...]).astype(BF16)


def _window(qa, ka, va, sink, g):
    b, seq, _ = qa.shape
    nb = seq // BLK
    r = TQ // BLK
    cur = lambda w: pl.BlockSpec((1, TQ, w), lambda bi, i: (bi, i, 0))
    prev = pl.BlockSpec((1, BLK, DKA), lambda bi, i: (bi, jnp.maximum(i * r - 1, 0), 0))
    nxt = pl.BlockSpec((1, BLK, DKA), lambda bi, i: (bi, jnp.minimum((i + 1) * r, nb - 1), 0))
    return pl.pallas_call(
        functools.partial(_window_kernel, nb=nb),
        grid=(b, seq // TQ),
        in_specs=[pl.BlockSpec(memory_space=pltpu.SMEM), cur(DQA), prev, cur(DKA), nxt, prev, cur(DKA), nxt,
                  pl.BlockSpec(g.shape, lambda bi, i: (0, 0))],
        out_specs=cur(DQA),
        out_shape=jax.ShapeDtypeStruct((b, seq, DQA), BF16),
        compiler_params=_cparams(("parallel", "parallel")),
        name="window_attn",
    )(sink, qa, ka, ka, ka, va, va, va, g)


def _natten_kernel(q_ref, k_ref, v_ref, tab_ref, g_ref, o_ref, *, rows):
    rb = pl.program_id(1)
    kstart = jnp.clip(NA_QROWS * rb - NA_ROWS // 2, 0, rows - NA_KROWS)
    off = pl.multiple_of(kstart * GRID_W, GRID_W)
    outs = []
    for h in range(H_B):
        sl = slice(h * HEAD_DIM, (h + 1) * HEAD_DIM)
        qh = q_ref[0, :, sl]
        kh = k_ref[0, pl.ds(off, NA_KEYS), sl]
        vh = v_ref[0, pl.ds(off, NA_KEYS), sl]
        s = lax.dot_general(qh, kh, _NT, preferred_element_type=F32) + tab_ref[0, h]
        m = jnp.max(s, axis=-1, keepdims=True)
        p = jnp.exp(s - m)
        pn = (p / jnp.sum(p, axis=-1, keepdims=True)).astype(BF16)
        outs.append(jnp.dot(pn, vh, preferred_element_type=F32))
    ob = jnp.concatenate(outs, axis=1)
    ms = jnp.mean(ob * ob, axis=-1, keepdims=True)
    o_ref[0] = (ob * lax.rsqrt(ms + RMS_EPS) * g_ref[...]).astype(BF16)


def _natten_variant_rowblocks(nrb):
    return (0, 1, 2, nrb - 2, nrb - 1)


def _natten_tables(rpb, rows):
    nrb = rows // NA_QROWS
    qc = np.arange(GRID_W)
    kc = np.arange(GRID_W)
    cs = np.clip(qc - NA_COLS // 2, 0, GRID_W - NA_COLS)
    col_ok = (kc[None, :] >= cs[:, None]) & (kc[None, :] < cs[:, None] + NA_COLS)
    cidx = np.clip(kc[None, :] - qc[:, None] + NA_COLS - 1, 0, 2 * NA_COLS - 2)
    tabs = []
    for rb in _natten_variant_rowblocks(nrb):
        r0 = rb * NA_QROWS
        qrows = r0 + np.arange(NA_QROWS)
        rs = np.clip(qrows - NA_ROWS // 2, 0, rows - NA_ROWS)
        kstart = int(np.clip(r0 - NA_ROWS // 2, 0, rows - NA_KROWS))
        krows = kstart + np.arange(NA_KROWS)
        row_ok = (krows[None, :] >= rs[:, None]) & (krows[None, :] < rs[:, None] + NA_ROWS)
        ridx = np.clip(krows[None, :] - qrows[:, None] + NA_ROWS - 1, 0, 2 * NA_ROWS - 2)
        valid = row_ok[:, None, :, None] & col_ok[None, :, None, :]
        bias = rpb.astype(F32)[:, ridx[:, None, :, None], cidx[None, :, None, :]]
        bias = jnp.where(valid[None], bias, NEG_INF)
        tabs.append(bias.reshape(H_B, NA_Q, NA_KEYS))
    return jnp.stack(tabs)


def _natten(qb, kb, vb, tabs, g):
    b, seq, _ = qb.shape
    rows = seq // GRID_W
    nrb = rows // NA_QROWS
    assert rows >= NA_KROWS and nrb >= 5

    def variant(rb):
        return jnp.where(rb < 2, rb, jnp.where(rb >= nrb - 2, rb - (nrb - 5), 2))

    whole = pl.BlockSpec((1, seq, DB), lambda bi, rb: (bi, 0, 0))
    blk = pl.BlockSpec((1, NA_Q, DB), lambda bi, rb: (bi, rb, 0))
    return pl.pallas_call(
        functools.partial(_natten_kernel, rows=rows),
        grid=(b, nrb),
        in_specs=[blk, whole, whole,
                  pl.BlockSpec((1, H_B, NA_Q, NA_KEYS), lambda bi, rb: (variant(rb), 0, 0, 0)),
                  pl.BlockSpec(g.shape, lambda bi, rb: (0, 0))],
        out_specs=blk,
        out_shape=jax.ShapeDtypeStruct((b, seq, DB), BF16),
        compiler_params=_cparams(("parallel", "arbitrary")),
        name="natten",
    )(qb, kb, vb, tabs, g)


def _outproj_kernel(x_ref, oa_ref, ob_ref, wo_ref, g_ref, wr_ref, x1_ref, h_ref, aff_ref):
    x1 = (x_ref[...]
          + jnp.dot(oa_ref[...], wo_ref[0:DQA, :], preferred_element_type=F32)
          + jnp.dot(ob_ref[...], wo_ref[DQA:DQA + DB, :], preferred_element_type=F32))
    x1_ref[...] = x1
    ms = jnp.mean(x1 * x1, axis=-1, keepdims=True)
    hf = x1 * lax.rsqrt(ms + RMS_EPS) * g_ref[...]
    h_hi = hf.astype(BF16)
    h_lo = (hf - h_hi.astype(F32)).astype(BF16)
    h_ref[...] = h_hi
    l_hi = lax.dot_general(wr_ref[...], h_hi, _NT, preferred_element_type=F32)
    l_lo = lax.dot_general(wr_ref[0:N_EXPERTS, :], h_lo, _NT, preferred_element_type=F32)
    logits = l_hi[0:N_EXPERTS] + l_hi[N_EXPERTS:2 * N_EXPERTS] + l_lo
    m = jnp.max(logits, axis=0, keepdims=True)
    e = jnp.exp(logits - m)
    aff_ref[...] = e / jnp.sum(e, axis=0, keepdims=True)


def _outproj(x2, oa, ob, wo_bf, g, wr_t):
    n = x2.shape[0]
    tok = lambda w: pl.BlockSpec((TM, w), lambda i: (i, 0))
    full = lambda a: pl.BlockSpec(a.shape, lambda i: (0,) * a.ndim)
    return pl.pallas_call(
        _outproj_kernel,
        grid=(n // TM,),
        in_specs=[tok(D_MODEL), tok(DQA), tok(DB), full(wo_bf), full(g), full(wr_t)],
        out_specs=[tok(D_MODEL), tok(D_MODEL), pl.BlockSpec((N_EXPERTS, TM), lambda i: (0, i))],
        out_shape=[jax.ShapeDtypeStruct((n, D_MODEL), F32), jax.ShapeDtypeStruct((n, D_MODEL), BF16),
                   jax.ShapeDtypeStruct((N_EXPERTS, n), F32)],
        compiler_params=_cparams(("parallel",)),
        name="outproj_router",
    )(x2, oa, ob, wo_bf, g, wr_t)


def _moe_kernel(x_ref, gate_ref, wg_ref, wu_ref, wd_ref, o_ref):
    x = x_ref[0]
    y = jnp.zeros((x.shape[0], D_MODEL), F32)
    for c in range(D_EXPERT // FC):
        sl = slice(c * FC, (c + 1) * FC)
        a = jnp.dot(x, wg_ref[0, :, sl], preferred_element_type=F32)
        u = jnp.dot(x, wu_ref[0, :, sl], preferred_element_type=F32)
        hid = (a * jax.nn.sigmoid(a) * u).astype(BF16)
        y = y + jnp.dot(hid, wd_ref[0, sl, :], preferred_element_type=F32)
    o_ref[0] = (y * gate_ref[0]).astype(BF16)


def _moe(xe, gate3, wg, wu, wd):
    e, cap, _ = xe.shape
    tc = min(TC, cap)
    tokb = lambda w: pl.BlockSpec((1, tc, w), lambda ei, i: (ei, i, 0))
    wspec = lambda a: pl.BlockSpec((1,) + a.shape[1:], lambda ei, i: (ei, 0, 0))
    return pl.pallas_call(
        _moe_kernel,
        grid=(e, cap // tc),
        in_specs=[tokb(D_MODEL), tokb(1), wspec(wg), wspec(wu), wspec(wd)],
        out_specs=tokb(D_MODEL),
        out_shape=jax.ShapeDtypeStruct((e, cap, D_MODEL), BF16),
        compiler_params=_cparams(("parallel", "arbitrary")),
        name="moe_ffn",
    )(xe, gate3, wg, wu, wd)


_FIRST, _LAST, _ACTIVE = 1, 2, 4


def _combine_kernel(tile_ref, blk_ref, flag_ref, tok_ref, yg_ref, x1_ref, g_ref, o_ref, acc_ref):
    s = pl.program_id(0)
    flag = flag_ref[s]

    @pl.when((flag & _FIRST) != 0)
    def _():
        acc_ref[...] = jnp.zeros_like(acc_ref)

    @pl.when((flag & _ACTIVE) != 0)
    def _():
        tok_row = tile_ref[s] * TS + lax.broadcasted_iota(jnp.int32, (TS, SB), 0)
        onehot = jnp.where(tok_ref[0] == tok_row, 1.0, 0.0).astype(BF16)
        acc_ref[...] += jnp.dot(onehot, yg_ref[...], preferred_element_type=F32)

    @pl.when((flag & _LAST) != 0)
    def _():
        y = x1_ref[...] + acc_ref[...]
        ms = jnp.mean(y * y, axis=-1, keepdims=True)
        o_ref[...] = y * lax.rsqrt(ms + RMS_EPS) * g_ref[...]


def _combine_schedule(tok_sorted, n):
    nt, nblk = n // TS, tok_sorted.shape[0] // SB
    steps = nt + nblk
    bounds = jnp.searchsorted(tok_sorted, jnp.arange(nt + 1, dtype=jnp.int32) * TS, side="left").astype(jnp.int32)
    lo, hi = bounds[:-1], bounds[1:]
    b_lo = jnp.minimum(lo // SB, nblk - 1)
    b_hi = jnp.where(hi > lo, (hi - 1) // SB, b_lo)
    nst = b_hi - b_lo + 1
    cum = jnp.cumsum(nst)
    start = cum - nst
    s = jnp.arange(steps, dtype=jnp.int32)
    tile = jnp.minimum(jnp.searchsorted(cum, s, side="right").astype(jnp.int32), nt - 1)
    active = s < cum[-1]
    blk = jnp.where(active, b_lo[tile] + s - start[tile], b_hi[nt - 1])
    first = active & (s == start[tile])
    last = active & (s == cum[tile] - 1)
    flag = first * _FIRST + last * _LAST + active * _ACTIVE
    return tile, blk.astype(jnp.int32), flag.astype(jnp.int32)


def _combine(tile, blk, flag, tok3, yg, x1, g):
    n = x1.shape[0]
    steps = tile.shape[0]
    grid_spec = pltpu.PrefetchScalarGridSpec(
        num_scalar_prefetch=3,
        grid=(steps,),
        in_specs=[pl.BlockSpec((1, 1, SB), lambda s, t, b, f: (b[s], 0, 0)),
                  pl.BlockSpec((SB, D_MODEL), lambda s, t, b, f: (b[s], 0)),
                  pl.BlockSpec((TS, D_MODEL), lambda s, t, b, f: (t[s], 0)),
                  pl.BlockSpec(g.shape, lambda s, t, b, f: (0, 0))],
        out_specs=pl.BlockSpec((TS, D_MODEL), lambda s, t, b, f: (t[s], 0)),
        scratch_shapes=[pltpu.VMEM((TS, D_MODEL), F32)],
    )
    return pl.pallas_call(
        _combine_kernel,
        grid_spec=grid_spec,
        out_shape=jax.ShapeDtypeStruct((n, D_MODEL), F32),
        compiler_params=_cparams(("arbitrary",)),
        name="combine_norm",
    )(tile, blk, flag, tok3, yg, x1, g)


def _group_forward(x, p):
    b, seq, _ = x.shape
    n = b * seq
    cap = CAP_FACTOR * n // N_EXPERTS
    x2 = x.reshape(n, D_MODEL)
    cos_t, sa_t, sb_t = p["rot"]
    qa, ka, va, qb, kb, vb = _inproj(x2, p["g_attn"], p["w_in"], cos_t[:seq], sa_t[:seq], sb_t[:seq], seq)
    r3 = lambda a: a.reshape(b, seq, a.shape[-1])
    oa = _window(r3(qa), r3(ka), r3(va), p["sink"], p["g_out_a"])
    ob = _natten(r3(qb), r3(kb), r3(vb), _natten_tables(p["rpb"], seq // GRID_W), p["g_out_b"])
    x1, h, aff_t = _outproj(x2, oa.reshape(n, DQA), ob.reshape(n, DB), p["w_out"], p["g_ffn"], p["w_router_t"])

    _, idx = lax.top_k(aff_t, cap)
    idx = jnp.sort(idx, axis=1)
    gate = jnp.take_along_axis(aff_t, idx, axis=1)
    flat = idx.reshape(-1)
    xe = jnp.take(h, flat, axis=0).reshape(N_EXPERTS, cap, D_MODEL)
    ye = _moe(xe, gate.reshape(N_EXPERTS, cap, 1), p["w_gate"], p["w_up"], p["w_down"])

    perm = jnp.argsort(flat, stable=True).astype(jnp.int32)
    tok_sorted = flat[perm]
    yg = jnp.take(ye.reshape(N_EXPERTS * cap, D_MODEL), perm, axis=0)
    tile, blk, flag = _combine_schedule(tok_sorted, n)
    y = _combine(tile, blk, flag, tok_sorted.reshape(-1, 1, SB), yg, x1, p["g_final"])
    return y.reshape(b, seq, D_MODEL)


def kernel(x_prompt, x_sample, g_attn, w_in, g_out_a, g_out_b, sink_a, rpb_b, w_out, g_ffn, w_router,
           w_gate, w_up, w_down, g_final):
    assert g_attn.shape[0] == 1, "single trunk layer"
    wr = w_router[0].T
    wr_hi = wr.astype(BF16)
    wr_lo = (wr - wr_hi.astype(F32)).astype(BF16)
    p = {
        "g_attn": g_attn[0][None, :], "w_in": w_in[0].astype(BF16),
        "g_out_a": g_out_a[0][None, :], "g_out_b": g_out_b[0][None, :],
        "sink": sink_a[0], "rpb": rpb_b[0], "w_out": w_out[0].astype(BF16),
        "g_ffn": g_ffn[0][None, :], "w_router_t": jnp.concatenate([wr_hi, wr_lo], axis=0),
        "w_gate": w_gate[0].astype(BF16), "w_up": w_up[0].astype(BF16), "w_down": w_down[0].astype(BF16),
        "g_final": g_final[None, :],
        "rot": _rotary_tables(max(x_prompt.shape[1], x_sample.shape[1])),
    }
    return (_group_forward(x_prompt, p), _group_forward(x_sample, p))
```

```python
import functools

import jax
import jax.numpy as jnp
import numpy as np
from jax import lax
from jax.experimental import pallas as pl
from jax.experimental.pallas import tpu as pltpu

D_MODEL = 1024
HEAD_DIM = 64
HQ_A = 8
HKV_A = 2
H_B = 8
GRP = HQ_A // HKV_A
DQA = HQ_A * HEAD_DIM
DKA = HKV_A * HEAD_DIM
DB = H_B * HEAD_DIM
D_IN = DQA + 2 * DKA + 3 * DB
WINDOW = 128
BLK = 128
ROPE_THETA = 500000.0
ROT_DIM = HEAD_DIM // 4
GRID_W = 64
NA_ROWS = 8
NA_COLS = 16
NA_QROWS = 2
NA_KROWS = NA_ROWS + NA_QROWS - 1
NA_KEYS = NA_KROWS * GRID_W
NA_Q = NA_QROWS * GRID_W
N_EXPERTS = 16
CAP_FACTOR = 2
D_EXPERT = 2 * D_MODEL
RMS_EPS = 1e-6
NEG_INF = -1e30
SCALE = HEAD_DIM ** -0.5

LANES = 128
TM = 512
TQ = 512
TC = 512
FC = 512
TS = 256
SB = 256
VMEM_LIMIT = 56 * 1024 * 1024

BF16 = jnp.bfloat16
F32 = jnp.float32
_NT = (((1,), (1,)), ((), ()))


def _cparams(sem):
    return pltpu.CompilerParams(dimension_semantics=sem, vmem_limit_bytes=VMEM_LIMIT)


def _inproj_kernel(x_ref, g_ref, w_ref, cos_ref, sa_ref, sb_ref,
                   qa_ref, ka_ref, va_ref, qb_ref, kb_ref, vb_ref):
    x = x_ref[...]
    ms = jnp.mean(x * x, axis=-1, keepdims=True)
    hn = (x * lax.rsqrt(ms + RMS_EPS) * g_ref[...]).astype(BF16)
    c, sa, sb = cos_ref[...], sa_ref[...], sb_ref[...]

    def proj(lo, n):
        return jnp.dot(hn, w_ref[:, lo:lo + n], preferred_element_type=F32)

    def rot(blk):
        return blk * c + pltpu.roll(blk, LANES - ROT_DIM // 2, 1) * sa + pltpu.roll(blk, ROT_DIM // 2, 1) * sb

    for j in range(DQA // LANES):
        qa_ref[:, j * LANES:(j + 1) * LANES] = (rot(proj(j * LANES, LANES)) * SCALE).astype(BF16)
    ka_ref[...] = rot(proj(DQA, DKA)).astype(BF16)
    va_ref[...] = proj(DQA + DKA, DKA).astype(BF16)
    qb_ref[...] = (proj(DQA + 2 * DKA, DB) * SCALE).astype(BF16)
    kb_ref[...] = proj(DQA + 2 * DKA + DB, DB).astype(BF16)
    vb_ref[...] = proj(DQA + 2 * DKA + 2 * DB, DB).astype(BF16)


def _inproj(x2, g, w_bf, cos_t, sa_t, sb_t, seq):
    n = x2.shape[0]
    per_seq = seq // TM
    tok = lambda w: pl.BlockSpec((TM, w), lambda i: (i, 0))
    full = lambda a: pl.BlockSpec(a.shape, lambda i: (0,) * a.ndim)
    pos = pl.BlockSpec((TM, LANES), lambda i: (i % per_seq, 0))
    widths = (DQA, DKA, DKA, DB, DB, DB)
    return pl.pallas_call(
        _inproj_kernel,
        grid=(n // TM,),
        in_specs=[tok(D_MODEL), full(g), full(w_bf), pos, pos, pos],
        out_specs=[tok(w) for w in widths],
        out_shape=[jax.ShapeDtypeStruct((n, w), BF16) for w in widths],
        compiler_params=_cparams(("parallel",)),
        name="inproj",
    )(x2, g, w_bf, cos_t, sa_t, sb_t)


def _rotary_tables(seq):
    half = ROT_DIM // 2
    inv_freq = jnp.float32(ROPE_THETA) ** (-(jnp.arange(half, dtype=F32) * 2.0) / ROT_DIM)
    ang = jnp.arange(seq, dtype=F32)[:, None] * inv_freq[None, :]
    cos, sin = jnp.cos(ang), jnp.sin(ang)
    ones = jnp.ones((seq, HEAD_DIM - ROT_DIM), F32)
    zeros = jnp.zeros((seq, HEAD_DIM - ROT_DIM), F32)
    zh = jnp.zeros((seq, half), F32)
    rep = LANES // HEAD_DIM
    cos_t = jnp.tile(jnp.concatenate([cos, cos, ones], axis=1), (1, rep))
    sa_t = jnp.tile(jnp.concatenate([-sin, zh, zeros], axis=1), (1, rep))
    sb_t = jnp.tile(jnp.concatenate([zh, sin, zeros], axis=1), (1, rep))
    return cos_t, sa_t, sb_t


def _window_kernel(q_ref, kp_ref, kc_ref, kn_ref, vp_ref, vc_ref, vn_ref, sink_ref, g_ref, o_ref,
                   k_ref, v_ref, s_ref, p_ref, *, nb):
    i = pl.program_id(1)
    nsub = TQ // BLK
    k_ref[0:BLK] = kp_ref[0]
    k_ref[BLK:BLK + TQ] = kc_ref[0]
    k_ref[BLK + TQ:TQ + 2 * BLK] = kn_ref[0]
    v_ref[0:BLK] = vp_ref[0]
    v_ref[BLK:BLK + TQ] = vc_ref[0]
    v_ref[BLK + TQ:TQ + 2 * BLK] = vn_ref[0]
    qi = lax.broadcasted_iota(jnp.int32, (BLK, 3 * BLK), 0)
    kj = lax.broadcasted_iota(jnp.int32, (BLK, 3 * BLK), 1)
    band = jnp.abs(kj - BLK - qi) <= WINDOW
    sink = sink_ref[...]
    for j in range(nsub):
        n = i * nsub + j
        k_lo = jnp.where(n >= 1, 0, BLK)
        k_hi = jnp.where(n <= nb - 2, 3 * BLK, 2 * BLK)
        bias = jnp.where(band & (kj >= k_lo) & (kj < k_hi), 0.0, NEG_INF)
        rows = slice(j * BLK, (j + 1) * BLK)
        keys = slice(j * BLK, (j + 3) * BLK)
        for h in range(HKV_A):
            kh = k_ref[keys, h * HEAD_DIM:(h + 1) * HEAD_DIM]
            qs = jnp.concatenate(
                [q_ref[0, rows, hd * HEAD_DIM:(hd + 1) * HEAD_DIM] for hd in range(h * GRP, (h + 1) * GRP)], axis=0)
            s = lax.dot_general(qs, kh, _NT, preferred_element_type=F32)
            s_ref[h * GRP:(h + 1) * GRP] = s.reshape(GRP, BLK, 3 * BLK)
        s = s_ref[...] + bias[None]
        m = jnp.maximum(jnp.max(s, axis=-1, keepdims=True), sink)
        p = jnp.exp(s - m)
        denom = jnp.sum(p, axis=-1, keepdims=True) + jnp.exp(sink - m)
        p_ref[...] = p.astype(BF16)
        outs = []
        for h in range(HKV_A):
            vh = v_ref[keys, h * HEAD_DIM:(h + 1) * HEAD_DIM]
            ph = p_ref[h * GRP:(h + 1) * GRP].reshape(GRP * BLK, 3 * BLK)
            o = jnp.dot(ph, vh, preferred_element_type=F32)
            outs += [o[g * BLK:(g + 1) * BLK] / denom[h * GRP + g] for g in range(GRP)]
        ob = jnp.concatenate(outs, axis=1)
        ms = jnp.mean(ob * ob, axis=-1, keepdims=True)
        o_ref[0, rows, :] = (ob * lax.rsqrt(ms + RMS_EPS) * g_ref[...]).astype(BF16)


def _window(qa, ka, va, sink, g):
    b, seq, _ = qa.shape
    nb = seq // BLK
    r = TQ // BLK
    cur = lambda w: pl.BlockSpec((1, TQ, w), lambda bi, i: (bi, i, 0))
    prev = pl.BlockSpec((1, BLK, DKA), lambda bi, i: (bi, jnp.maximum(i * r - 1, 0), 0))
    nxt = pl.BlockSpec((1, BLK, DKA), lambda bi, i: (bi, jnp.minimum((i + 1) * r, nb - 1), 0))
    sink3 = sink.reshape(HQ_A, 1, 1)
    return pl.pallas_call(
        functools.partial(_window_kernel, nb=nb),
        grid=(b, seq // TQ),
        in_specs=[cur(DQA), prev, cur(DKA), nxt, prev, cur(DKA), nxt,
                  pl.BlockSpec(sink3.shape, lambda bi, i: (0, 0, 0)), pl.BlockSpec(g.shape, lambda bi, i: (0, 0))],
        out_specs=cur(DQA),
        out_shape=jax.ShapeDtypeStruct((b, seq, DQA), BF16),
        scratch_shapes=[pltpu.VMEM((TQ + 2 * BLK, DKA), BF16), pltpu.VMEM((TQ + 2 * BLK, DKA), BF16),
                        pltpu.VMEM((HQ_A, BLK, 3 * BLK), F32), pltpu.VMEM((HQ_A, BLK, 3 * BLK), BF16)],
        compiler_params=_cparams(("parallel", "parallel")),
        name="window_attn",
    )(qa, ka, ka, ka, va, va, va, sink3, g)


def _natten_kernel(q_ref, k_ref, v_ref, tab_ref, g_ref, o_ref, s_ref, p_ref, *, rows):
    rb = pl.program_id(1)
    kstart = jnp.clip(NA_QROWS * rb - NA_ROWS // 2, 0, rows - NA_KROWS)
    keys = pl.ds(pl.multiple_of(kstart * GRID_W, GRID_W), NA_KEYS)
    for h in range(H_B):
        sl = slice(h * HEAD_DIM, (h + 1) * HEAD_DIM)
        s_ref[h] = lax.dot_general(q_ref[0, :, sl], k_ref[0, keys, sl], _NT, preferred_element_type=F32)
    s = s_ref[...] + tab_ref[0]
    m = jnp.max(s, axis=-1, keepdims=True)
    p = jnp.exp(s - m)
    denom = jnp.sum(p, axis=-1, keepdims=True)
    p_ref[...] = p.astype(BF16)
    outs = []
    for h in range(H_B):
        sl = slice(h * HEAD_DIM, (h + 1) * HEAD_DIM)
        outs.append(jnp.dot(p_ref[h], v_ref[0, keys, sl], preferred_element_type=F32) / denom[h])
    ob = jnp.concatenate(outs, axis=1)
    ms = jnp.mean(ob * ob, axis=-1, keepdims=True)
    o_ref[0] = (ob * lax.rsqrt(ms + RMS_EPS) * g_ref[...]).astype(BF16)


def _natten_variant_rowblocks(nrb):
    return (0, 1, 2, nrb - 2, nrb - 1)


def _natten_tables(rpb, rows):
    nrb = rows // NA_QROWS
    qc = np.arange(GRID_W)
    kc = np.arange(GRID_W)
    cs = np.clip(qc - NA_COLS // 2, 0, GRID_W - NA_COLS)
    col_ok = (kc[None, :] >= cs[:, None]) & (kc[None, :] < cs[:, None] + NA_COLS)
    cidx = np.clip(kc[None, :] - qc[:, None] + NA_COLS - 1, 0, 2 * NA_COLS - 2)
    tabs = []
    for rb in _natten_variant_rowblocks(nrb):
        r0 = rb * NA_QROWS
        qrows = r0 + np.arange(NA_QROWS)
        rs = np.clip(qrows - NA_ROWS // 2, 0, rows - NA_ROWS)
        kstart = int(np.clip(r0 - NA_ROWS // 2, 0, rows - NA_KROWS))
        krows = kstart + np.arange(NA_KROWS)
        row_ok = (krows[None, :] >= rs[:, None]) & (krows[None, :] < rs[:, None] + NA_ROWS)
        ridx = np.clip(krows[None, :] - qrows[:, None] + NA_ROWS - 1, 0, 2 * NA_ROWS - 2)
        valid = row_ok[:, None, :, None] & col_ok[None, :, None, :]
        bias = rpb.astype(F32)[:, ridx[:, None, :, None], cidx[None, :, None, :]]
        bias = jnp.where(valid[None], bias, NEG_INF)
        tabs.append(bias.reshape(H_B, NA_Q, NA_KEYS))
    return jnp.stack(tabs)


def _natten(qb, kb, vb, tabs, g):
    b, seq, _ = qb.shape
    rows = seq // GRID_W
    nrb = rows // NA_QROWS
    assert rows >= NA_KROWS and nrb >= 5

    def variant(rb):
        return jnp.where(rb < 2, rb, jnp.where(rb >= nrb - 2, rb - (nrb - 5), 2))

    whole = pl.BlockSpec((1, seq, DB), lambda bi, rb: (bi, 0, 0))
    blk = pl.BlockSpec((1, NA_Q, DB), lambda bi, rb: (bi, rb, 0))
    return pl.pallas_call(
        functools.partial(_natten_kernel, rows=rows),
        grid=(b, nrb),
        in_specs=[blk, whole, whole,
                  pl.BlockSpec((1, H_B, NA_Q, NA_KEYS), lambda bi, rb: (variant(rb), 0, 0, 0)),
                  pl.BlockSpec(g.shape, lambda bi, rb: (0, 0))],
        out_specs=blk,
        out_shape=jax.ShapeDtypeStruct((b, seq, DB), BF16),
        scratch_shapes=[pltpu.VMEM((H_B, NA_Q, NA_KEYS), F32), pltpu.VMEM((H_B, NA_Q, NA_KEYS), BF16)],
        compiler_params=_cparams(("parallel", "arbitrary")),
        name="natten",
    )(qb, kb, vb, tabs, g)


def _outproj_kernel(x_ref, oa_ref, ob_ref, wo_ref, g_ref, wr_ref, x1_ref, h_ref, aff_ref):
    x1 = (x_ref[...]
          + jnp.dot(oa_ref[...], wo_ref[0:DQA, :], preferred_element_type=F32)
          + jnp.dot(ob_ref[...], wo_ref[DQA:DQA + DB, :], preferred_element_type=F32))
    x1_ref[...] = x1
    ms = jnp.mean(x1 * x1, axis=-1, keepdims=True)
    hf = x1 * lax.rsqrt(ms + RMS_EPS) * g_ref[...]
    h_hi = hf.astype(BF16)
    h_lo = (hf - h_hi.astype(F32)).astype(BF16)
    h_ref[...] = h_hi
    l_hi = lax.dot_general(wr_ref[...], h_hi, _NT, preferred_element_type=F32)
    l_lo = lax.dot_general(wr_ref[0:N_EXPERTS, :], h_lo, _NT, preferred_element_type=F32)
    logits = l_hi[0:N_EXPERTS] + l_hi[N_EXPERTS:2 * N_EXPERTS] + l_lo
    m = jnp.max(logits, axis=0, keepdims=True)
    e = jnp.exp(logits - m)
    aff_ref[...] = e / jnp.sum(e, axis=0, keepdims=True)


def _outproj(x2, oa, ob, wo_bf, g, wr_t):
    n = x2.shape[0]
    tok = lambda w: pl.BlockSpec((TM, w), lambda i: (i, 0))
    full = lambda a: pl.BlockSpec(a.shape, lambda i: (0,) * a.ndim)
    return pl.pallas_call(
        _outproj_kernel,
        grid=(n // TM,),
        in_specs=[tok(D_MODEL), tok(DQA), tok(DB), full(wo_bf), full(g), full(wr_t)],
        out_specs=[tok(D_MODEL), tok(D_MODEL), pl.BlockSpec((N_EXPERTS, TM), lambda i: (0, i))],
        out_shape=[jax.ShapeDtypeStruct((n, D_MODEL), F32), jax.ShapeDtypeStruct((n, D_MODEL), BF16),
                   jax.ShapeDtypeStruct((N_EXPERTS, n), F32)],
        compiler_params=_cparams(("parallel",)),
        name="outproj_router",
    )(x2, oa, ob, wo_bf, g, wr_t)


def _moe_kernel(x_ref, gate_ref, wg_ref, wu_ref, wd_ref, o_ref):
    x = x_ref[0]
    y = jnp.zeros((x.shape[0], D_MODEL), F32)
    for c in range(D_EXPERT // FC):
        sl = slice(c * FC, (c + 1) * FC)
        a = jnp.dot(x, wg_ref[0, :, sl], preferred_element_type=F32)
        u = jnp.dot(x, wu_ref[0, :, sl], preferred_element_type=F32)
        hid = (a * jax.nn.sigmoid(a) * u).astype(BF16)
        y = y + jnp.dot(hid, wd_ref[0, sl, :], preferred_element_type=F32)
    o_ref[0] = (y * gate_ref[0]).astype(BF16)


def _moe(xe, gate3, wg, wu, wd):
    e, cap, _ = xe.shape
    tc = min(TC, cap)
    tokb = lambda w: pl.BlockSpec((1, tc, w), lambda ei, i: (ei, i, 0))
    wspec = lambda a: pl.BlockSpec((1,) + a.shape[1:], lambda ei, i: (ei, 0, 0))
    return pl.pallas_call(
        _moe_kernel,
        grid=(e, cap // tc),
        in_specs=[tokb(D_MODEL), tokb(1), wspec(wg), wspec(wu), wspec(wd)],
        out_specs=tokb(D_MODEL),
        out_shape=jax.ShapeDtypeStruct((e, cap, D_MODEL), BF16),
        compiler_params=_cparams(("parallel", "arbitrary")),
        name="moe_ffn",
    )(xe, gate3, wg, wu, wd)


_FIRST, _LAST, _ACTIVE = 1, 2, 4


def _combine_kernel(tile_ref, blk_ref, flag_ref, tok_ref, yg_ref, x1_ref, g_ref, o_ref, acc_ref):
    s = pl.program_id(0)
    flag = flag_ref[s]

    @pl.when((flag & _FIRST) != 0)
    def _():
        acc_ref[...] = jnp.zeros_like(acc_ref)

    @pl.when((flag & _ACTIVE) != 0)
    def _():
        tok_row = tile_ref[s] * TS + lax.broadcasted_iota(jnp.int32, (TS, SB), 0)
        onehot = jnp.where(tok_ref[0] == tok_row, 1.0, 0.0).astype(BF16)
        acc_ref[...] += jnp.dot(onehot, yg_ref[...], preferred_element_type=F32)

    @pl.when((flag & _LAST) != 0)
    def _():
        y = x1_ref[...] + acc_ref[...]
        ms = jnp.mean(y * y, axis=-1, keepdims=True)
        o_ref[...] = y * lax.rsqrt(ms + RMS_EPS) * g_ref[...]


def _combine_schedule(tok_sorted, n):
    nt, nblk = n // TS, tok_sorted.shape[0] // SB
    steps = nt + nblk
    bounds = jnp.searchsorted(tok_sorted, jnp.arange(nt + 1, dtype=jnp.int32) * TS, side="left").astype(jnp.int32)
    lo, hi = bounds[:-1], bounds[1:]
    b_lo = jnp.minimum(lo // SB, nblk - 1)
    b_hi = jnp.where(hi > lo, (hi - 1) // SB, b_lo)
    nst = b_hi - b_lo + 1
    cum = jnp.cumsum(nst)
    start = cum - nst
    s = jnp.arange(steps, dtype=jnp.int32)
    tile = jnp.minimum(jnp.searchsorted(cum, s, side="right").astype(jnp.int32), nt - 1)
    active = s < cum[-1]
    blk = jnp.where(active, b_lo[tile] + s - start[tile], b_hi[nt - 1])
    first = active & (s == start[tile])
    last = active & (s == cum[tile] - 1)
    flag = first * _FIRST + last * _LAST + active * _ACTIVE
    return tile, blk.astype(jnp.int32), flag.astype(jnp.int32)


def _combine(tile, blk, flag, tok3, yg, x1, g):
    n = x1.shape[0]
    steps = tile.shape[0]
    grid_spec = pltpu.PrefetchScalarGridSpec(
        num_scalar_prefetch=3,
        grid=(steps,),
        in_specs=[pl.BlockSpec((1, 1, SB), lambda s, t, b, f: (b[s], 0, 0)),
                  pl.BlockSpec((SB, D_MODEL), lambda s, t, b, f: (b[s], 0)),
                  pl.BlockSpec((TS, D_MODEL), lambda s, t, b, f: (t[s], 0)),
                  pl.BlockSpec(g.shape, lambda s, t, b, f: (0, 0))],
        out_specs=pl.BlockSpec((TS, D_MODEL), lambda s, t, b, f: (t[s], 0)),
        scratch_shapes=[pltpu.VMEM((TS, D_MODEL), F32)],
    )
    return pl.pallas_call(
        _combine_kernel,
        grid_spec=grid_spec,
        out_shape=jax.ShapeDtypeStruct((n, D_MODEL), F32),
        compiler_params=_cparams(("arbitrary",)),
        name="combine_norm",
    )(tile, blk, flag, tok3, yg, x1, g)


def _group_forward(x, p):
    b, seq, _ = x.shape
    n = b * seq
    cap = CAP_FACTOR * n // N_EXPERTS
    x2 = x.reshape(n, D_MODEL)
    cos_t, sa_t, sb_t = p["rot"]
    qa, ka, va, qb, kb, vb = _inproj(x2, p["g_attn"], p["w_in"], cos_t[:seq], sa_t[:seq], sb_t[:seq], seq)
    r3 = lambda a: a.reshape(b, seq, a.shape[-1])
    oa = _window(r3(qa), r3(ka), r3(va), p["sink"], p["g_out_a"])
    ob = _natten(r3(qb), r3(kb), r3(vb), _natten_tables(p["rpb"], seq // GRID_W), p["g_out_b"])
    x1, h, aff_t = _outproj(x2, oa.reshape(n, DQA), ob.reshape(n, DB), p["w_out"], p["g_ffn"], p["w_router_t"])

    _, idx = lax.top_k(aff_t, cap)
    idx = jnp.sort(idx, axis=1)
    gate = jnp.take_along_axis(aff_t, idx, axis=1)
    flat = idx.reshape(-1)
    xe = jnp.take(h, flat, axis=0).reshape(N_EXPERTS, cap, D_MODEL)
    ye = _moe(xe, gate.reshape(N_EXPERTS, cap, 1), p["w_gate"], p["w_up"], p["w_down"])

    perm = jnp.argsort(flat, stable=True).astype(jnp.int32)
    tok_sorted = flat[perm]
    yg = jnp.take(ye.reshape(N_EXPERTS * cap, D_MODEL), perm, axis=0)
    tile, blk, flag = _combine_schedule(tok_sorted, n)
    y = _combine(tile, blk, flag, tok_sorted.reshape(-1, 1, SB), yg, x1, p["g_final"])
    return y.reshape(b, seq, D_MODEL)


def kernel(x_prompt, x_sample, g_attn, w_in, g_out_a, g_out_b, sink_a, rpb_b, w_out, g_ffn, w_router,
           w_gate, w_up, w_down, g_final):
    assert g_attn.shape[0] == 1, "single trunk layer"
    wr = w_router[0].T
    wr_hi = wr.astype(BF16)
    wr_lo = (wr - wr_hi.astype(F32)).astype(BF16)
    p = {
        "g_attn": g_attn[0][None, :], "w_in": w_in[0].astype(BF16),
        "g_out_a": g_out_a[0][None, :], "g_out_b": g_out_b[0][None, :],
        "sink": sink_a[0], "rpb": rpb_b[0], "w_out": w_out[0].astype(BF16),
        "g_ffn": g_ffn[0][None, :], "w_router_t": jnp.concatenate([wr_hi, wr_lo], axis=0),
        "w_gate": w_gate[0].astype(BF16), "w_up": w_up[0].astype(BF16), "w_down": w_down[0].astype(BF16),
        "g_final": g_final[None, :],
        "rot": _rotary_tables(max(x_prompt.shape[1], x_sample.shape[1])),
    }
    return (_group_forward(x_prompt, p), _group_forward(x_sample, p))
```

```python
import functools

import jax
import jax.numpy as jnp
import numpy as np
from jax import lax
from jax.experimental import pallas as pl
from jax.experimental.pallas import tpu as pltpu
from jax.experimental.pallas import tpu_sc as plsc

D_MODEL = 1024
HEAD_DIM = 64
HQ_A = 8
HKV_A = 2
H_B = 8
GRP = HQ_A // HKV_A
DQA = HQ_A * HEAD_DIM
DKA = HKV_A * HEAD_DIM
DB = H_B * HEAD_DIM
D_IN = DQA + 2 * DKA + 3 * DB
WINDOW = 128
BLK = 128
ROPE_THETA = 500000.0
ROT_DIM = HEAD_DIM // 4
GRID_W = 64
NA_ROWS = 8
NA_COLS = 16
NA_QROWS = 2
NA_KROWS = NA_ROWS + NA_QROWS - 1
NA_KEYS = NA_KROWS * GRID_W
NA_Q = NA_QROWS * GRID_W
N_EXPERTS = 16
CAP_FACTOR = 2
D_EXPERT = 2 * D_MODEL
RMS_EPS = 1e-6
NEG_INF = -1e30
SCALE = HEAD_DIM ** -0.5

LANES = 128
TM = 512
TQ = 512
TC = 512
FC = 512
TS = 256
SB = 256
VMEM_LIMIT = 56 * 1024 * 1024

BF16 = jnp.bfloat16
F32 = jnp.float32
_NT = (((1,), (1,)), ((), ()))


def _cparams(sem):
    return pltpu.CompilerParams(dimension_semantics=sem, vmem_limit_bytes=VMEM_LIMIT)


HALF = D_MODEL // 2
U32 = jnp.uint32


def _pack_rows(v):
    r = v.astype(BF16).astype(F32)
    hi = pltpu.bitcast(r[:, :HALF], U32)
    lo = pltpu.bitcast(r[:, HALF:], U32)
    return (hi & jnp.uint32(0xFFFF0000)) | (lo >> 16)


def _unpack_rows(w):
    hi = pltpu.bitcast(w & jnp.uint32(0xFFFF0000), F32)
    lo = pltpu.bitcast(w << 16, F32)
    return jnp.concatenate([hi, lo], axis=1).astype(BF16)


SC_CORES = 2
SC_SUBCORES = 16
SC_WORKERS = SC_CORES * SC_SUBCORES
GATHER_ROWS = 64


def _sc_gather_rows(table, idx):
    m, w = idx.shape[0], table.shape[1]
    nchunks = m // GATHER_ROWS
    per_worker = nchunks // SC_WORKERS
    assert per_worker * SC_WORKERS * GATHER_ROWS == m and per_worker % 8 == 0
    mesh = plsc.VectorSubcoreMesh(core_axis_name="c", subcore_axis_name="s")

    @functools.partial(
        pl.kernel, mesh=mesh, out_type=jax.ShapeDtypeStruct((m, w), table.dtype),
        scratch_types=[pltpu.VMEM((per_worker, GATHER_ROWS), jnp.int32),
                       pltpu.VMEM((GATHER_ROWS, w), table.dtype),
                       pltpu.SemaphoreType.DMA])
    def gather(tab_hbm, idx_hbm, out_hbm, idx_v, rows_v, sem):
        wid = lax.axis_index("s") * SC_CORES + lax.axis_index("c")
        first = wid * per_worker
        pltpu.sync_copy(idx_hbm.at[pl.ds(first, per_worker)], idx_v)

        @pl.loop(0, per_worker)
        def _(c):
            pltpu.async_copy(tab_hbm.at[idx_v.at[c]], rows_v, sem).wait()
            pltpu.sync_copy(rows_v, out_hbm.at[pl.ds((first + c) * GATHER_ROWS, GATHER_ROWS)])

    return gather(table, idx.reshape(nchunks, GATHER_ROWS))


def _inproj_kernel(x_ref, g_ref, w_ref, cos_ref, sa_ref, sb_ref,
                   qa_ref, ka_ref, va_ref, qb_ref, kb_ref, vb_ref):
    x = x_ref[...]
    ms = jnp.mean(x * x, axis=-1, keepdims=True)
    hn = (x * lax.rsqrt(ms + RMS_EPS) * g_ref[...]).astype(BF16)
    c, sa, sb = cos_ref[...], sa_ref[...], sb_ref[...]

    def proj(lo, n):
        return jnp.dot(hn, w_ref[:, lo:lo + n], preferred_element_type=F32)

    def rot(blk):
        return blk * c + pltpu.roll(blk, LANES - ROT_DIM // 2, 1) * sa + pltpu.roll(blk, ROT_DIM // 2, 1) * sb

    for j in range(DQA // LANES):
        qa_ref[:, j * LANES:(j + 1) * LANES] = (rot(proj(j * LANES, LANES)) * SCALE).astype(BF16)
    ka_ref[...] = rot(proj(DQA, DKA)).astype(BF16)
    va_ref[...] = proj(DQA + DKA, DKA).astype(BF16)
    qb_ref[...] = (proj(DQA + 2 * DKA, DB) * SCALE).astype(BF16)
    kb_ref[...] = proj(DQA + 2 * DKA + DB, DB).astype(BF16)
    vb_ref[...] = proj(DQA + 2 * DKA + 2 * DB, DB).astype(BF16)


def _inproj(x2, g, w_bf, cos_t, sa_t, sb_t, seq):
    n = x2.shape[0]
    per_seq = seq // TM
    tok = lambda w: pl.BlockSpec((TM, w), lambda i: (i, 0))
    full = lambda a: pl.BlockSpec(a.shape, lambda i: (0,) * a.ndim)
    pos = pl.BlockSpec((TM, LANES), lambda i: (i % per_seq, 0))
    widths = (DQA, DKA, DKA, DB, DB, DB)
    return pl.pallas_call(
        _inproj_kernel,
        grid=(n // TM,),
        in_specs=[tok(D_MODEL), full(g), full(w_bf), pos, pos, pos],
        out_specs=[tok(w) for w in widths],
        out_shape=[jax.ShapeDtypeStruct((n, w), BF16) for w in widths],
        compiler_params=_cparams(("parallel",)),
        name="inproj",
    )(x2, g, w_bf, cos_t, sa_t, sb_t)


def _rotary_tables(seq):
    half = ROT_DIM // 2
    inv_freq = jnp.float32(ROPE_THETA) ** (-(jnp.arange(half, dtype=F32) * 2.0) / ROT_DIM)
    ang = jnp.arange(seq, dtype=F32)[:, None] * inv_freq[None, :]
    cos, sin = jnp.cos(ang), jnp.sin(ang)
    ones = jnp.ones((seq, HEAD_DIM - ROT_DIM), F32)
    zeros = jnp.zeros((seq, HEAD_DIM - ROT_DIM), F32)
    zh = jnp.zeros((seq, half), F32)
    rep = LANES // HEAD_DIM
    cos_t = jnp.tile(jnp.concatenate([cos, cos, ones], axis=1), (1, rep))
    sa_t = jnp.tile(jnp.concatenate([-sin, zh, zeros], axis=1), (1, rep))
    sb_t = jnp.tile(jnp.concatenate([zh, sin, zeros], axis=1), (1, rep))
    return cos_t, sa_t, sb_t


def _window_kernel(q_ref, kp_ref, kc_ref, kn_ref, vp_ref, vc_ref, vn_ref, sink_ref, g_ref, o_ref,
                   k_ref, v_ref, s_ref, p_ref, *, nb):
    i = pl.program_id(1)
    nsub = TQ // BLK
    k_ref[0:BLK] = kp_ref[0]
    k_ref[BLK:BLK + TQ] = kc_ref[0]
    k_ref[BLK + TQ:TQ + 2 * BLK] = kn_ref[0]
    v_ref[0:BLK] = vp_ref[0]
    v_ref[BLK:BLK + TQ] = vc_ref[0]
    v_ref[BLK + TQ:TQ + 2 * BLK] = vn_ref[0]
    qi = lax.broadcasted_iota(jnp.int32, (BLK, 3 * BLK), 0)
    kj = lax.broadcasted_iota(jnp.int32, (BLK, 3 * BLK), 1)
    band = jnp.abs(kj - BLK - qi) <= WINDOW
    sink = sink_ref[...]
    for j in range(nsub):
        n = i * nsub + j
        k_lo = jnp.where(n >= 1, 0, BLK)
        k_hi = jnp.where(n <= nb - 2, 3 * BLK, 2 * BLK)
        bias = jnp.where(band & (kj >= k_lo) & (kj < k_hi), 0.0, NEG_INF)
        rows = slice(j * BLK, (j + 1) * BLK)
        keys = slice(j * BLK, (j + 3) * BLK)
        for h in range(HKV_A):
            kh = k_ref[keys, h * HEAD_DIM:(h + 1) * HEAD_DIM]
            qs = jnp.concatenate(
                [q_ref[0, rows, hd * HEAD_DIM:(hd + 1) * HEAD_DIM] for hd in range(h * GRP, (h + 1) * GRP)], axis=0)
            s = lax.dot_general(qs, kh, _NT, preferred_element_type=F32)
            s_ref[h * GRP:(h + 1) * GRP] = s.reshape(GRP, BLK, 3 * BLK)
        s = s_ref[...] + bias[None]
        m = jnp.maximum(jnp.max(s, axis=-1, keepdims=True), sink)
        p = jnp.exp(s - m)
        denom = jnp.sum(p, axis=-1, keepdims=True) + jnp.exp(sink - m)
        p_ref[...] = p.astype(BF16)
        outs = []
        for h in range(HKV_A):
            vh = v_ref[keys, h * HEAD_DIM:(h + 1) * HEAD_DIM]
            ph = p_ref[h * GRP:(h + 1) * GRP].reshape(GRP * BLK, 3 * BLK)
            o = jnp.dot(ph, vh, preferred_element_type=F32)
            outs += [o[g * BLK:(g + 1) * BLK] / denom[h * GRP + g] for g in range(GRP)]
        ob = jnp.concatenate(outs, axis=1)
        ms = jnp.mean(ob * ob, axis=-1, keepdims=True)
        o_ref[0, rows, :] = (ob * lax.rsqrt(ms + RMS_EPS) * g_ref[...]).astype(BF16)


def _window(qa, ka, va, sink, g):
    b, seq, _ = qa.shape
    nb = seq // BLK
    r = TQ // BLK
    cur = lambda w: pl.BlockSpec((1, TQ, w), lambda bi, i: (bi, i, 0))
    prev = pl.BlockSpec((1, BLK, DKA), lambda bi, i: (bi, jnp.maximum(i * r - 1, 0), 0))
    nxt = pl.BlockSpec((1, BLK, DKA), lambda bi, i: (bi, jnp.minimum((i + 1) * r, nb - 1), 0))
    sink3 = sink.reshape(HQ_A, 1, 1)
    return pl.pallas_call(
        functools.partial(_window_kernel, nb=nb),
        grid=(b, seq // TQ),
        in_specs=[cur(DQA), prev, cur(DKA), nxt, prev, cur(DKA), nxt,
                  pl.BlockSpec(sink3.shape, lambda bi, i: (0, 0, 0)), pl.BlockSpec(g.shape, lambda bi, i: (0, 0))],
        out_specs=cur(DQA),
        out_shape=jax.ShapeDtypeStruct((b, seq, DQA), BF16),
        scratch_shapes=[pltpu.VMEM((TQ + 2 * BLK, DKA), BF16), pltpu.VMEM((TQ + 2 * BLK, DKA), BF16),
                        pltpu.VMEM((HQ_A, BLK, 3 * BLK), F32), pltpu.VMEM((HQ_A, BLK, 3 * BLK), BF16)],
        compiler_params=_cparams(("parallel", "parallel")),
        name="window_attn",
    )(qa, ka, ka, ka, va, va, va, sink3, g)


def _natten_kernel(q_ref, k_ref, v_ref, tab_ref, g_ref, o_ref, s_ref, p_ref, *, rows):
    rb = pl.program_id(1)
    kstart = jnp.clip(NA_QROWS * rb - NA_ROWS // 2, 0, rows - NA_KROWS)
    keys = pl.ds(pl.multiple_of(kstart * GRID_W, GRID_W), NA_KEYS)
    for h in range(H_B):
        sl = slice(h * HEAD_DIM, (h + 1) * HEAD_DIM)
        s_ref[h] = lax.dot_general(q_ref[0, :, sl], k_ref[0, keys, sl], _NT, preferred_element_type=F32)
    s = s_ref[...] + tab_ref[0]
    m = jnp.max(s, axis=-1, keepdims=True)
    p = jnp.exp(s - m)
    denom = jnp.sum(p, axis=-1, keepdims=True)
    p_ref[...] = p.astype(BF16)
    outs = []
    for h in range(H_B):
        sl = slice(h * HEAD_DIM, (h + 1) * HEAD_DIM)
        outs.append(jnp.dot(p_ref[h], v_ref[0, keys, sl], preferred_element_type=F32) / denom[h])
    ob = jnp.concatenate(outs, axis=1)
    ms = jnp.mean(ob * ob, axis=-1, keepdims=True)
    o_ref[0] = (ob * lax.rsqrt(ms + RMS_EPS) * g_ref[...]).astype(BF16)


def _natten_variant_rowblocks(nrb):
    return (0, 1, 2, nrb - 2, nrb - 1)


def _natten_tables(rpb, rows):
    nrb = rows // NA_QROWS
    qc = np.arange(GRID_W)
    kc = np.arange(GRID_W)
    cs = np.clip(qc - NA_COLS // 2, 0, GRID_W - NA_COLS)
    col_ok = (kc[None, :] >= cs[:, None]) & (kc[None, :] < cs[:, None] + NA_COLS)
    cidx = np.clip(kc[None, :] - qc[:, None] + NA_COLS - 1, 0, 2 * NA_COLS - 2)
    tabs = []
    for rb in _natten_variant_rowblocks(nrb):
        r0 = rb * NA_QROWS
        qrows = r0 + np.arange(NA_QROWS)
        rs = np.clip(qrows - NA_ROWS // 2, 0, rows - NA_ROWS)
        kstart = int(np.clip(r0 - NA_ROWS // 2, 0, rows - NA_KROWS))
        krows = kstart + np.arange(NA_KROWS)
        row_ok = (krows[None, :] >= rs[:, None]) & (krows[None, :] < rs[:, None] + NA_ROWS)
        ridx = np.clip(krows[None, :] - qrows[:, None] + NA_ROWS - 1, 0, 2 * NA_ROWS - 2)
        valid = row_ok[:, None, :, None] & col_ok[None, :, None, :]
        bias = rpb.astype(F32)[:, ridx[:, None, :, None], cidx[None, :, None, :]]
        bias = jnp.where(valid[None], bias, NEG_INF)
        tabs.append(bias.reshape(H_B, NA_Q, NA_KEYS))
    return jnp.stack(tabs)


def _natten(qb, kb, vb, tabs, g):
    b, seq, _ = qb.shape
    rows = seq // GRID_W
    nrb = rows // NA_QROWS
    assert rows >= NA_KROWS and nrb >= 5

    def variant(rb):
        return jnp.where(rb < 2, rb, jnp.where(rb >= nrb - 2, rb - (nrb - 5), 2))

    whole = pl.BlockSpec((1, seq, DB), lambda bi, rb: (bi, 0, 0))
    blk = pl.BlockSpec((1, NA_Q, DB), lambda bi, rb: (bi, rb, 0))
    return pl.pallas_call(
        functools.partial(_natten_kernel, rows=rows),
        grid=(b, nrb),
        in_specs=[blk, whole, whole,
                  pl.BlockSpec((1, H_B, NA_Q, NA_KEYS), lambda bi, rb: (variant(rb), 0, 0, 0)),
                  pl.BlockSpec(g.shape, lambda bi, rb: (0, 0))],
        out_specs=blk,
        out_shape=jax.ShapeDtypeStruct((b, seq, DB), BF16),
        scratch_shapes=[pltpu.VMEM((H_B, NA_Q, NA_KEYS), F32), pltpu.VMEM((H_B, NA_Q, NA_KEYS), BF16)],
        compiler_params=_cparams(("parallel", "arbitrary")),
        name="natten",
    )(qb, kb, vb, tabs, g)


def _outproj_kernel(x_ref, oa_ref, ob_ref, wo_ref, g_ref, wr_ref, x1_ref, h_ref, aff_ref):
    x1 = (x_ref[...]
          + jnp.dot(oa_ref[...], wo_ref[0:DQA, :], preferred_element_type=F32)
          + jnp.dot(ob_ref[...], wo_ref[DQA:DQA + DB, :], preferred_element_type=F32))
    x1_ref[...] = x1
    ms = jnp.mean(x1 * x1, axis=-1, keepdims=True)
    hf = x1 * lax.rsqrt(ms + RMS_EPS) * g_ref[...]
    h_hi = hf.astype(BF16)
    h_lo = (hf - h_hi.astype(F32)).astype(BF16)
    h_ref[...] = _pack_rows(hf)
    l_hi = lax.dot_general(wr_ref[...], h_hi, _NT, preferred_element_type=F32)
    l_lo = lax.dot_general(wr_ref[0:N_EXPERTS, :], h_lo, _NT, preferred_element_type=F32)
    logits = l_hi[0:N_EXPERTS] + l_hi[N_EXPERTS:2 * N_EXPERTS] + l_lo
    m = jnp.max(logits, axis=0, keepdims=True)
    e = jnp.exp(logits - m)
    aff_ref[...] = e / jnp.sum(e, axis=0, keepdims=True)


def _outproj(x2, oa, ob, wo_bf, g, wr_t):
    n = x2.shape[0]
    tok = lambda w: pl.BlockSpec((TM, w), lambda i: (i, 0))
    full = lambda a: pl.BlockSpec(a.shape, lambda i: (0,) * a.ndim)
    return pl.pallas_call(
        _outproj_kernel,
        grid=(n // TM,),
        in_specs=[tok(D_MODEL), tok(DQA), tok(DB), full(wo_bf), full(g), full(wr_t)],
        out_specs=[tok(D_MODEL), tok(HALF), pl.BlockSpec((N_EXPERTS, TM), lambda i: (0, i))],
        out_shape=[jax.ShapeDtypeStruct((n, D_MODEL), F32), jax.ShapeDtypeStruct((n, HALF), U32),
                   jax.ShapeDtypeStruct((N_EXPERTS, n), F32)],
        compiler_params=_cparams(("parallel",)),
        name="outproj_router",
    )(x2, oa, ob, wo_bf, g, wr_t)


def _moe_kernel(x_ref, gate_ref, wg_ref, wu_ref, wd_ref, o_ref):
    x = _unpack_rows(x_ref[0])
    y = jnp.zeros((x.shape[0], D_MODEL), F32)
    for c in range(D_EXPERT // FC):
        sl = slice(c * FC, (c + 1) * FC)
        a = jnp.dot(x, wg_ref[0, :, sl], preferred_element_type=F32)
        u = jnp.dot(x, wu_ref[0, :, sl], preferred_element_type=F32)
        hid = (a * jax.nn.sigmoid(a) * u).astype(BF16)
        y = y + jnp.dot(hid, wd_ref[0, sl, :], preferred_element_type=F32)
    o_ref[0] = _pack_rows(y * gate_ref[0])


def _moe(xe, gate3, wg, wu, wd):
    e, cap, _ = xe.shape
    tc = min(TC, cap)
    tokb = lambda w: pl.BlockSpec((1, tc, w), lambda ei, i: (ei, i, 0))
    wspec = lambda a: pl.BlockSpec((1,) + a.shape[1:], lambda ei, i: (ei, 0, 0))
    return pl.pallas_call(
        _moe_kernel,
        grid=(e, cap // tc),
        in_specs=[tokb(HALF), tokb(1), wspec(wg), wspec(wu), wspec(wd)],
        out_specs=tokb(HALF),
        out_shape=jax.ShapeDtypeStruct((e, cap, HALF), U32),
        compiler_params=_cparams(("parallel", "arbitrary")),
        name="moe_ffn",
    )(xe, gate3, wg, wu, wd)


_FIRST, _LAST, _ACTIVE = 1, 2, 4


def _combine_kernel(tile_ref, blk_ref, flag_ref, tok_ref, yg_ref, x1_ref, g_ref, o_ref, acc_ref):
    s = pl.program_id(0)
    flag = flag_ref[s]

    @pl.when((flag & _FIRST) != 0)
    def _():
        acc_ref[...] = jnp.zeros_like(acc_ref)

    @pl.when((flag & _ACTIVE) != 0)
    def _():
        tok_row = tile_ref[s] * TS + lax.broadcasted_iota(jnp.int32, (TS, SB), 0)
        onehot = jnp.where(tok_ref[0] == tok_row, 1.0, 0.0).astype(BF16)
        acc_ref[...] += jnp.dot(onehot, _unpack_rows(yg_ref[...]), preferred_element_type=F32)

    @pl.when((flag & _LAST) != 0)
    def _():
        y = x1_ref[...] + acc_ref[...]
        ms = jnp.mean(y * y, axis=-1, keepdims=True)
        o_ref[...] = y * lax.rsqrt(ms + RMS_EPS) * g_ref[...]


def _combine_schedule(tok_sorted, n):
    nt, nblk = n // TS, tok_sorted.shape[0] // SB
    steps = nt + nblk
    bounds = jnp.searchsorted(tok_sorted, jnp.arange(nt + 1, dtype=jnp.int32) * TS, side="left").astype(jnp.int32)
    lo, hi = bounds[:-1], bounds[1:]
    b_lo = jnp.minimum(lo // SB, nblk - 1)
    b_hi = jnp.where(hi > lo, (hi - 1) // SB, b_lo)
    nst = b_hi - b_lo + 1
    cum = jnp.cumsum(nst)
    start = cum - nst
    s = jnp.arange(steps, dtype=jnp.int32)
    tile = jnp.minimum(jnp.searchsorted(cum, s, side="right").astype(jnp.int32), nt - 1)
    active = s < cum[-1]
    blk = jnp.where(active, b_lo[tile] + s - start[tile], b_hi[nt - 1])
    first = active & (s == start[tile])
    last = active & (s == cum[tile] - 1)
    flag = first * _FIRST + last * _LAST + active * _ACTIVE
    return tile, blk.astype(jnp.int32), flag.astype(jnp.int32)


def _combine(tile, blk, flag, tok3, yg, x1, g):
    n = x1.shape[0]
    steps = tile.shape[0]
    grid_spec = pltpu.PrefetchScalarGridSpec(
        num_scalar_prefetch=3,
        grid=(steps,),
        in_specs=[pl.BlockSpec((1, 1, SB), lambda s, t, b, f: (b[s], 0, 0)),
                  pl.BlockSpec((SB, HALF), lambda s, t, b, f: (b[s], 0)),
                  pl.BlockSpec((TS, D_MODEL), lambda s, t, b, f: (t[s], 0)),
                  pl.BlockSpec(g.shape, lambda s, t, b, f: (0, 0))],
        out_specs=pl.BlockSpec((TS, D_MODEL), lambda s, t, b, f: (t[s], 0)),
        scratch_shapes=[pltpu.VMEM((TS, D_MODEL), F32)],
    )
    return pl.pallas_call(
        _combine_kernel,
        grid_spec=grid_spec,
        out_shape=jax.ShapeDtypeStruct((n, D_MODEL), F32),
        compiler_params=_cparams(("arbitrary",)),
        name="combine_norm",
    )(tile, blk, flag, tok3, yg, x1, g)


def _group_forward(x, p):
    b, seq, _ = x.shape
    n = b * seq
    cap = CAP_FACTOR * n // N_EXPERTS
    x2 = x.reshape(n, D_MODEL)
    cos_t, sa_t, sb_t = p["rot"]
    qa, ka, va, qb, kb, vb = _inproj(x2, p["g_attn"], p["w_in"], cos_t[:seq], sa_t[:seq], sb_t[:seq], seq)
    r3 = lambda a: a.reshape(b, seq, a.shape[-1])
    oa = _window(r3(qa), r3(ka), r3(va), p["sink"], p["g_out_a"])
    ob = _natten(r3(qb), r3(kb), r3(vb), _natten_tables(p["rpb"], seq // GRID_W), p["g_out_b"])
    x1, h, aff_t = _outproj(x2, oa.reshape(n, DQA), ob.reshape(n, DB), p["w_out"], p["g_ffn"], p["w_router_t"])

    _, idx = lax.top_k(aff_t, cap)
    idx = jnp.sort(idx, axis=1)
    gate = jnp.take_along_axis(aff_t, idx, axis=1)
    flat = idx.reshape(-1)
    xe = _sc_gather_rows(h, flat).reshape(N_EXPERTS, cap, HALF)
    ye = _moe(xe, gate.reshape(N_EXPERTS, cap, 1), p["w_gate"], p["w_up"], p["w_down"])

    perm = jnp.argsort(flat, stable=True).astype(jnp.int32)
    tok_sorted = flat[perm]
    yg = _sc_gather_rows(ye.reshape(N_EXPERTS * cap, HALF), perm)
    tile, blk, flag = _combine_schedule(tok_sorted, n)
    y = _combine(tile, blk, flag, tok_sorted.reshape(-1, 1, SB), yg, x1, p["g_final"])
    return y.reshape(b, seq, D_MODEL)


def kernel(x_prompt, x_sample, g_attn, w_in, g_out_a, g_out_b, sink_a, rpb_b, w_out, g_ffn, w_router,
           w_gate, w_up, w_down, g_final):
    assert g_attn.shape[0] == 1, "single trunk layer"
    wr = w_router[0].T
    wr_hi = wr.astype(BF16)
    wr_lo = (wr - wr_hi.astype(F32)).astype(BF16)
    p = {
        "g_attn": g_attn[0][None, :], "w_in": w_in[0].astype(BF16),
        "g_out_a": g_out_a[0][None, :], "g_out_b": g_out_b[0][None, :],
        "sink": sink_a[0], "rpb": rpb_b[0], "w_out": w_out[0].astype(BF16),
        "g_ffn": g_ffn[0][None, :], "w_router_t": jnp.concatenate([wr_hi, wr_lo], axis=0),
        "w_gate": w_gate[0].astype(BF16), "w_up": w_up[0].astype(BF16), "w_down": w_down[0].astype(BF16),
        "g_final": g_final[None, :],
        "rot": _rotary_tables(max(x_prompt.shape[1], x_sample.shape[1])),
    }
    return (_group_forward(x_prompt, p), _group_forward(x_sample, p))
```

```python
import functools

import jax
import jax.numpy as jnp
import numpy as np
from jax import lax
from jax.experimental import pallas as pl
from jax.experimental.pallas import tpu as pltpu
from jax.experimental.pallas import tpu_sc as plsc

D_MODEL = 1024
HEAD_DIM = 64
HQ_A = 8
HKV_A = 2
H_B = 8
GRP = HQ_A // HKV_A
DQA = HQ_A * HEAD_DIM
DKA = HKV_A * HEAD_DIM
DB = H_B * HEAD_DIM
D_IN = DQA + 2 * DKA + 3 * DB
WINDOW = 128
BLK = 128
ROPE_THETA = 500000.0
ROT_DIM = HEAD_DIM // 4
GRID_W = 64
NA_ROWS = 8
NA_COLS = 16
NA_QROWS = 2
NA_KROWS = NA_ROWS + NA_QROWS - 1
NA_KEYS = NA_KROWS * GRID_W
NA_Q = NA_QROWS * GRID_W
N_EXPERTS = 16
CAP_FACTOR = 2
D_EXPERT = 2 * D_MODEL
RMS_EPS = 1e-6
NEG_INF = -1e30
SCALE = HEAD_DIM ** -0.5

LANES = 128
TM = 512
TQ = 512
TC = 512
FC = 512
TS = 512
SB = 512
VMEM_LIMIT = 56 * 1024 * 1024

BF16 = jnp.bfloat16
F32 = jnp.float32
_NT = (((1,), (1,)), ((), ()))


def _cparams(sem):
    return pltpu.CompilerParams(dimension_semantics=sem, vmem_limit_bytes=VMEM_LIMIT)


HALF = D_MODEL // 2
U32 = jnp.uint32


def _pack_rows(v):
    r = v.astype(BF16).astype(F32)
    hi = pltpu.bitcast(r[:, :HALF], U32)
    lo = pltpu.bitcast(r[:, HALF:], U32)
    return (hi & jnp.uint32(0xFFFF0000)) | (lo >> 16)


def _unpack_rows(w):
    hi = pltpu.bitcast(w & jnp.uint32(0xFFFF0000), F32)
    lo = pltpu.bitcast(w << 16, F32)
    return jnp.concatenate([hi, lo], axis=1).astype(BF16)


SC_CORES = 2
SC_SUBCORES = 16
SC_WORKERS = SC_CORES * SC_SUBCORES
GATHER_ROWS = 64


def _sc_gather_rows(table, idx):
    m, w = idx.shape[0], table.shape[1]
    nchunks = m // GATHER_ROWS
    per_worker = nchunks // SC_WORKERS
    assert per_worker * SC_WORKERS * GATHER_ROWS == m and per_worker % 8 == 0
    mesh = plsc.VectorSubcoreMesh(core_axis_name="c", subcore_axis_name="s")

    @functools.partial(
        pl.kernel, mesh=mesh, out_type=jax.ShapeDtypeStruct((m, w), table.dtype),
        scratch_types=[pltpu.VMEM((per_worker, GATHER_ROWS), jnp.int32),
                       pltpu.VMEM((GATHER_ROWS, w), table.dtype),
                       pltpu.SemaphoreType.DMA])
    def gather(tab_hbm, idx_hbm, out_hbm, idx_v, rows_v, sem):
        wid = lax.axis_index("s") * SC_CORES + lax.axis_index("c")
        first = wid * per_worker
        pltpu.sync_copy(idx_hbm.at[pl.ds(first, per_worker)], idx_v)

        @pl.loop(0, per_worker)
        def _(c):
            pltpu.async_copy(tab_hbm.at[idx_v.at[c]], rows_v, sem).wait()
            pltpu.sync_copy(rows_v, out_hbm.at[pl.ds((first + c) * GATHER_ROWS, GATHER_ROWS)])

    return gather(table, idx.reshape(nchunks, GATHER_ROWS))


def _inproj_kernel(x_ref, g_ref, w_ref, cos_ref, sa_ref, sb_ref,
                   qa_ref, ka_ref, va_ref, qb_ref, kb_ref, vb_ref):
    x = x_ref[...]
    ms = jnp.mean(x * x, axis=-1, keepdims=True)
    hn = (x * lax.rsqrt(ms + RMS_EPS) * g_ref[...]).astype(BF16)
    c, sa, sb = cos_ref[...], sa_ref[...], sb_ref[...]

    def proj(lo, n):
        return jnp.dot(hn, w_ref[:, lo:lo + n], preferred_element_type=F32)

    def rot(blk):
        return blk * c + pltpu.roll(blk, LANES - ROT_DIM // 2, 1) * sa + pltpu.roll(blk, ROT_DIM // 2, 1) * sb

    for j in range(DQA // LANES):
        qa_ref[:, j * LANES:(j + 1) * LANES] = (rot(proj(j * LANES, LANES)) * SCALE).astype(BF16)
    ka_ref[...] = rot(proj(DQA, DKA)).astype(BF16)
    va_ref[...] = proj(DQA + DKA, DKA).astype(BF16)
    qb_ref[...] = (proj(DQA + 2 * DKA, DB) * SCALE).astype(BF16)
    kb_ref[...] = proj(DQA + 2 * DKA + DB, DB).astype(BF16)
    vb_ref[...] = proj(DQA + 2 * DKA + 2 * DB, DB).astype(BF16)


def _inproj(x2, g, w_bf, cos_t, sa_t, sb_t, seq):
    n = x2.shape[0]
    per_seq = seq // TM
    tok = lambda w: pl.BlockSpec((TM, w), lambda i: (i, 0))
    full = lambda a: pl.BlockSpec(a.shape, lambda i: (0,) * a.ndim)
    pos = pl.BlockSpec((TM, LANES), lambda i: (i % per_seq, 0))
    widths = (DQA, DKA, DKA, DB, DB, DB)
    return pl.pallas_call(
        _inproj_kernel,
        grid=(n // TM,),
        in_specs=[tok(D_MODEL), full(g), full(w_bf), pos, pos, pos],
        out_specs=[tok(w) for w in widths],
        out_shape=[jax.ShapeDtypeStruct((n, w), BF16) for w in widths],
        compiler_params=_cparams(("parallel",)),
        name="inproj",
    )(x2, g, w_bf, cos_t, sa_t, sb_t)


def _rotary_tables(seq):
    half = ROT_DIM // 2
    inv_freq = jnp.float32(ROPE_THETA) ** (-(jnp.arange(half, dtype=F32) * 2.0) / ROT_DIM)
    ang = jnp.arange(seq, dtype=F32)[:, None] * inv_freq[None, :]
    cos, sin = jnp.cos(ang), jnp.sin(ang)
    ones = jnp.ones((seq, HEAD_DIM - ROT_DIM), F32)
    zeros = jnp.zeros((seq, HEAD_DIM - ROT_DIM), F32)
    zh = jnp.zeros((seq, half), F32)
    rep = LANES // HEAD_DIM
    cos_t = jnp.tile(jnp.concatenate([cos, cos, ones], axis=1), (1, rep))
    sa_t = jnp.tile(jnp.concatenate([-sin, zh, zeros], axis=1), (1, rep))
    sb_t = jnp.tile(jnp.concatenate([zh, sin, zeros], axis=1), (1, rep))
    return cos_t, sa_t, sb_t


def _window_kernel(q_ref, kp_ref, kc_ref, kn_ref, vp_ref, vc_ref, vn_ref, sink_ref, g_ref, o_ref,
                   k_ref, v_ref, s_ref, p_ref, *, nb):
    i = pl.program_id(1)
    nsub = TQ // BLK
    k_ref[0:BLK] = kp_ref[0]
    k_ref[BLK:BLK + TQ] = kc_ref[0]
    k_ref[BLK + TQ:TQ + 2 * BLK] = kn_ref[0]
    v_ref[0:BLK] = vp_ref[0]
    v_ref[BLK:BLK + TQ] = vc_ref[0]
    v_ref[BLK + TQ:TQ + 2 * BLK] = vn_ref[0]
    qi = lax.broadcasted_iota(jnp.int32, (BLK, 3 * BLK), 0)
    kj = lax.broadcasted_iota(jnp.int32, (BLK, 3 * BLK), 1)
    band = jnp.abs(kj - BLK - qi) <= WINDOW
    sink = sink_ref[...]
    for j in range(nsub):
        n = i * nsub + j
        k_lo = jnp.where(n >= 1, 0, BLK)
        k_hi = jnp.where(n <= nb - 2, 3 * BLK, 2 * BLK)
        bias = jnp.where(band & (kj >= k_lo) & (kj < k_hi), 0.0, NEG_INF)
        rows = slice(j * BLK, (j + 1) * BLK)
        keys = slice(j * BLK, (j + 3) * BLK)
        for h in range(HKV_A):
            kh = k_ref[keys, h * HEAD_DIM:(h + 1) * HEAD_DIM]
            qs = jnp.concatenate(
                [q_ref[0, rows, hd * HEAD_DIM:(hd + 1) * HEAD_DIM] for hd in range(h * GRP, (h + 1) * GRP)], axis=0)
            s = lax.dot_general(qs, kh, _NT, preferred_element_type=F32)
            s_ref[h * GRP:(h + 1) * GRP] = s.reshape(GRP, BLK, 3 * BLK)
        s = s_ref[...] + bias[None]
        m = jnp.maximum(jnp.max(s, axis=-1, keepdims=True), sink)
        p = jnp.exp(s - m)
        denom = jnp.sum(p, axis=-1, keepdims=True) + jnp.exp(sink - m)
        p_ref[...] = p.astype(BF16)
        outs = []
        for h in range(HKV_A):
            vh = v_ref[keys, h * HEAD_DIM:(h + 1) * HEAD_DIM]
            ph = p_ref[h * GRP:(h + 1) * GRP].reshape(GRP * BLK, 3 * BLK)
            o = jnp.dot(ph, vh, preferred_element_type=F32)
            outs += [o[g * BLK:(g + 1) * BLK] / denom[h * GRP + g] for g in range(GRP)]
        ob = jnp.concatenate(outs, axis=1)
        ms = jnp.mean(ob * ob, axis=-1, keepdims=True)
        o_ref[0, rows, :] = (ob * lax.rsqrt(ms + RMS_EPS) * g_ref[...]).astype(BF16)


def _window(qa, ka, va, sink, g):
    b, seq, _ = qa.shape
    nb = seq // BLK
    r = TQ // BLK
    cur = lambda w: pl.BlockSpec((1, TQ, w), lambda bi, i: (bi, i, 0))
    prev = pl.BlockSpec((1, BLK, DKA), lambda bi, i: (bi, jnp.maximum(i * r - 1, 0), 0))
    nxt = pl.BlockSpec((1, BLK, DKA), lambda bi, i: (bi, jnp.minimum((i + 1) * r, nb - 1), 0))
    sink3 = sink.reshape(HQ_A, 1, 1)
    return pl.pallas_call(
        functools.partial(_window_kernel, nb=nb),
        grid=(b, seq // TQ),
        in_specs=[cur(DQA), prev, cur(DKA), nxt, prev, cur(DKA), nxt,
                  pl.BlockSpec(sink3.shape, lambda bi, i: (0, 0, 0)), pl.BlockSpec(g.shape, lambda bi, i: (0, 0))],
        out_specs=cur(DQA),
        out_shape=jax.ShapeDtypeStruct((b, seq, DQA), BF16),
        scratch_shapes=[pltpu.VMEM((TQ + 2 * BLK, DKA), BF16), pltpu.VMEM((TQ + 2 * BLK, DKA), BF16),
                        pltpu.VMEM((HQ_A, BLK, 3 * BLK), F32), pltpu.VMEM((HQ_A, BLK, 3 * BLK), BF16)],
        compiler_params=_cparams(("parallel", "parallel")),
        name="window_attn",
    )(qa, ka, ka, ka, va, va, va, sink3, g)


def _natten_kernel(q_ref, k_ref, v_ref, tab_ref, g_ref, o_ref, s_ref, p_ref, *, rows):
    rb = pl.program_id(1)
    kstart = jnp.clip(NA_QROWS * rb - NA_ROWS // 2, 0, rows - NA_KROWS)
    keys = pl.ds(pl.multiple_of(kstart * GRID_W, GRID_W), NA_KEYS)
    for h in range(H_B):
        sl = slice(h * HEAD_DIM, (h + 1) * HEAD_DIM)
        s_ref[h] = lax.dot_general(q_ref[0, :, sl], k_ref[0, keys, sl], _NT, preferred_element_type=F32)
    s = s_ref[...] + tab_ref[0]
    m = jnp.max(s, axis=-1, keepdims=True)
    p = jnp.exp(s - m)
    denom = jnp.sum(p, axis=-1, keepdims=True)
    p_ref[...] = p.astype(BF16)
    outs = []
    for h in range(H_B):
        sl = slice(h * HEAD_DIM, (h + 1) * HEAD_DIM)
        outs.append(jnp.dot(p_ref[h], v_ref[0, keys, sl], preferred_element_type=F32) / denom[h])
    ob = jnp.concatenate(outs, axis=1)
    ms = jnp.mean(ob * ob, axis=-1, keepdims=True)
    o_ref[0] = (ob * lax.rsqrt(ms + RMS_EPS) * g_ref[...]).astype(BF16)


def _natten_variant_rowblocks(nrb):
    return (0, 1, 2, nrb - 2, nrb - 1)


def _natten_tables(rpb, rows):
    nrb = rows // NA_QROWS
    qc = np.arange(GRID_W)
    kc = np.arange(GRID_W)
    cs = np.clip(qc - NA_COLS // 2, 0, GRID_W - NA_COLS)
    col_ok = (kc[None, :] >= cs[:, None]) & (kc[None, :] < cs[:, None] + NA_COLS)
    cidx = np.clip(kc[None, :] - qc[:, None] + NA_COLS - 1, 0, 2 * NA_COLS - 2)
    tabs = []
    for rb in _natten_variant_rowblocks(nrb):
        r0 = rb * NA_QROWS
        qrows = r0 + np.arange(NA_QROWS)
        rs = np.clip(qrows - NA_ROWS // 2, 0, rows - NA_ROWS)
        kstart = int(np.clip(r0 - NA_ROWS // 2, 0, rows - NA_KROWS))
        krows = kstart + np.arange(NA_KROWS)
        row_ok = (krows[None, :] >= rs[:, None]) & (krows[None, :] < rs[:, None] + NA_ROWS)
        ridx = np.clip(krows[None, :] - qrows[:, None] + NA_ROWS - 1, 0, 2 * NA_ROWS - 2)
        valid = row_ok[:, None, :, None] & col_ok[None, :, None, :]
        r1h = (ridx[:, :, None] == np.arange(2 * NA_ROWS - 1)).astype(np.float32)
        c1h = (cidx[:, :, None] == np.arange(2 * NA_COLS - 1)).astype(np.float32)
        bias = jnp.einsum("rka,hab,qcb->hrqkc", r1h, rpb.astype(F32), c1h, precision=lax.Precision.HIGHEST)
        bias = jnp.where(valid[None], bias, NEG_INF)
        tabs.append(bias.reshape(H_B, NA_Q, NA_KEYS))
    return jnp.stack(tabs)


def _natten(qb, kb, vb, tabs, g):
    b, seq, _ = qb.shape
    rows = seq // GRID_W
    nrb = rows // NA_QROWS
    assert rows >= NA_KROWS and nrb >= 5

    def variant(rb):
        return jnp.where(rb < 2, rb, jnp.where(rb >= nrb - 2, rb - (nrb - 5), 2))

    whole = pl.BlockSpec((1, seq, DB), lambda bi, rb: (bi, 0, 0))
    blk = pl.BlockSpec((1, NA_Q, DB), lambda bi, rb: (bi, rb, 0))
    return pl.pallas_call(
        functools.partial(_natten_kernel, rows=rows),
        grid=(b, nrb),
        in_specs=[blk, whole, whole,
                  pl.BlockSpec((1, H_B, NA_Q, NA_KEYS), lambda bi, rb: (variant(rb), 0, 0, 0)),
                  pl.BlockSpec(g.shape, lambda bi, rb: (0, 0))],
        out_specs=blk,
        out_shape=jax.ShapeDtypeStruct((b, seq, DB), BF16),
        scratch_shapes=[pltpu.VMEM((H_B, NA_Q, NA_KEYS), F32), pltpu.VMEM((H_B, NA_Q, NA_KEYS), BF16)],
        compiler_params=_cparams(("parallel", "arbitrary")),
        name="natten",
    )(qb, kb, vb, tabs, g)


def _outproj_kernel(x_ref, oa_ref, ob_ref, wo_ref, g_ref, wr_ref, x1_ref, h_ref, aff_ref):
    x1 = (x_ref[...]
          + jnp.dot(oa_ref[...], wo_ref[0:DQA, :], preferred_element_type=F32)
          + jnp.dot(ob_ref[...], wo_ref[DQA:DQA + DB, :], preferred_element_type=F32))
    x1_ref[...] = x1
    ms = jnp.mean(x1 * x1, axis=-1, keepdims=True)
    hf = x1 * lax.rsqrt(ms + RMS_EPS) * g_ref[...]
    h_hi = hf.astype(BF16)
    h_lo = (hf - h_hi.astype(F32)).astype(BF16)
    h_ref[...] = _pack_rows(hf)
    l_hi = lax.dot_general(wr_ref[...], h_hi, _NT, preferred_element_type=F32)
    l_lo = lax.dot_general(wr_ref[0:N_EXPERTS, :], h_lo, _NT, preferred_element_type=F32)
    logits = l_hi[0:N_EXPERTS] + l_hi[N_EXPERTS:2 * N_EXPERTS] + l_lo
    m = jnp.max(logits, axis=0, keepdims=True)
    e = jnp.exp(logits - m)
    aff_ref[...] = e / jnp.sum(e, axis=0, keepdims=True)


def _outproj(x2, oa, ob, wo_bf, g, wr_t):
    n = x2.shape[0]
    tok = lambda w: pl.BlockSpec((TM, w), lambda i: (i, 0))
    full = lambda a: pl.BlockSpec(a.shape, lambda i: (0,) * a.ndim)
    return pl.pallas_call(
        _outproj_kernel,
        grid=(n // TM,),
        in_specs=[tok(D_MODEL), tok(DQA), tok(DB), full(wo_bf), full(g), full(wr_t)],
        out_specs=[tok(D_MODEL), tok(HALF), pl.BlockSpec((N_EXPERTS, TM), lambda i: (0, i))],
        out_shape=[jax.ShapeDtypeStruct((n, D_MODEL), F32), jax.ShapeDtypeStruct((n, HALF), U32),
                   jax.ShapeDtypeStruct((N_EXPERTS, n), F32)],
        compiler_params=_cparams(("parallel",)),
        name="outproj_router",
    )(x2, oa, ob, wo_bf, g, wr_t)


def _moe_kernel(x_ref, gate_ref, wg_ref, wu_ref, wd_ref, o_ref):
    x = _unpack_rows(x_ref[0])
    y = jnp.zeros((x.shape[0], D_MODEL), F32)
    for c in range(D_EXPERT // FC):
        sl = slice(c * FC, (c + 1) * FC)
        a = jnp.dot(x, wg_ref[0, :, sl], preferred_element_type=F32)
        u = jnp.dot(x, wu_ref[0, :, sl], preferred_element_type=F32)
        hid = (a * jax.nn.sigmoid(a) * u).astype(BF16)
        y = y + jnp.dot(hid, wd_ref[0, sl, :], preferred_element_type=F32)
    o_ref[0] = _pack_rows(y * gate_ref[0])


def _moe(xe, gate3, wg, wu, wd):
    e, cap, _ = xe.shape
    tc = min(TC, cap)
    tokb = lambda w: pl.BlockSpec((1, tc, w), lambda ei, i: (ei, i, 0))
    wspec = lambda a: pl.BlockSpec((1,) + a.shape[1:], lambda ei, i: (ei, 0, 0))
    return pl.pallas_call(
        _moe_kernel,
        grid=(e, cap // tc),
        in_specs=[tokb(HALF), tokb(1), wspec(wg), wspec(wu), wspec(wd)],
        out_specs=tokb(HALF),
        out_shape=jax.ShapeDtypeStruct((e, cap, HALF), U32),
        compiler_params=_cparams(("parallel", "arbitrary")),
        name="moe_ffn",
    )(xe, gate3, wg, wu, wd)


_FIRST, _LAST, _ACTIVE = 1, 2, 4


def _combine_kernel(tile_ref, blk_ref, flag_ref, tok_ref, yg_ref, x1_ref, g_ref, o_ref, acc_ref):
    s = pl.program_id(0)
    flag = flag_ref[s]

    @pl.when((flag & _FIRST) != 0)
    def _():
        acc_ref[...] = jnp.zeros_like(acc_ref)

    @pl.when((flag & _ACTIVE) != 0)
    def _():
        tok_row = tile_ref[s] * TS + lax.broadcasted_iota(jnp.int32, (TS, SB), 0)
        onehot = jnp.where(tok_ref[0] == tok_row, 1.0, 0.0).astype(BF16)
        acc_ref[...] += jnp.dot(onehot, _unpack_rows(yg_ref[...]), preferred_element_type=F32)

    @pl.when((flag & _LAST) != 0)
    def _():
        y = x1_ref[...] + acc_ref[...]
        ms = jnp.mean(y * y, axis=-1, keepdims=True)
        o_ref[...] = y * lax.rsqrt(ms + RMS_EPS) * g_ref[...]


def _combine_schedule(tok_sorted, n):
    nt, nblk = n // TS, tok_sorted.shape[0] // SB
    steps = nt + nblk
    bounds = jnp.searchsorted(tok_sorted, jnp.arange(nt + 1, dtype=jnp.int32) * TS, side="left").astype(jnp.int32)
    lo, hi = bounds[:-1], bounds[1:]
    b_lo = jnp.minimum(lo // SB, nblk - 1)
    b_hi = jnp.where(hi > lo, (hi - 1) // SB, b_lo)
    nst = b_hi - b_lo + 1
    cum = jnp.cumsum(nst)
    start = cum - nst
    s = jnp.arange(steps, dtype=jnp.int32)
    tile = jnp.minimum(jnp.searchsorted(cum, s, side="right").astype(jnp.int32), nt - 1)
    active = s < cum[-1]
    blk = jnp.where(active, b_lo[tile] + s - start[tile], b_hi[nt - 1])
    first = active & (s == start[tile])
    last = active & (s == cum[tile] - 1)
    flag = first * _FIRST + last * _LAST + active * _ACTIVE
    return tile, blk.astype(jnp.int32), flag.astype(jnp.int32)


def _combine(tile, blk, flag, tok3, yg, x1, g):
    n = x1.shape[0]
    steps = tile.shape[0]
    grid_spec = pltpu.PrefetchScalarGridSpec(
        num_scalar_prefetch=3,
        grid=(steps,),
        in_specs=[pl.BlockSpec((1, 1, SB), lambda s, t, b, f: (b[s], 0, 0)),
                  pl.BlockSpec((SB, HALF), lambda s, t, b, f: (b[s], 0)),
                  pl.BlockSpec((TS, D_MODEL), lambda s, t, b, f: (t[s], 0)),
                  pl.BlockSpec(g.shape, lambda s, t, b, f: (0, 0))],
        out_specs=pl.BlockSpec((TS, D_MODEL), lambda s, t, b, f: (t[s], 0)),
        scratch_shapes=[pltpu.VMEM((TS, D_MODEL), F32)],
    )
    return pl.pallas_call(
        _combine_kernel,
        grid_spec=grid_spec,
        out_shape=jax.ShapeDtypeStruct((n, D_MODEL), F32),
        compiler_params=_cparams(("arbitrary",)),
        name="combine_norm",
    )(tile, blk, flag, tok3, yg, x1, g)


def _group_forward(x, p):
    b, seq, _ = x.shape
    n = b * seq
    cap = CAP_FACTOR * n // N_EXPERTS
    x2 = x.reshape(n, D_MODEL)
    cos_t, sa_t, sb_t = p["rot"]
    qa, ka, va, qb, kb, vb = _inproj(x2, p["g_attn"], p["w_in"], cos_t[:seq], sa_t[:seq], sb_t[:seq], seq)
    r3 = lambda a: a.reshape(b, seq, a.shape[-1])
    oa = _window(r3(qa), r3(ka), r3(va), p["sink"], p["g_out_a"])
    ob = _natten(r3(qb), r3(kb), r3(vb), p["na_tabs"], p["g_out_b"])
    x1, h, aff_t = _outproj(x2, oa.reshape(n, DQA), ob.reshape(n, DB), p["w_out"], p["g_ffn"], p["w_router_t"])

    _, idx = lax.top_k(aff_t, cap)
    idx = jnp.sort(idx, axis=1)
    gate = jnp.take_along_axis(aff_t, idx, axis=1)
    flat = idx.reshape(-1)
    xe = _sc_gather_rows(h, flat).reshape(N_EXPERTS, cap, HALF)
    ye = _moe(xe, gate.reshape(N_EXPERTS, cap, 1), p["w_gate"], p["w_up"], p["w_down"])

    perm = jnp.argsort(flat, stable=True).astype(jnp.int32)
    tok_sorted = flat[perm]
    yg = _sc_gather_rows(ye.reshape(N_EXPERTS * cap, HALF), perm)
    tile, blk, flag = _combine_schedule(tok_sorted, n)
    y = _combine(tile, blk, flag, tok_sorted.reshape(-1, 1, SB), yg, x1, p["g_final"])
    return y.reshape(b, seq, D_MODEL)


def kernel(x_prompt, x_sample, g_attn, w_in, g_out_a, g_out_b, sink_a, rpb_b, w_out, g_ffn, w_router,
           w_gate, w_up, w_down, g_final):
    assert g_attn.shape[0] == 1, "single trunk layer"
    wr = w_router[0].T
    wr_hi = wr.astype(BF16)
    wr_lo = (wr - wr_hi.astype(F32)).astype(BF16)
    p = {
        "g_attn": g_attn[0][None, :], "w_in": w_in[0].astype(BF16),
        "g_out_a": g_out_a[0][None, :], "g_out_b": g_out_b[0][None, :],
        "sink": sink_a[0], "na_tabs": _natten_tables(rpb_b[0], 4 * NA_ROWS), "w_out": w_out[0].astype(BF16),
        "g_ffn": g_ffn[0][None, :], "w_router_t": jnp.concatenate([wr_hi, wr_lo], axis=0),
        "w_gate": w_gate[0].astype(BF16), "w_up": w_up[0].astype(BF16), "w_down": w_down[0].astype(BF16),
        "g_final": g_final[None, :],
        "rot": _rotary_tables(max(x_prompt.shape[1], x_sample.shape[1])),
    }
    return (_group_forward(x_prompt, p), _group_forward(x_sample, p))
```

```python
import functools

import jax
import jax.numpy as jnp
import numpy as np
from jax import lax
from jax.experimental import pallas as pl
from jax.experimental.pallas import tpu as pltpu
from jax.experimental.pallas import tpu_sc as plsc

D_MODEL = 1024
HEAD_DIM = 64
HQ_A = 8
HKV_A = 2
H_B = 8
GRP = HQ_A // HKV_A
DQA = HQ_A * HEAD_DIM
DKA = HKV_A * HEAD_DIM
DB = H_B * HEAD_DIM
D_IN = DQA + 2 * DKA + 3 * DB
WINDOW = 128
BLK = 128
ROPE_THETA = 500000.0
ROT_DIM = HEAD_DIM // 4
GRID_W = 64
NA_ROWS = 8
NA_COLS = 16
NA_QROWS = 2
NA_KROWS = NA_ROWS + NA_QROWS - 1
NA_KEYS = NA_KROWS * GRID_W
NA_Q = NA_QROWS * GRID_W
N_EXPERTS = 16
CAP_FACTOR = 2
D_EXPERT = 2 * D_MODEL
RMS_EPS = 1e-6
NEG_INF = -1e30
SCALE = HEAD_DIM ** -0.5

LANES = 128
TM = 512
TQ = 512
TC = 512
FC = 512
TS = 512
SB = 512
VMEM_LIMIT = 56 * 1024 * 1024

BF16 = jnp.bfloat16
F32 = jnp.float32
_NT = (((1,), (1,)), ((), ()))


def _cparams(sem):
    return pltpu.CompilerParams(dimension_semantics=sem, vmem_limit_bytes=VMEM_LIMIT)


HALF = D_MODEL // 2
U32 = jnp.uint32


def _pack_rows(v):
    r = v.astype(BF16).astype(F32)
    hi = pltpu.bitcast(r[:, :HALF], U32)
    lo = pltpu.bitcast(r[:, HALF:], U32)
    return (hi & jnp.uint32(0xFFFF0000)) | (lo >> 16)


def _unpack_rows(w):
    hi = pltpu.bitcast(w & jnp.uint32(0xFFFF0000), F32)
    lo = pltpu.bitcast(w << 16, F32)
    return jnp.concatenate([hi, lo], axis=1).astype(BF16)


SC_CORES = 2
SC_SUBCORES = 16
SC_WORKERS = SC_CORES * SC_SUBCORES
GATHER_ROWS = 64


def _sc_gather_rows(table, idx):
    m, w = idx.shape[0], table.shape[1]
    nchunks = m // GATHER_ROWS
    per_worker = nchunks // SC_WORKERS
    assert per_worker * SC_WORKERS * GATHER_ROWS == m and per_worker % 8 == 0
    mesh = plsc.VectorSubcoreMesh(core_axis_name="c", subcore_axis_name="s")

    @functools.partial(
        pl.kernel, mesh=mesh, out_type=jax.ShapeDtypeStruct((m, w), table.dtype),
        scratch_types=[pltpu.VMEM((per_worker, GATHER_ROWS), jnp.int32),
                       pltpu.VMEM((GATHER_ROWS, w), table.dtype),
                       pltpu.SemaphoreType.DMA])
    def gather(tab_hbm, idx_hbm, out_hbm, idx_v, rows_v, sem):
        wid = lax.axis_index("s") * SC_CORES + lax.axis_index("c")
        first = wid * per_worker
        pltpu.sync_copy(idx_hbm.at[pl.ds(first, per_worker)], idx_v)

        @pl.loop(0, per_worker)
        def _(c):
            pltpu.async_copy(tab_hbm.at[idx_v.at[c]], rows_v, sem).wait()
            pltpu.sync_copy(rows_v, out_hbm.at[pl.ds((first + c) * GATHER_ROWS, GATHER_ROWS)])

    return gather(table, idx.reshape(nchunks, GATHER_ROWS))


def _inproj_kernel(x_ref, g_ref, w_ref, cos_ref, sa_ref, sb_ref,
                   qa_ref, ka_ref, va_ref, qb_ref, kb_ref, vb_ref):
    x = x_ref[...]
    ms = jnp.mean(x * x, axis=-1, keepdims=True)
    hn = (x * lax.rsqrt(ms + RMS_EPS) * g_ref[...]).astype(BF16)
    c, sa, sb = cos_ref[...], sa_ref[...], sb_ref[...]

    def proj(lo, n):
        return jnp.dot(hn, w_ref[:, lo:lo + n], preferred_element_type=F32)

    def rot(blk):
        return blk * c + pltpu.roll(blk, LANES - ROT_DIM // 2, 1) * sa + pltpu.roll(blk, ROT_DIM // 2, 1) * sb

    for j in range(DQA // LANES):
        qa_ref[:, j * LANES:(j + 1) * LANES] = (rot(proj(j * LANES, LANES)) * SCALE).astype(BF16)
    ka_ref[...] = rot(proj(DQA, DKA)).astype(BF16)
    va_ref[...] = proj(DQA + DKA, DKA).astype(BF16)
    qb_ref[...] = (proj(DQA + 2 * DKA, DB) * SCALE).astype(BF16)
    kb_ref[...] = proj(DQA + 2 * DKA + DB, DB).astype(BF16)
    vb_ref[...] = proj(DQA + 2 * DKA + 2 * DB, DB).astype(BF16)


def _inproj(x2, g, w_bf, cos_t, sa_t, sb_t, seq):
    n = x2.shape[0]
    per_seq = seq // TM
    tok = lambda w: pl.BlockSpec((TM, w), lambda i: (i, 0))
    full = lambda a: pl.BlockSpec(a.shape, lambda i: (0,) * a.ndim)
    pos = pl.BlockSpec((TM, LANES), lambda i: (i % per_seq, 0))
    widths = (DQA, DKA, DKA, DB, DB, DB)
    return pl.pallas_call(
        _inproj_kernel,
        grid=(n // TM,),
        in_specs=[tok(D_MODEL), full(g), full(w_bf), pos, pos, pos],
        out_specs=[tok(w) for w in widths],
        out_shape=[jax.ShapeDtypeStruct((n, w), BF16) for w in widths],
        compiler_params=_cparams(("parallel",)),
        name="inproj",
    )(x2, g, w_bf, cos_t, sa_t, sb_t)


def _rotary_tables(seq):
    half = ROT_DIM // 2
    inv_freq = jnp.float32(ROPE_THETA) ** (-(jnp.arange(half, dtype=F32) * 2.0) / ROT_DIM)
    ang = jnp.arange(seq, dtype=F32)[:, None] * inv_freq[None, :]
    cos, sin = jnp.cos(ang), jnp.sin(ang)
    ones = jnp.ones((seq, HEAD_DIM - ROT_DIM), F32)
    zeros = jnp.zeros((seq, HEAD_DIM - ROT_DIM), F32)
    zh = jnp.zeros((seq, half), F32)
    rep = LANES // HEAD_DIM
    cos_t = jnp.tile(jnp.concatenate([cos, cos, ones], axis=1), (1, rep))
    sa_t = jnp.tile(jnp.concatenate([-sin, zh, zeros], axis=1), (1, rep))
    sb_t = jnp.tile(jnp.concatenate([zh, sin, zeros], axis=1), (1, rep))
    return cos_t, sa_t, sb_t


def _window_kernel(q_ref, kp_ref, kc_ref, kn_ref, vp_ref, vc_ref, vn_ref, sink_ref, g_ref, o_ref,
                   k_ref, v_ref, s_ref, p_ref, *, nb):
    i = pl.program_id(1)
    nsub = TQ // BLK
    k_ref[0:BLK] = kp_ref[0]
    k_ref[BLK:BLK + TQ] = kc_ref[0]
    k_ref[BLK + TQ:TQ + 2 * BLK] = kn_ref[0]
    v_ref[0:BLK] = vp_ref[0]
    v_ref[BLK:BLK + TQ] = vc_ref[0]
    v_ref[BLK + TQ:TQ + 2 * BLK] = vn_ref[0]
    qi = lax.broadcasted_iota(jnp.int32, (BLK, 3 * BLK), 0)
    kj = lax.broadcasted_iota(jnp.int32, (BLK, 3 * BLK), 1)
    band = jnp.abs(kj - BLK - qi) <= WINDOW
    sink = sink_ref[...]
    for j in range(nsub):
        n = i * nsub + j
        k_lo = jnp.where(n >= 1, 0, BLK)
        k_hi = jnp.where(n <= nb - 2, 3 * BLK, 2 * BLK)
        bias = jnp.where(band & (kj >= k_lo) & (kj < k_hi), 0.0, NEG_INF)
        rows = slice(j * BLK, (j + 1) * BLK)
        keys = slice(j * BLK, (j + 3) * BLK)
        for h in range(HKV_A):
            kh = k_ref[keys, h * HEAD_DIM:(h + 1) * HEAD_DIM]
            qs = jnp.concatenate(
                [q_ref[0, rows, hd * HEAD_DIM:(hd + 1) * HEAD_DIM] for hd in range(h * GRP, (h + 1) * GRP)], axis=0)
            s = lax.dot_general(qs, kh, _NT, preferred_element_type=F32)
            s_ref[h * GRP:(h + 1) * GRP] = s.reshape(GRP, BLK, 3 * BLK)
        s = s_ref[...] + bias[None]
        m = jnp.maximum(jnp.max(s, axis=-1, keepdims=True), sink)
        p = jnp.exp(s - m)
        denom = jnp.sum(p, axis=-1, keepdims=True) + jnp.exp(sink - m)
        p_ref[...] = p.astype(BF16)
        outs = []
        for h in range(HKV_A):
            vh = v_ref[keys, h * HEAD_DIM:(h + 1) * HEAD_DIM]
            ph = p_ref[h * GRP:(h + 1) * GRP].reshape(GRP * BLK, 3 * BLK)
            o = jnp.dot(ph, vh, preferred_element_type=F32)
            outs += [o[g * BLK:(g + 1) * BLK] / denom[h * GRP + g] for g in range(GRP)]
        ob = jnp.concatenate(outs, axis=1)
        ms = jnp.mean(ob * ob, axis=-1, keepdims=True)
        o_ref[0, rows, :] = (ob * lax.rsqrt(ms + RMS_EPS) * g_ref[...]).astype(BF16)


def _window(qa, ka, va, sink, g):
    b, seq, _ = qa.shape
    nb = seq // BLK
    r = TQ // BLK
    cur = lambda w: pl.BlockSpec((1, TQ, w), lambda bi, i: (bi, i, 0))
    prev = pl.BlockSpec((1, BLK, DKA), lambda bi, i: (bi, jnp.maximum(i * r - 1, 0), 0))
    nxt = pl.BlockSpec((1, BLK, DKA), lambda bi, i: (bi, jnp.minimum((i + 1) * r, nb - 1), 0))
    sink3 = sink.reshape(HQ_A, 1, 1)
    return pl.pallas_call(
        functools.partial(_window_kernel, nb=nb),
        grid=(b, seq // TQ),
        in_specs=[cur(DQA), prev, cur(DKA), nxt, prev, cur(DKA), nxt,
                  pl.BlockSpec(sink3.shape, lambda bi, i: (0, 0, 0)), pl.BlockSpec(g.shape, lambda bi, i: (0, 0))],
        out_specs=cur(DQA),
        out_shape=jax.ShapeDtypeStruct((b, seq, DQA), BF16),
        scratch_shapes=[pltpu.VMEM((TQ + 2 * BLK, DKA), BF16), pltpu.VMEM((TQ + 2 * BLK, DKA), BF16),
                        pltpu.VMEM((HQ_A, BLK, 3 * BLK), F32), pltpu.VMEM((HQ_A, BLK, 3 * BLK), BF16)],
        compiler_params=_cparams(("parallel", "parallel")),
        name="window_attn",
    )(qa, ka, ka, ka, va, va, va, sink3, g)


def _natten_kernel(q_ref, k_ref, v_ref, tab_ref, g_ref, o_ref, s_ref, p_ref, *, rows):
    rb = pl.program_id(1)
    kstart = jnp.clip(NA_QROWS * rb - NA_ROWS // 2, 0, rows - NA_KROWS)
    keys = pl.ds(pl.multiple_of(kstart * GRID_W, GRID_W), NA_KEYS)
    for h in range(H_B):
        sl = slice(h * HEAD_DIM, (h + 1) * HEAD_DIM)
        s_ref[h] = lax.dot_general(q_ref[0, :, sl], k_ref[0, keys, sl], _NT, preferred_element_type=F32)
    s = s_ref[...] + tab_ref[0]
    m = jnp.max(s, axis=-1, keepdims=True)
    p = jnp.exp(s - m)
    denom = jnp.sum(p, axis=-1, keepdims=True)
    p_ref[...] = p.astype(BF16)
    outs = []
    for h in range(H_B):
        sl = slice(h * HEAD_DIM, (h + 1) * HEAD_DIM)
        outs.append(jnp.dot(p_ref[h], v_ref[0, keys, sl], preferred_element_type=F32) / denom[h])
    ob = jnp.concatenate(outs, axis=1)
    ms = jnp.mean(ob * ob, axis=-1, keepdims=True)
    o_ref[0] = (ob * lax.rsqrt(ms + RMS_EPS) * g_ref[...]).astype(BF16)


def _natten_variant_rowblocks(nrb):
    return (0, 1, 2, nrb - 2, nrb - 1)


def _natten_tables(rpb, rows):
    nrb = rows // NA_QROWS
    qc = np.arange(GRID_W)
    kc = np.arange(GRID_W)
    cs = np.clip(qc - NA_COLS // 2, 0, GRID_W - NA_COLS)
    col_ok = (kc[None, :] >= cs[:, None]) & (kc[None, :] < cs[:, None] + NA_COLS)
    cidx = np.clip(kc[None, :] - qc[:, None] + NA_COLS - 1, 0, 2 * NA_COLS - 2)
    tabs = []
    for rb in _natten_variant_rowblocks(nrb):
        r0 = rb * NA_QROWS
        qrows = r0 + np.arange(NA_QROWS)
        rs = np.clip(qrows - NA_ROWS // 2, 0, rows - NA_ROWS)
        kstart = int(np.clip(r0 - NA_ROWS // 2, 0, rows - NA_KROWS))
        krows = kstart + np.arange(NA_KROWS)
        row_ok = (krows[None, :] >= rs[:, None]) & (krows[None, :] < rs[:, None] + NA_ROWS)
        ridx = np.clip(krows[None, :] - qrows[:, None] + NA_ROWS - 1, 0, 2 * NA_ROWS - 2)
        valid = row_ok[:, None, :, None] & col_ok[None, :, None, :]
        r1h = (ridx[:, :, None] == np.arange(2 * NA_ROWS - 1)).astype(np.float32)
        c1h = (cidx[:, :, None] == np.arange(2 * NA_COLS - 1)).astype(np.float32)
        bias = jnp.einsum("rka,hab,qcb->hrqkc", r1h, rpb.astype(F32), c1h, precision=lax.Precision.HIGHEST)
        bias = jnp.where(valid[None], bias, NEG_INF)
        tabs.append(bias.reshape(H_B, NA_Q, NA_KEYS))
    return jnp.stack(tabs)


def _natten(qb, kb, vb, tabs, g):
    b, seq, _ = qb.shape
    rows = seq // GRID_W
    nrb = rows // NA_QROWS
    assert rows >= NA_KROWS and nrb >= 5

    def variant(rb):
        return jnp.where(rb < 2, rb, jnp.where(rb >= nrb - 2, rb - (nrb - 5), 2))

    whole = pl.BlockSpec((1, seq, DB), lambda bi, rb: (bi, 0, 0))
    blk = pl.BlockSpec((1, NA_Q, DB), lambda bi, rb: (bi, rb, 0))
    return pl.pallas_call(
        functools.partial(_natten_kernel, rows=rows),
        grid=(b, nrb),
        in_specs=[blk, whole, whole,
                  pl.BlockSpec((1, H_B, NA_Q, NA_KEYS), lambda bi, rb: (variant(rb), 0, 0, 0)),
                  pl.BlockSpec(g.shape, lambda bi, rb: (0, 0))],
        out_specs=blk,
        out_shape=jax.ShapeDtypeStruct((b, seq, DB), BF16),
        scratch_shapes=[pltpu.VMEM((H_B, NA_Q, NA_KEYS), F32), pltpu.VMEM((H_B, NA_Q, NA_KEYS), BF16)],
        compiler_params=_cparams(("parallel", "arbitrary")),
        name="natten",
    )(qb, kb, vb, tabs, g)


def _outproj_kernel(x_ref, oa_ref, ob_ref, wo_ref, g_ref, wr_ref, x1_ref, h_ref, aff_ref):
    x1 = (x_ref[...]
          + jnp.dot(oa_ref[...], wo_ref[0:DQA, :], preferred_element_type=F32)
          + jnp.dot(ob_ref[...], wo_ref[DQA:DQA + DB, :], preferred_element_type=F32))
    x1_ref[...] = x1
    ms = jnp.mean(x1 * x1, axis=-1, keepdims=True)
    hf = x1 * lax.rsqrt(ms + RMS_EPS) * g_ref[...]
    h_hi = hf.astype(BF16)
    h_lo = (hf - h_hi.astype(F32)).astype(BF16)
    h_ref[...] = _pack_rows(hf)
    l_hi = lax.dot_general(wr_ref[...], h_hi, _NT, preferred_element_type=F32)
    l_lo = lax.dot_general(wr_ref[0:N_EXPERTS, :], h_lo, _NT, preferred_element_type=F32)
    logits = l_hi[0:N_EXPERTS] + l_hi[N_EXPERTS:2 * N_EXPERTS] + l_lo
    m = jnp.max(logits, axis=0, keepdims=True)
    e = jnp.exp(logits - m)
    aff_ref[...] = e / jnp.sum(e, axis=0, keepdims=True)


def _outproj(x2, oa, ob, wo_bf, g, wr_t):
    n = x2.shape[0]
    tok = lambda w: pl.BlockSpec((TM, w), lambda i: (i, 0))
    full = lambda a: pl.BlockSpec(a.shape, lambda i: (0,) * a.ndim)
    return pl.pallas_call(
        _outproj_kernel,
        grid=(n // TM,),
        in_specs=[tok(D_MODEL), tok(DQA), tok(DB), full(wo_bf), full(g), full(wr_t)],
        out_specs=[tok(D_MODEL), tok(HALF), pl.BlockSpec((N_EXPERTS, TM), lambda i: (0, i))],
        out_shape=[jax.ShapeDtypeStruct((n, D_MODEL), F32), jax.ShapeDtypeStruct((n, HALF), U32),
                   jax.ShapeDtypeStruct((N_EXPERTS, n), F32)],
        compiler_params=_cparams(("parallel",)),
        name="outproj_router",
    )(x2, oa, ob, wo_bf, g, wr_t)


def _cumsum_tokens(x, u_ref, l_ref):
    within = jnp.dot(x.astype(BF16), u_ref[...], preferred_element_type=F32)
    rowtot = jnp.broadcast_to(within[:, LANES - 1:LANES], within.shape).astype(BF16)
    return within + jnp.dot(l_ref[...], rowtot, preferred_element_type=F32)


def _route_select_kernel(aff_ref, u_ref, l_ref, pos_ref, cexcl_ref, sincl_ref, sexcl_ref, run_ref, *, cap):
    e = pl.program_id(0)
    bits = pltpu.bitcast(aff_ref[0], jnp.int32)

    def count(mask):
        c = jnp.sum(jnp.where(mask, 1.0, 0.0), axis=0, keepdims=True)
        return jnp.sum(c, axis=1, keepdims=True)

    def step(i, t):
        cand = t | jnp.left_shift(jnp.int32(1), 30 - i)
        return jnp.where(count(bits >= cand) >= cap, cand, t)

    thr = lax.fori_loop(0, 31, step, jnp.zeros((1, 1), jnp.int32))
    gt = bits > thr
    eq = bits == thr
    need = cap - count(gt)
    eqf = jnp.where(eq, 1.0, 0.0)
    tie_rank = _cumsum_tokens(eqf, u_ref, l_ref) - eqf
    sel = jnp.where(gt | (eq & (tie_rank < need)), 1.0, 0.0)
    pos = _cumsum_tokens(sel, u_ref, l_ref)

    @pl.when(e == 0)
    def _():
        run_ref[...] = jnp.zeros_like(run_ref)

    cnt_before = run_ref[0]
    s_incl = run_ref[1] + pos
    pos_ref[0] = pos
    cexcl_ref[0] = cnt_before
    cnt = cnt_before + sel
    run_ref[0] = cnt
    run_ref[1] = s_incl
    sincl_ref[0] = s_incl
    sexcl_ref[0] = s_incl - cnt


def _route_select(aff3, cap):
    e, r, _ = aff3.shape
    u = jnp.asarray(np.triu(np.ones((LANES, LANES), np.float32)), BF16)
    lo = jnp.asarray(np.tril(np.ones((r, r), np.float32), -1), BF16)
    per_e = pl.BlockSpec((1, r, LANES), lambda ei: (ei, 0, 0))
    shared = pl.BlockSpec((1, r, LANES), lambda ei: (0, 0, 0))
    full = lambda a: pl.BlockSpec(a.shape, lambda ei: (0,) * a.ndim)
    return pl.pallas_call(
        functools.partial(_route_select_kernel, cap=cap),
        grid=(e,),
        in_specs=[per_e, full(u), full(lo)],
        out_specs=[per_e, per_e, shared, shared],
        out_shape=[jax.ShapeDtypeStruct((e, r, LANES), F32), jax.ShapeDtypeStruct((e, r, LANES), F32),
                   jax.ShapeDtypeStruct((1, r, LANES), F32), jax.ShapeDtypeStruct((1, r, LANES), F32)],
        scratch_shapes=[pltpu.VMEM((2, r, LANES), F32)],
        compiler_params=_cparams(("arbitrary",)),
        name="route_select",
    )(aff3, u, lo)


SEARCH_CHUNK = 1024
_INT3, _INT1, _F32 = "int3", "int1", "f32"
_NPARTS = {_INT3: 3, _INT1: 1, _F32: 3}


def _bf16_parts(x, kind):
    if kind == _INT1:
        return [x]
    if kind == _INT3:
        d2 = jnp.floor(x * (1.0 / 65536.0))
        r = x - d2 * 65536.0
        d1 = jnp.floor(r * (1.0 / 256.0))
        return [r - d1 * 256.0, d1, d2]
    a1 = x.astype(BF16).astype(F32)
    r1 = x - a1
    a2 = r1.astype(BF16).astype(F32)
    return [a1, a2, r1 - a2]


def _join_parts(parts, kind):
    if kind == _INT1:
        return parts[0]
    if kind == _INT3:
        return parts[0] + 256.0 * parts[1] + 65536.0 * parts[2]
    return (parts[0] + parts[1]) + parts[2]


def _rank_search_kernel(*refs, kinds, chunk):
    npay = len(kinds)
    cnt_ref, pay_refs = refs[0], refs[1:1 + npay]
    tok_ref, out_refs = refs[1 + npay], refs[2 + npay:2 + 2 * npay]
    lhs_ref = refs[-1]
    c = pl.program_id(1)
    r = cnt_ref.shape[1]
    all_kinds = (_INT3,) + tuple(kinds)

    @pl.when(c == 0)
    def _():
        row = 0
        for ref, kind in zip((cnt_ref,) + tuple(pay_refs), all_kinds):
            for part in _bf16_parts(ref[0], kind):
                lhs_ref[row:row + LANES, :] = part.T.astype(BF16)
                row += LANES

    target = (c * chunk + 1 + lax.broadcasted_iota(jnp.int32, (1, chunk), 1)).astype(F32)
    row_end = cnt_ref[0, :, LANES - 1:LANES]
    before = row_end < target
    rho = jnp.sum(jnp.where(before, 1.0, 0.0), axis=0, keepdims=True).astype(jnp.int32)
    onehot = jnp.where(lax.broadcasted_iota(jnp.int32, (r, chunk), 0) == rho, 1.0, 0.0).astype(BF16)
    fetched = jnp.dot(lhs_ref[...], onehot, preferred_element_type=F32)

    def take(first_part, kind):
        parts = [fetched[(first_part + k) * LANES:(first_part + k + 1) * LANES] for k in range(_NPARTS[kind])]
        return _join_parts(parts, kind)

    lam = jnp.sum(jnp.where(take(0, _INT3) < target, 1.0, 0.0), axis=0, keepdims=True).astype(jnp.int32)
    tok_ref[0] = rho * LANES + lam
    at_lane = lax.broadcasted_iota(jnp.int32, (LANES, chunk), 0) == lam
    first_part = _NPARTS[_INT3]
    for out_ref, kind in zip(out_refs, kinds):
        out_ref[0] = jnp.sum(jnp.where(at_lane, take(first_part, kind), 0.0), axis=0, keepdims=True)
        first_part += _NPARTS[kind]


def _rank_search(counts, nslots, payloads=()):
    g, r, _ = counts.shape
    chunk = min(SEARCH_CHUNK, nslots)
    kinds = tuple(k for _, k in payloads)
    nparts = _NPARTS[_INT3] + sum(_NPARTS[k] for k in kinds)

    def in_spec(a):
        if a.shape[0] == 1:
            return pl.BlockSpec((1, r, LANES), lambda gi, ci: (0, 0, 0))
        return pl.BlockSpec((1, r, LANES), lambda gi, ci: (gi, 0, 0))

    out_spec = pl.BlockSpec((1, 1, chunk), lambda gi, ci: (gi, 0, ci))
    outs = pl.pallas_call(
        functools.partial(_rank_search_kernel, kinds=kinds, chunk=chunk),
        grid=(g, nslots // chunk),
        in_specs=[in_spec(counts)] + [in_spec(a) for a, _ in payloads],
        out_specs=[out_spec] * (1 + len(kinds)),
        out_shape=[jax.ShapeDtypeStruct((g, 1, nslots), jnp.int32)]
        + [jax.ShapeDtypeStruct((g, 1, nslots), F32)] * len(kinds),
        scratch_shapes=[pltpu.VMEM((nparts * LANES, r), BF16)],
        compiler_params=_cparams(("parallel", "arbitrary")),
        name="rank_search",
    )(counts, *[a for a, _ in payloads])
    return outs[0], outs[1:]


def _sc_scatter_rows(rows, idx):
    m, w = rows.shape
    nchunks = m // GATHER_ROWS
    per_worker = nchunks // SC_WORKERS
    assert per_worker * SC_WORKERS * GATHER_ROWS == m and per_worker % 8 == 0
    mesh = plsc.VectorSubcoreMesh(core_axis_name="c", subcore_axis_name="s")

    @functools.partial(
        pl.kernel, mesh=mesh, out_type=jax.ShapeDtypeStruct((m, w), rows.dtype),
        scratch_types=[pltpu.VMEM((per_worker, GATHER_ROWS), jnp.int32),
                       pltpu.VMEM((GATHER_ROWS, w), rows.dtype),
                       pltpu.SemaphoreType.DMA])
    def scatter(rows_hbm, idx_hbm, out_hbm, idx_v, rows_v, sem):
        wid = lax.axis_index("s") * SC_CORES + lax.axis_index("c")
        first = wid * per_worker
        pltpu.sync_copy(idx_hbm.at[pl.ds(first, per_worker)], idx_v)

        @pl.loop(0, per_worker)
        def _(c):
            pltpu.sync_copy(rows_hbm.at[pl.ds((first + c) * GATHER_ROWS, GATHER_ROWS)], rows_v)
            pltpu.async_copy(rows_v, out_hbm.at[idx_v.at[c]], sem).wait()

    return scatter(rows, idx.reshape(nchunks, GATHER_ROWS))


def _moe_kernel(x_ref, gate_ref, wg_ref, wu_ref, wd_ref, o_ref):
    x = _unpack_rows(x_ref[0])
    y = jnp.zeros((x.shape[0], D_MODEL), F32)
    for c in range(D_EXPERT // FC):
        sl = slice(c * FC, (c + 1) * FC)
        a = jnp.dot(x, wg_ref[0, :, sl], preferred_element_type=F32)
        u = jnp.dot(x, wu_ref[0, :, sl], preferred_element_type=F32)
        hid = (a * jax.nn.sigmoid(a) * u).astype(BF16)
        y = y + jnp.dot(hid, wd_ref[0, sl, :], preferred_element_type=F32)
    o_ref[0] = _pack_rows(y * gate_ref[0])


def _moe(xe, gate3, wg, wu, wd):
    e, cap, _ = xe.shape
    tc = min(TC, cap)
    tokb = lambda w: pl.BlockSpec((1, tc, w), lambda ei, i: (ei, i, 0))
    wspec = lambda a: pl.BlockSpec((1,) + a.shape[1:], lambda ei, i: (ei, 0, 0))
    return pl.pallas_call(
        _moe_kernel,
        grid=(e, cap // tc),
        in_specs=[tokb(HALF), tokb(1), wspec(wg), wspec(wu), wspec(wd)],
        out_specs=tokb(HALF),
        out_shape=jax.ShapeDtypeStruct((e, cap, HALF), U32),
        compiler_params=_cparams(("parallel", "arbitrary")),
        name="moe_ffn",
    )(xe, gate3, wg, wu, wd)


_FIRST, _LAST, _ACTIVE = 1, 2, 4


def _combine_kernel(tile_ref, blk_ref, flag_ref, tok_ref, yg_ref, x1_ref, g_ref, o_ref, acc_ref):
    s = pl.program_id(0)
    flag = flag_ref[s]

    @pl.when((flag & _FIRST) != 0)
    def _():
        acc_ref[...] = jnp.zeros_like(acc_ref)

    @pl.when((flag & _ACTIVE) != 0)
    def _():
        tok_row = tile_ref[s] * TS + lax.broadcasted_iota(jnp.int32, (TS, SB), 0)
        onehot = jnp.where(tok_ref[0] == tok_row, 1.0, 0.0).astype(BF16)
        acc_ref[...] += jnp.dot(onehot, _unpack_rows(yg_ref[...]), preferred_element_type=F32)

    @pl.when((flag & _LAST) != 0)
    def _():
        y = x1_ref[...] + acc_ref[...]
        ms = jnp.mean(y * y, axis=-1, keepdims=True)
        o_ref[...] = y * lax.rsqrt(ms + RMS_EPS) * g_ref[...]


def _combine_schedule(bounds, nslots):
    nt, nblk = bounds.shape[0] - 1, nslots // SB
    steps = nt + nblk
    lo, hi = bounds[:-1], bounds[1:]
    b_lo = jnp.minimum(lo // SB, nblk - 1)
    b_hi = jnp.where(hi > lo, (hi - 1) // SB, b_lo)
    nst = b_hi - b_lo + 1
    cum = jnp.cumsum(nst)
    start = cum - nst
    s = jnp.arange(steps, dtype=jnp.int32)
    tile = jnp.minimum(jnp.searchsorted(cum, s, side="right").astype(jnp.int32), nt - 1)
    active = s < cum[-1]
    blk = jnp.where(active, b_lo[tile] + s - start[tile], b_hi[nt - 1])
    first = active & (s == start[tile])
    last = active & (s == cum[tile] - 1)
    flag = first * _FIRST + last * _LAST + active * _ACTIVE
    return tile, blk.astype(jnp.int32), flag.astype(jnp.int32)


def _combine(tile, blk, flag, tok3, yg, x1, g):
    n = x1.shape[0]
    steps = tile.shape[0]
    grid_spec = pltpu.PrefetchScalarGridSpec(
        num_scalar_prefetch=3,
        grid=(steps,),
        in_specs=[pl.BlockSpec((1, 1, SB), lambda s, t, b, f: (b[s], 0, 0)),
                  pl.BlockSpec((SB, HALF), lambda s, t, b, f: (b[s], 0)),
                  pl.BlockSpec((TS, D_MODEL), lambda s, t, b, f: (t[s], 0)),
                  pl.BlockSpec(g.shape, lambda s, t, b, f: (0, 0))],
        out_specs=pl.BlockSpec((TS, D_MODEL), lambda s, t, b, f: (t[s], 0)),
        scratch_shapes=[pltpu.VMEM((TS, D_MODEL), F32)],
    )
    return pl.pallas_call(
        _combine_kernel,
        grid_spec=grid_spec,
        out_shape=jax.ShapeDtypeStruct((n, D_MODEL), F32),
        compiler_params=_cparams(("arbitrary",)),
        name="combine_norm",
    )(tile, blk, flag, tok3, yg, x1, g)


def _group_forward(x, p):
    b, seq, _ = x.shape
    n = b * seq
    cap = CAP_FACTOR * n // N_EXPERTS
    x2 = x.reshape(n, D_MODEL)
    cos_t, sa_t, sb_t = p["rot"]
    qa, ka, va, qb, kb, vb = _inproj(x2, p["g_attn"], p["w_in"], cos_t[:seq], sa_t[:seq], sb_t[:seq], seq)
    r3 = lambda a: a.reshape(b, seq, a.shape[-1])
    oa = _window(r3(qa), r3(ka), r3(va), p["sink"], p["g_out_a"])
    ob = _natten(r3(qb), r3(kb), r3(vb), p["na_tabs"], p["g_out_b"])
    x1, h, aff_t = _outproj(x2, oa.reshape(n, DQA), ob.reshape(n, DB), p["w_out"], p["g_ffn"], p["w_router_t"])

    nslots = N_EXPERTS * cap
    aff3 = aff_t.reshape(N_EXPERTS, n // LANES, LANES)
    pos, c_excl, s_incl, s_excl = _route_select(aff3, cap)
    idx, (gate, s_at, c_at) = _rank_search(pos, cap, [(aff3, _F32), (s_excl, _INT3), (c_excl, _INT1)])
    dest = (s_at + c_at).astype(jnp.int32).reshape(nslots)
    xe = _sc_gather_rows(h, idx.reshape(nslots)).reshape(N_EXPERTS, cap, HALF)
    ye = _moe(xe, gate.reshape(N_EXPERTS, cap, 1), p["w_gate"], p["w_up"], p["w_down"])
    yg = _sc_scatter_rows(ye.reshape(nslots, HALF), dest)

    tok_sorted, _ = _rank_search(s_incl, nslots)
    tile_end = s_incl.reshape(n)[TS - 1::TS].astype(jnp.int32)
    bounds = jnp.concatenate([jnp.zeros((1,), jnp.int32), tile_end])
    tile, blk, flag = _combine_schedule(bounds, nslots)
    y = _combine(tile, blk, flag, tok_sorted.reshape(-1, 1, SB), yg, x1, p["g_final"])
    return y.reshape(b, seq, D_MODEL)


def kernel(x_prompt, x_sample, g_attn, w_in, g_out_a, g_out_b, sink_a, rpb_b, w_out, g_ffn, w_router,
           w_gate, w_up, w_down, g_final):
    assert g_attn.shape[0] == 1, "single trunk layer"
    wr = w_router[0].T
    wr_hi = wr.astype(BF16)
    wr_lo = (wr - wr_hi.astype(F32)).astype(BF16)
    p = {
        "g_attn": g_attn[0][None, :], "w_in": w_in[0].astype(BF16),
        "g_out_a": g_out_a[0][None, :], "g_out_b": g_out_b[0][None, :],
        "sink": sink_a[0], "na_tabs": _natten_tables(rpb_b[0], 4 * NA_ROWS), "w_out": w_out[0].astype(BF16),
        "g_ffn": g_ffn[0][None, :], "w_router_t": jnp.concatenate([wr_hi, wr_lo], axis=0),
        "w_gate": w_gate[0].astype(BF16), "w_up": w_up[0].astype(BF16), "w_down": w_down[0].astype(BF16),
        "g_final": g_final[None, :],
        "rot": _rotary_tables(max(x_prompt.shape[1], x_sample.shape[1])),
    }
    return (_group_forward(x_prompt, p), _group_forward(x_sample, p))
```

```python
import functools

import jax
import jax.numpy as jnp
import numpy as np
from jax import lax
from jax.experimental import pallas as pl
from jax.experimental.pallas import tpu as pltpu
from jax.experimental.pallas import tpu_sc as plsc

D_MODEL = 1024
HEAD_DIM = 64
HQ_A = 8
HKV_A = 2
H_B = 8
GRP = HQ_A // HKV_A
DQA = HQ_A * HEAD_DIM
DKA = HKV_A * HEAD_DIM
DB = H_B * HEAD_DIM
D_IN = DQA + 2 * DKA + 3 * DB
WINDOW = 128
BLK = 128
ROPE_THETA = 500000.0
ROT_DIM = HEAD_DIM // 4
GRID_W = 64
NA_ROWS = 8
NA_COLS = 16
NA_QROWS = 2
NA_KROWS = NA_ROWS + NA_QROWS
NA_KEYS = NA_KROWS * GRID_W
NA_Q = NA_QROWS * GRID_W
N_EXPERTS = 16
CAP_FACTOR = 2
D_EXPERT = 2 * D_MODEL
RMS_EPS = 1e-6
NEG_INF = -1e30
SCALE = HEAD_DIM ** -0.5

LANES = 128
TM = 512
TQ = 1024
TC = 512
FC = 512
TS = 512
SB = 512
VMEM_LIMIT = 56 * 1024 * 1024

BF16 = jnp.bfloat16
F32 = jnp.float32
_NT = (((1,), (1,)), ((), ()))


def _cparams(sem):
    return pltpu.CompilerParams(dimension_semantics=sem, vmem_limit_bytes=VMEM_LIMIT)


HALF = D_MODEL // 2
U32 = jnp.uint32


def _pack_rows(v):
    r = v.astype(BF16).astype(F32)
    hi = pltpu.bitcast(r[:, :HALF], U32)
    lo = pltpu.bitcast(r[:, HALF:], U32)
    return (hi & jnp.uint32(0xFFFF0000)) | (lo >> 16)


def _unpack_rows(w):
    hi = pltpu.bitcast(w & jnp.uint32(0xFFFF0000), F32)
    lo = pltpu.bitcast(w << 16, F32)
    return jnp.concatenate([hi, lo], axis=1).astype(BF16)


SC_CORES = 2
SC_SUBCORES = 16
SC_WORKERS = SC_CORES * SC_SUBCORES
GATHER_ROWS = 64


def _sc_gather_rows(table, idx):
    m, w = idx.shape[0], table.shape[1]
    nchunks = m // GATHER_ROWS
    per_worker = nchunks // SC_WORKERS
    assert per_worker * SC_WORKERS * GATHER_ROWS == m and per_worker % 8 == 0
    mesh = plsc.VectorSubcoreMesh(core_axis_name="c", subcore_axis_name="s")

    @functools.partial(
        pl.kernel, mesh=mesh, out_type=jax.ShapeDtypeStruct((m, w), table.dtype),
        scratch_types=[pltpu.VMEM((per_worker, GATHER_ROWS), jnp.int32),
                       pltpu.VMEM((GATHER_ROWS, w), table.dtype),
                       pltpu.SemaphoreType.DMA])
    def gather(tab_hbm, idx_hbm, out_hbm, idx_v, rows_v, sem):
        wid = lax.axis_index("s") * SC_CORES + lax.axis_index("c")
        first = wid * per_worker
        pltpu.sync_copy(idx_hbm.at[pl.ds(first, per_worker)], idx_v)

        @pl.loop(0, per_worker)
        def _(c):
            pltpu.async_copy(tab_hbm.at[idx_v.at[c]], rows_v, sem).wait()
            pltpu.sync_copy(rows_v, out_hbm.at[pl.ds((first + c) * GATHER_ROWS, GATHER_ROWS)])

    return gather(table, idx.reshape(nchunks, GATHER_ROWS))


def _inproj_kernel(x_ref, g_ref, w_ref, cos_ref, sa_ref, sb_ref,
                   qa_ref, ka_ref, va_ref, qb_ref, kb_ref, vb_ref):
    x = x_ref[...]
    ms = jnp.mean(x * x, axis=-1, keepdims=True)
    hn = (x * lax.rsqrt(ms + RMS_EPS) * g_ref[...]).astype(BF16)
    c, sa, sb = cos_ref[...], sa_ref[...], sb_ref[...]

    def proj(lo, n):
        return jnp.dot(hn, w_ref[:, lo:lo + n], preferred_element_type=F32)

    def rot(blk):
        return blk * c + pltpu.roll(blk, LANES - ROT_DIM // 2, 1) * sa + pltpu.roll(blk, ROT_DIM // 2, 1) * sb

    for j in range(DQA // LANES):
        qa_ref[:, j * LANES:(j + 1) * LANES] = (rot(proj(j * LANES, LANES)) * SCALE).astype(BF16)
    ka_ref[...] = rot(proj(DQA, DKA)).astype(BF16)
    va_ref[...] = proj(DQA + DKA, DKA).astype(BF16)
    qb_ref[...] = (proj(DQA + 2 * DKA, DB) * SCALE).astype(BF16)
    kb_ref[...] = proj(DQA + 2 * DKA + DB, DB).astype(BF16)
    vb_ref[...] = proj(DQA + 2 * DKA + 2 * DB, DB).astype(BF16)


def _inproj(x2, g, w_bf, cos_t, sa_t, sb_t, seq):
    n = x2.shape[0]
    per_seq = seq // TM
    tok = lambda w: pl.BlockSpec((TM, w), lambda i: (i, 0))
    full = lambda a: pl.BlockSpec(a.shape, lambda i: (0,) * a.ndim)
    pos = pl.BlockSpec((TM, LANES), lambda i: (i % per_seq, 0))
    widths = (DQA, DKA, DKA, DB, DB, DB)
    return pl.pallas_call(
        _inproj_kernel,
        grid=(n // TM,),
        in_specs=[tok(D_MODEL), full(g), full(w_bf), pos, pos, pos],
        out_specs=[tok(w) for w in widths],
        out_shape=[jax.ShapeDtypeStruct((n, w), BF16) for w in widths],
        compiler_params=_cparams(("parallel",)),
        name="inproj",
    )(x2, g, w_bf, cos_t, sa_t, sb_t)


def _rotary_tables(seq):
    half = ROT_DIM // 2
    inv_freq = jnp.float32(ROPE_THETA) ** (-(jnp.arange(half, dtype=F32) * 2.0) / ROT_DIM)
    ang = jnp.arange(seq, dtype=F32)[:, None] * inv_freq[None, :]
    cos, sin = jnp.cos(ang), jnp.sin(ang)
    ones = jnp.ones((seq, HEAD_DIM - ROT_DIM), F32)
    zeros = jnp.zeros((seq, HEAD_DIM - ROT_DIM), F32)
    zh = jnp.zeros((seq, half), F32)
    rep = LANES // HEAD_DIM
    cos_t = jnp.tile(jnp.concatenate([cos, cos, ones], axis=1), (1, rep))
    sa_t = jnp.tile(jnp.concatenate([-sin, zh, zeros], axis=1), (1, rep))
    sb_t = jnp.tile(jnp.concatenate([zh, sin, zeros], axis=1), (1, rep))
    return cos_t, sa_t, sb_t


SM_ROWS = 32


def _sm_chunks(shape):
    nh, nq, nk = shape
    chunks = [(h, slice(r0, r0 + SM_ROWS)) for h in range(nh) for r0 in range(0, nq, SM_ROWS)]
    return chunks, [slice(l0, l0 + LANES) for l0 in range(0, nk, LANES)]


def _softmax_max(score, shape, pm_ref, mb_ref, floor=None):
    chunks, lanes = _sm_chunks(shape)
    for h, rc in chunks:
        pm_ref[h, rc, :] = functools.reduce(jnp.maximum, [score(h, rc, ls) for ls in lanes])
    m = jnp.max(pm_ref[...], axis=-1, keepdims=True)
    if floor is not None:
        m = jnp.maximum(m, floor)
    mb_ref[...] = jnp.broadcast_to(m, mb_ref.shape)
    return m


def _softmax_exp(score, p_ref, mb_ref, ps_ref):
    chunks, lanes = _sm_chunks(p_ref.shape)
    for h, rc in chunks:
        mb = mb_ref[h, rc, :]
        total = None
        for ls in lanes:
            p = jnp.exp(score(h, rc, ls) - mb)
            p_ref[h, rc, ls] = p.astype(BF16)
            total = p if total is None else total + p
        ps_ref[h, rc, :] = total
    return jnp.sum(ps_ref[...], axis=-1, keepdims=True)


def _window_kernel(q_ref, kp_ref, kc_ref, kn_ref, vp_ref, vc_ref, vn_ref, sink_ref, g_ref, o_ref,
                   k_ref, v_ref, s_ref, p_ref, b_ref, pm_ref, mb_ref, ps_ref, *, nb):
    i = pl.program_id(1)
    nsub = TQ // BLK
    k_ref[0:BLK] = kp_ref[0]
    k_ref[BLK:BLK + TQ] = kc_ref[0]
    k_ref[BLK + TQ:TQ + 2 * BLK] = kn_ref[0]
    v_ref[0:BLK] = vp_ref[0]
    v_ref[BLK:BLK + TQ] = vc_ref[0]
    v_ref[BLK + TQ:TQ + 2 * BLK] = vn_ref[0]
    qi = lax.broadcasted_iota(jnp.int32, (BLK, 3 * BLK), 0)
    kj = lax.broadcasted_iota(jnp.int32, (BLK, 3 * BLK), 1)
    band = jnp.abs(kj - BLK - qi) <= WINDOW
    sink = sink_ref[...]

    def scores(j):
        n = i * nsub + j
        k_lo = jnp.where(n >= 1, 0, BLK)
        k_hi = jnp.where(n <= nb - 2, 3 * BLK, 2 * BLK)
        b_ref[j % 3] = jnp.where(band & (kj >= k_lo) & (kj < k_hi), 0.0, NEG_INF)
        for h in range(HKV_A):
            kh = k_ref[j * BLK:(j + 3) * BLK, h * HEAD_DIM:(h + 1) * HEAD_DIM]
            qs = jnp.concatenate([q_ref[0, j * BLK:(j + 1) * BLK, hd * HEAD_DIM:(hd + 1) * HEAD_DIM]
                                  for hd in range(h * GRP, (h + 1) * GRP)], axis=0)
            s = lax.dot_general(qs, kh, _NT, preferred_element_type=F32)
            s_ref[j % 3, h * GRP:(h + 1) * GRP] = s.reshape(GRP, BLK, 3 * BLK)

    def score_fn(j):
        sb, bb = s_ref.at[j % 3], b_ref.at[j % 3]
        return lambda hd, rc, ls: sb[hd, rc, ls] + bb[rc, ls]

    def values(j, denom):
        outs = []
        for h in range(HKV_A):
            vh = v_ref[j * BLK:(j + 3) * BLK, h * HEAD_DIM:(h + 1) * HEAD_DIM]
            ph = p_ref[j % 2, h * GRP:(h + 1) * GRP].reshape(GRP * BLK, 3 * BLK)
            o = jnp.dot(ph, vh, preferred_element_type=F32)
            outs += [o[g * BLK:(g + 1) * BLK] / denom[h * GRP + g] for g in range(GRP)]
        ob = jnp.concatenate(outs, axis=1)
        ms = jnp.mean(ob * ob, axis=-1, keepdims=True)
        o_ref[0, j * BLK:(j + 1) * BLK, :] = (ob * lax.rsqrt(ms + RMS_EPS) * g_ref[...]).astype(BF16)

    maxima, denoms = {}, {}
    for t in range(nsub + 3):
        if t < nsub:
            scores(t)
        if 0 <= t - 2 < nsub:
            j = t - 2
            rowsum = _softmax_exp(score_fn(j), p_ref.at[j % 2], mb_ref.at[j % 2], ps_ref.at[j % 2])
            denoms[j] = rowsum + jnp.exp(sink - maxima.pop(j))
        if 0 <= t - 3 < nsub:
            values(t - 3, denoms.pop(t - 3))
        if 0 <= t - 1 < nsub:
            j = t - 1
            maxima[j] = _softmax_max(score_fn(j), s_ref.shape[1:], pm_ref.at[j % 2], mb_ref.at[j % 2], floor=sink)


def _window(qa, ka, va, sink, g):
    b, seq, _ = qa.shape
    nb = seq // BLK
    r = TQ // BLK
    cur = lambda w: pl.BlockSpec((1, TQ, w), lambda bi, i: (bi, i, 0))
    prev = pl.BlockSpec((1, BLK, DKA), lambda bi, i: (bi, jnp.maximum(i * r - 1, 0), 0))
    nxt = pl.BlockSpec((1, BLK, DKA), lambda bi, i: (bi, jnp.minimum((i + 1) * r, nb - 1), 0))
    sink3 = sink.reshape(HQ_A, 1, 1)
    return pl.pallas_call(
        functools.partial(_window_kernel, nb=nb),
        grid=(b, seq // TQ),
        in_specs=[cur(DQA), prev, cur(DKA), nxt, prev, cur(DKA), nxt,
                  pl.BlockSpec(sink3.shape, lambda bi, i: (0, 0, 0)), pl.BlockSpec(g.shape, lambda bi, i: (0, 0))],
        out_specs=cur(DQA),
        out_shape=jax.ShapeDtypeStruct((b, seq, DQA), BF16),
        scratch_shapes=[pltpu.VMEM((TQ + 2 * BLK, DKA), BF16), pltpu.VMEM((TQ + 2 * BLK, DKA), BF16),
                        pltpu.VMEM((3, HQ_A, BLK, 3 * BLK), F32), pltpu.VMEM((2, HQ_A, BLK, 3 * BLK), BF16),
                        pltpu.VMEM((3, BLK, 3 * BLK), F32)] + [pltpu.VMEM((2, HQ_A, BLK, LANES), F32)] * 3,
        compiler_params=_cparams(("parallel", "parallel")),
        name="window_attn",
    )(qa, ka, ka, ka, va, va, va, sink3, g)


NA_RB = 8
NA_TILE = NA_RB * NA_Q
NA_TILE_ROWS = NA_RB * NA_QROWS


def _natten_kernel(q_ref, kp_ref, kc_ref, kn_ref, vp_ref, vc_ref, vn_ref, tab_ref, g_ref, o_ref,
                   k_ref, v_ref, s_ref, p_ref, pm_ref, mb_ref, ps_ref, *, rows):
    i = pl.program_id(1)
    nrb = rows // NA_QROWS
    for dst, parts in ((k_ref, (kp_ref, kc_ref, kn_ref)), (v_ref, (vp_ref, vc_ref, vn_ref))):
        for t, part in enumerate(parts):
            dst[t * NA_TILE:(t + 1) * NA_TILE] = part[0]

    def block(r):
        rb = i * NA_RB + r
        kstart = jnp.clip(NA_QROWS * rb - NA_ROWS // 2, 0, rows - NA_KROWS)
        off = (kstart - (i - 1) * NA_TILE_ROWS) * GRID_W
        variant = jnp.where(rb < 2, rb, jnp.where(rb >= nrb - 2, rb - (nrb - 5), 2))
        return pl.ds(pl.multiple_of(off, GRID_W), NA_KEYS), variant

    def scores(r):
        keys, _ = block(r)
        for h in range(H_B):
            sl = slice(h * HEAD_DIM, (h + 1) * HEAD_DIM)
            s_ref[r % 3, h] = lax.dot_general(q_ref[0, r * NA_Q:(r + 1) * NA_Q, sl], k_ref[keys, sl], _NT,
                                              preferred_element_type=F32)

    def score_fn(r):
        _, variant = block(r)
        sb = s_ref.at[r % 3]
        return lambda h, rc, ls: sb[h, rc, ls] + tab_ref[variant, h, rc, ls]

    def values(r, denom):
        keys, _ = block(r)
        outs = []
        for h in range(H_B):
            sl = slice(h * HEAD_DIM, (h + 1) * HEAD_DIM)
            outs.append(jnp.dot(p_ref[r % 2, h], v_ref[keys, sl], preferred_element_type=F32) / denom[h])
        ob = jnp.concatenate(outs, axis=1)
        ms = jnp.mean(ob * ob, axis=-1, keepdims=True)
        o_ref[0, r * NA_Q:(r + 1) * NA_Q, :] = (ob * lax.rsqrt(ms + RMS_EPS) * g_ref[...]).astype(BF16)

    denoms = {}
    for t in range(NA_RB + 3):
        if t < NA_RB:
            scores(t)
        if 0 <= t - 2 < NA_RB:
            r = t - 2
            denoms[r] = _softmax_exp(score_fn(r), p_ref.at[r % 2], mb_ref.at[r % 2], ps_ref.at[r % 2])
        if 0 <= t - 3 < NA_RB:
            values(t - 3, denoms.pop(t - 3))
        if 0 <= t - 1 < NA_RB:
            r = t - 1
            _softmax_max(score_fn(r), s_ref.shape[1:], pm_ref.at[r % 2], mb_ref.at[r % 2])


def _natten_variant_rowblocks(nrb):
    return (0, 1, 2, nrb - 2, nrb - 1)


def _natten_tables(rpb, rows):
    nrb = rows // NA_QROWS
    qc = np.arange(GRID_W)
    kc = np.arange(GRID_W)
    cs = np.clip(qc - NA_COLS // 2, 0, GRID_W - NA_COLS)
    col_ok = (kc[None, :] >= cs[:, None]) & (kc[None, :] < cs[:, None] + NA_COLS)
    cidx = np.clip(kc[None, :] - qc[:, None] + NA_COLS - 1, 0, 2 * NA_COLS - 2)
    tabs = []
    for rb in _natten_variant_rowblocks(nrb):
        r0 = rb * NA_QROWS
        qrows = r0 + np.arange(NA_QROWS)
        rs = np.clip(qrows - NA_ROWS // 2, 0, rows - NA_ROWS)
        kstart = int(np.clip(r0 - NA_ROWS // 2, 0, rows - NA_KROWS))
        krows = kstart + np.arange(NA_KROWS)
        row_ok = (krows[None, :] >= rs[:, None]) & (krows[None, :] < rs[:, None] + NA_ROWS)
        ridx = np.clip(krows[None, :] - qrows[:, None] + NA_ROWS - 1, 0, 2 * NA_ROWS - 2)
        valid = row_ok[:, None, :, None] & col_ok[None, :, None, :]
        r1h = (ridx[:, :, None] == np.arange(2 * NA_ROWS - 1)).astype(np.float32)
        c1h = (cidx[:, :, None] == np.arange(2 * NA_COLS - 1)).astype(np.float32)
        bias = jnp.einsum("rka,hab,qcb->hrqkc", r1h, rpb.astype(F32), c1h, precision=lax.Precision.HIGHEST)
        bias = jnp.where(valid[None], bias, NEG_INF)
        tabs.append(bias.reshape(H_B, NA_Q, NA_KEYS))
    return jnp.stack(tabs)


def _natten(qb, kb, vb, tabs, g):
    b, seq, _ = qb.shape
    rows = seq // GRID_W
    nrb = rows // NA_QROWS
    nt = seq // NA_TILE
    assert rows >= NA_KROWS and nrb >= 5 and nt * NA_TILE == seq
    cur = pl.BlockSpec((1, NA_TILE, DB), lambda bi, i: (bi, i, 0))
    prev = pl.BlockSpec((1, NA_TILE, DB), lambda bi, i: (bi, jnp.maximum(i - 1, 0), 0))
    nxt = pl.BlockSpec((1, NA_TILE, DB), lambda bi, i: (bi, jnp.minimum(i + 1, nt - 1), 0))
    return pl.pallas_call(
        functools.partial(_natten_kernel, rows=rows),
        grid=(b, nt),
        in_specs=[cur, prev, cur, nxt, prev, cur, nxt,
                  pl.BlockSpec(tabs.shape, lambda bi, i: (0, 0, 0, 0)),
                  pl.BlockSpec(g.shape, lambda bi, i: (0, 0))],
        out_specs=cur,
        out_shape=jax.ShapeDtypeStruct((b, seq, DB), BF16),
        scratch_shapes=[pltpu.VMEM((3 * NA_TILE, DB), BF16), pltpu.VMEM((3 * NA_TILE, DB), BF16),
                        pltpu.VMEM((3, H_B, NA_Q, NA_KEYS), F32), pltpu.VMEM((2, H_B, NA_Q, NA_KEYS), BF16),
                        ] + [pltpu.VMEM((2, H_B, NA_Q, LANES), F32)] * 3,
        compiler_params=_cparams(("parallel", "arbitrary")),
        name="natten",
    )(qb, kb, kb, kb, vb, vb, vb, tabs, g)


def _outproj_kernel(x_ref, oa_ref, ob_ref, wo_ref, g_ref, wr_ref, x1_ref, h_ref, aff_ref):
    x1 = (x_ref[...]
          + jnp.dot(oa_ref[...], wo_ref[0:DQA, :], preferred_element_type=F32)
          + jnp.dot(ob_ref[...], wo_ref[DQA:DQA + DB, :], preferred_element_type=F32))
    x1_ref[...] = x1
    ms = jnp.mean(x1 * x1, axis=-1, keepdims=True)
    hf = x1 * lax.rsqrt(ms + RMS_EPS) * g_ref[...]
    h_hi = hf.astype(BF16)
    h_lo = (hf - h_hi.astype(F32)).astype(BF16)
    h_ref[...] = _pack_rows(hf)
    l_hi = lax.dot_general(wr_ref[...], h_hi, _NT, preferred_element_type=F32)
    l_lo = lax.dot_general(wr_ref[0:N_EXPERTS, :], h_lo, _NT, preferred_element_type=F32)
    logits = l_hi[0:N_EXPERTS] + l_hi[N_EXPERTS:2 * N_EXPERTS] + l_lo
    m = jnp.max(logits, axis=0, keepdims=True)
    e = jnp.exp(logits - m)
    aff_ref[...] = e / jnp.sum(e, axis=0, keepdims=True)


def _outproj(x2, oa, ob, wo_bf, g, wr_t):
    n = x2.shape[0]
    tok = lambda w: pl.BlockSpec((TM, w), lambda i: (i, 0))
    full = lambda a: pl.BlockSpec(a.shape, lambda i: (0,) * a.ndim)
    return pl.pallas_call(
        _outproj_kernel,
        grid=(n // TM,),
        in_specs=[tok(D_MODEL), tok(DQA), tok(DB), full(wo_bf), full(g), full(wr_t)],
        out_specs=[tok(D_MODEL), tok(HALF), pl.BlockSpec((N_EXPERTS, TM), lambda i: (0, i))],
        out_shape=[jax.ShapeDtypeStruct((n, D_MODEL), F32), jax.ShapeDtypeStruct((n, HALF), U32),
                   jax.ShapeDtypeStruct((N_EXPERTS, n), F32)],
        compiler_params=_cparams(("parallel",)),
        name="outproj_router",
    )(x2, oa, ob, wo_bf, g, wr_t)


def _cumsum_tokens(x, u_ref, l_ref):
    within = jnp.dot(x.astype(BF16), u_ref[...], preferred_element_type=F32)
    rowtot = jnp.broadcast_to(within[:, LANES - 1:LANES], within.shape).astype(BF16)
    return within + jnp.dot(l_ref[...], rowtot, preferred_element_type=F32)


def _route_select_kernel(aff_ref, u_ref, l_ref, pos_ref, cexcl_ref, sincl_ref, sexcl_ref, run_ref, *, cap):
    e = pl.program_id(0)
    bits = pltpu.bitcast(aff_ref[0], jnp.int32)

    def count(mask):
        c = jnp.sum(jnp.where(mask, 1.0, 0.0), axis=0, keepdims=True)
        return jnp.sum(c, axis=1, keepdims=True)

    def step(i, t):
        cand = t | jnp.left_shift(jnp.int32(1), 30 - i)
        return jnp.where(count(bits >= cand) >= cap, cand, t)

    thr = lax.fori_loop(0, 31, step, jnp.zeros((1, 1), jnp.int32))
    gt = bits > thr
    eq = bits == thr
    need = cap - count(gt)
    eqf = jnp.where(eq, 1.0, 0.0)
    tie_rank = _cumsum_tokens(eqf, u_ref, l_ref) - eqf
    sel = jnp.where(gt | (eq & (tie_rank < need)), 1.0, 0.0)
    pos = _cumsum_tokens(sel, u_ref, l_ref)

    @pl.when(e == 0)
    def _():
        run_ref[...] = jnp.zeros_like(run_ref)

    cnt_before = run_ref[0]
    s_incl = run_ref[1] + pos
    pos_ref[0] = pos
    cexcl_ref[0] = cnt_before
    cnt = cnt_before + sel
    run_ref[0] = cnt
    run_ref[1] = s_incl
    sincl_ref[0] = s_incl
    sexcl_ref[0] = s_incl - cnt


def _route_select(aff3, cap):
    e, r, _ = aff3.shape
    u = jnp.asarray(np.triu(np.ones((LANES, LANES), np.float32)), BF16)
    lo = jnp.asarray(np.tril(np.ones((r, r), np.float32), -1), BF16)
    per_e = pl.BlockSpec((1, r, LANES), lambda ei: (ei, 0, 0))
    shared = pl.BlockSpec((1, r, LANES), lambda ei: (0, 0, 0))
    full = lambda a: pl.BlockSpec(a.shape, lambda ei: (0,) * a.ndim)
    return pl.pallas_call(
        functools.partial(_route_select_kernel, cap=cap),
        grid=(e,),
        in_specs=[per_e, full(u), full(lo)],
        out_specs=[per_e, per_e, shared, shared],
        out_shape=[jax.ShapeDtypeStruct((e, r, LANES), F32), jax.ShapeDtypeStruct((e, r, LANES), F32),
                   jax.ShapeDtypeStruct((1, r, LANES), F32), jax.ShapeDtypeStruct((1, r, LANES), F32)],
        scratch_shapes=[pltpu.VMEM((2, r, LANES), F32)],
        compiler_params=_cparams(("arbitrary",)),
        name="route_select",
    )(aff3, u, lo)


SEARCH_CHUNK = 1024
_INT3, _INT1, _F32 = "int3", "int1", "f32"
_NPARTS = {_INT3: 3, _INT1: 1, _F32: 3}


def _bf16_parts(x, kind):
    if kind == _INT1:
        return [x]
    if kind == _INT3:
        d2 = jnp.floor(x * (1.0 / 65536.0))
        r = x - d2 * 65536.0
        d1 = jnp.floor(r * (1.0 / 256.0))
        return [r - d1 * 256.0, d1, d2]
    a1 = x.astype(BF16).astype(F32)
    r1 = x - a1
    a2 = r1.astype(BF16).astype(F32)
    return [a1, a2, r1 - a2]


def _join_parts(parts, kind):
    if kind == _INT1:
        return parts[0]
    if kind == _INT3:
        return parts[0] + 256.0 * parts[1] + 65536.0 * parts[2]
    return (parts[0] + parts[1]) + parts[2]


def _rank_search_kernel(*refs, kinds, chunk):
    npay = len(kinds)
    cnt_ref, pay_refs = refs[0], refs[1:1 + npay]
    tok_ref, out_refs = refs[1 + npay], refs[2 + npay:2 + 2 * npay]
    lhs_ref = refs[-1]
    c = pl.program_id(1)
    r = cnt_ref.shape[1]
    all_kinds = (_INT3,) + tuple(kinds)

    @pl.when(c == 0)
    def _():
        row = 0
        for ref, kind in zip((cnt_ref,) + tuple(pay_refs), all_kinds):
            for part in _bf16_parts(ref[0], kind):
                lhs_ref[row:row + LANES, :] = part.T.astype(BF16)
                row += LANES

    target = (c * chunk + 1 + lax.broadcasted_iota(jnp.int32, (1, chunk), 1)).astype(F32)
    row_end = cnt_ref[0, :, LANES - 1:LANES]
    before = row_end < target
    rho = jnp.sum(jnp.where(before, 1.0, 0.0), axis=0, keepdims=True).astype(jnp.int32)
    onehot = jnp.where(lax.broadcasted_iota(jnp.int32, (r, chunk), 0) == rho, 1.0, 0.0).astype(BF16)
    fetched = jnp.dot(lhs_ref[...], onehot, preferred_element_type=F32)

    def take(first_part, kind):
        parts = [fetched[(first_part + k) * LANES:(first_part + k + 1) * LANES] for k in range(_NPARTS[kind])]
        return _join_parts(parts, kind)

    lam = jnp.sum(jnp.where(take(0, _INT3) < target, 1.0, 0.0), axis=0, keepdims=True).astype(jnp.int32)
    tok_ref[0] = rho * LANES + lam
    at_lane = lax.broadcasted_iota(jnp.int32, (LANES, chunk), 0) == lam
    first_part = _NPARTS[_INT3]
    for out_ref, kind in zip(out_refs, kinds):
        out_ref[0] = jnp.sum(jnp.where(at_lane, take(first_part, kind), 0.0), axis=0, keepdims=True)
        first_part += _NPARTS[kind]


def _rank_search(counts, nslots, payloads=()):
    g, r, _ = counts.shape
    chunk = min(SEARCH_CHUNK, nslots)
    kinds = tuple(k for _, k in payloads)
    nparts = _NPARTS[_INT3] + sum(_NPARTS[k] for k in kinds)

    def in_spec(a):
        if a.shape[0] == 1:
            return pl.BlockSpec((1, r, LANES), lambda gi, ci: (0, 0, 0))
        return pl.BlockSpec((1, r, LANES), lambda gi, ci: (gi, 0, 0))

    out_spec = pl.BlockSpec((1, 1, chunk), lambda gi, ci: (gi, 0, ci))
    outs = pl.pallas_call(
        functools.partial(_rank_search_kernel, kinds=kinds, chunk=chunk),
        grid=(g, nslots // chunk),
        in_specs=[in_spec(counts)] + [in_spec(a) for a, _ in payloads],
        out_specs=[out_spec] * (1 + len(kinds)),
        out_shape=[jax.ShapeDtypeStruct((g, 1, nslots), jnp.int32)]
        + [jax.ShapeDtypeStruct((g, 1, nslots), F32)] * len(kinds),
        scratch_shapes=[pltpu.VMEM((nparts * LANES, r), BF16)],
        compiler_params=_cparams(("parallel", "arbitrary")),
        name="rank_search",
    )(counts, *[a for a, _ in payloads])
    return outs[0], outs[1:]


def _sc_scatter_rows(rows, idx):
    m, w = rows.shape
    nchunks = m // GATHER_ROWS
    per_worker = nchunks // SC_WORKERS
    assert per_worker * SC_WORKERS * GATHER_ROWS == m and per_worker % 8 == 0
    mesh = plsc.VectorSubcoreMesh(core_axis_name="c", subcore_axis_name="s")

    @functools.partial(
        pl.kernel, mesh=mesh, out_type=jax.ShapeDtypeStruct((m, w), rows.dtype),
        scratch_types=[pltpu.VMEM((per_worker, GATHER_ROWS), jnp.int32),
                       pltpu.VMEM((GATHER_ROWS, w), rows.dtype),
                       pltpu.SemaphoreType.DMA])
    def scatter(rows_hbm, idx_hbm, out_hbm, idx_v, rows_v, sem):
        wid = lax.axis_index("s") * SC_CORES + lax.axis_index("c")
        first = wid * per_worker
        pltpu.sync_copy(idx_hbm.at[pl.ds(first, per_worker)], idx_v)

        @pl.loop(0, per_worker)
        def _(c):
            pltpu.sync_copy(rows_hbm.at[pl.ds((first + c) * GATHER_ROWS, GATHER_ROWS)], rows_v)
            pltpu.async_copy(rows_v, out_hbm.at[idx_v.at[c]], sem).wait()

    return scatter(rows, idx.reshape(nchunks, GATHER_ROWS))


def _moe_kernel(x_ref, gate_ref, wg_ref, wu_ref, wd_ref, o_ref):
    x = _unpack_rows(x_ref[0])
    y = jnp.zeros((x.shape[0], D_MODEL), F32)
    for c in range(D_EXPERT // FC):
        sl = slice(c * FC, (c + 1) * FC)
        a = jnp.dot(x, wg_ref[0, :, sl], preferred_element_type=F32)
        u = jnp.dot(x, wu_ref[0, :, sl], preferred_element_type=F32)
        hid = (a * jax.nn.sigmoid(a) * u).astype(BF16)
        y = y + jnp.dot(hid, wd_ref[0, sl, :], preferred_element_type=F32)
    o_ref[0] = _pack_rows(y * gate_ref[0])


def _moe(xe, gate3, wg, wu, wd):
    e, cap, _ = xe.shape
    tc = min(TC, cap)
    tokb = lambda w: pl.BlockSpec((1, tc, w), lambda ei, i: (ei, i, 0))
    wspec = lambda a: pl.BlockSpec((1,) + a.shape[1:], lambda ei, i: (ei, 0, 0))
    return pl.pallas_call(
        _moe_kernel,
        grid=(e, cap // tc),
        in_specs=[tokb(HALF), tokb(1), wspec(wg), wspec(wu), wspec(wd)],
        out_specs=tokb(HALF),
        out_shape=jax.ShapeDtypeStruct((e, cap, HALF), U32),
        compiler_params=_cparams(("parallel", "arbitrary")),
        name="moe_ffn",
    )(xe, gate3, wg, wu, wd)


_FIRST, _LAST, _ACTIVE = 1, 2, 4


def _combine_kernel(tile_ref, blk_ref, flag_ref, tok_ref, yg_ref, x1_ref, g_ref, o_ref, acc_ref):
    s = pl.program_id(0)
    flag = flag_ref[s]

    @pl.when((flag & _FIRST) != 0)
    def _():
        acc_ref[...] = jnp.zeros_like(acc_ref)

    @pl.when((flag & _ACTIVE) != 0)
    def _():
        tok_row = tile_ref[s] * TS + lax.broadcasted_iota(jnp.int32, (TS, SB), 0)
        onehot = jnp.where(tok_ref[0] == tok_row, 1.0, 0.0).astype(BF16)
        acc_ref[...] += jnp.dot(onehot, _unpack_rows(yg_ref[...]), preferred_element_type=F32)

    @pl.when((flag & _LAST) != 0)
    def _():
        y = x1_ref[...] + acc_ref[...]
        ms = jnp.mean(y * y, axis=-1, keepdims=True)
        o_ref[...] = y * lax.rsqrt(ms + RMS_EPS) * g_ref[...]


def _combine_schedule(bounds, nslots):
    nt, nblk = bounds.shape[0] - 1, nslots // SB
    steps = nt + nblk
    lo, hi = bounds[:-1], bounds[1:]
    b_lo = jnp.minimum(lo // SB, nblk - 1)
    b_hi = jnp.where(hi > lo, (hi - 1) // SB, b_lo)
    nst = b_hi - b_lo + 1
    cum = jnp.cumsum(nst)
    start = cum - nst
    s = jnp.arange(steps, dtype=jnp.int32)
    tile = jnp.minimum(jnp.searchsorted(cum, s, side="right").astype(jnp.int32), nt - 1)
    active = s < cum[-1]
    blk = jnp.where(active, b_lo[tile] + s - start[tile], b_hi[nt - 1])
    first = active & (s == start[tile])
    last = active & (s == cum[tile] - 1)
    flag = first * _FIRST + last * _LAST + active * _ACTIVE
    return tile, blk.astype(jnp.int32), flag.astype(jnp.int32)


def _combine(tile, blk, flag, tok3, yg, x1, g):
    n = x1.shape[0]
    steps = tile.shape[0]
    grid_spec = pltpu.PrefetchScalarGridSpec(
        num_scalar_prefetch=3,
        grid=(steps,),
        in_specs=[pl.BlockSpec((1, 1, SB), lambda s, t, b, f: (b[s], 0, 0)),
                  pl.BlockSpec((SB, HALF), lambda s, t, b, f: (b[s], 0)),
                  pl.BlockSpec((TS, D_MODEL), lambda s, t, b, f: (t[s], 0)),
                  pl.BlockSpec(g.shape, lambda s, t, b, f: (0, 0))],
        out_specs=pl.BlockSpec((TS, D_MODEL), lambda s, t, b, f: (t[s], 0)),
        scratch_shapes=[pltpu.VMEM((TS, D_MODEL), F32)],
    )
    return pl.pallas_call(
        _combine_kernel,
        grid_spec=grid_spec,
        out_shape=jax.ShapeDtypeStruct((n, D_MODEL), F32),
        compiler_params=_cparams(("arbitrary",)),
        name="combine_norm",
    )(tile, blk, flag, tok3, yg, x1, g)


def _group_forward(x, p):
    b, seq, _ = x.shape
    n = b * seq
    cap = CAP_FACTOR * n // N_EXPERTS
    x2 = x.reshape(n, D_MODEL)
    cos_t, sa_t, sb_t = p["rot"]
    qa, ka, va, qb, kb, vb = _inproj(x2, p["g_attn"], p["w_in"], cos_t[:seq], sa_t[:seq], sb_t[:seq], seq)
    r3 = lambda a: a.reshape(b, seq, a.shape[-1])
    oa = _window(r3(qa), r3(ka), r3(va), p["sink"], p["g_out_a"])
    ob = _natten(r3(qb), r3(kb), r3(vb), p["na_tabs"], p["g_out_b"])
    x1, h, aff_t = _outproj(x2, oa.reshape(n, DQA), ob.reshape(n, DB), p["w_out"], p["g_ffn"], p["w_router_t"])

    nslots = N_EXPERTS * cap
    aff3 = aff_t.reshape(N_EXPERTS, n // LANES, LANES)
    pos, c_excl, s_incl, s_excl = _route_select(aff3, cap)
    idx, (gate, s_at, c_at) = _rank_search(pos, cap, [(aff3, _F32), (s_excl, _INT3), (c_excl, _INT1)])
    dest = (s_at + c_at).astype(jnp.int32).reshape(nslots)
    xe = _sc_gather_rows(h, idx.reshape(nslots)).reshape(N_EXPERTS, cap, HALF)
    ye = _moe(xe, gate.reshape(N_EXPERTS, cap, 1), p["w_gate"], p["w_up"], p["w_down"])
    yg = _sc_scatter_rows(ye.reshape(nslots, HALF), dest)

    tok_sorted, _ = _rank_search(s_incl, nslots)
    tile_end = s_incl.reshape(n)[TS - 1::TS].astype(jnp.int32)
    bounds = jnp.concatenate([jnp.zeros((1,), jnp.int32), tile_end])
    tile, blk, flag = _combine_schedule(bounds, nslots)
    y = _combine(tile, blk, flag, tok_sorted.reshape(-1, 1, SB), yg, x1, p["g_final"])
    return y.reshape(b, seq, D_MODEL)


def kernel(x_prompt, x_sample, g_attn, w_in, g_out_a, g_out_b, sink_a, rpb_b, w_out, g_ffn, w_router,
           w_gate, w_up, w_down, g_final):
    assert g_attn.shape[0] == 1, "single trunk layer"
    wr = w_router[0].T
    wr_hi = wr.astype(BF16)
    wr_lo = (wr - wr_hi.astype(F32)).astype(BF16)
    p = {
        "g_attn": g_attn[0][None, :], "w_in": w_in[0].astype(BF16),
        "g_out_a": g_out_a[0][None, :], "g_out_b": g_out_b[0][None, :],
        "sink": sink_a[0], "na_tabs": _natten_tables(rpb_b[0], 4 * NA_ROWS), "w_out": w_out[0].astype(BF16),
        "g_ffn": g_ffn[0][None, :], "w_router_t": jnp.concatenate([wr_hi, wr_lo], axis=0),
        "w_gate": w_gate[0].astype(BF16), "w_up": w_up[0].astype(BF16), "w_down": w_down[0].astype(BF16),
        "g_final": g_final[None, :],
        "rot": _rotary_tables(max(x_prompt.shape[1], x_sample.shape[1])),
    }
    return (_group_forward(x_prompt, p), _group_forward(x_sample, p))
```

```python
import functools

import jax
import jax.numpy as jnp
import numpy as np
from jax import lax
from jax.experimental import pallas as pl
from jax.experimental.pallas import tpu as pltpu
from jax.experimental.pallas import tpu_sc as plsc

D_MODEL = 1024
HEAD_DIM = 64
HQ_A = 8
HKV_A = 2
H_B = 8
GRP = HQ_A // HKV_A
DQA = HQ_A * HEAD_DIM
DKA = HKV_A * HEAD_DIM
DB = H_B * HEAD_DIM
D_IN = DQA + 2 * DKA + 3 * DB
WINDOW = 128
BLK = 128
ROPE_THETA = 500000.0
ROT_DIM = HEAD_DIM // 4
GRID_W = 64
NA_ROWS = 8
NA_COLS = 16
NA_QROWS = 2
NA_KROWS = NA_ROWS + NA_QROWS
NA_KEYS = NA_KROWS * GRID_W
NA_Q = NA_QROWS * GRID_W
N_EXPERTS = 16
CAP_FACTOR = 2
D_EXPERT = 2 * D_MODEL
RMS_EPS = 1e-6
NEG_INF = -1e30
SCALE = HEAD_DIM ** -0.5

LANES = 128
TM = 1024
TSUB = 512
TQ = 1024
TC = 512
FC = 512
TS = 512
SB = 512
VMEM_LIMIT = 56 * 1024 * 1024

BF16 = jnp.bfloat16
F32 = jnp.float32
_NT = (((1,), (1,)), ((), ()))


def _cparams(sem):
    return pltpu.CompilerParams(dimension_semantics=sem, vmem_limit_bytes=VMEM_LIMIT)


HALF = D_MODEL // 2
U32 = jnp.uint32


def _pack_rows(v):
    r = v.astype(BF16).astype(F32)
    hi = pltpu.bitcast(r[:, :HALF], U32)
    lo = pltpu.bitcast(r[:, HALF:], U32)
    return (hi & jnp.uint32(0xFFFF0000)) | (lo >> 16)


def _unpack_rows(w):
    hi = pltpu.bitcast(w & jnp.uint32(0xFFFF0000), F32)
    lo = pltpu.bitcast(w << 16, F32)
    return jnp.concatenate([hi, lo], axis=1).astype(BF16)


SC_CORES = 2
SC_SUBCORES = 16
SC_WORKERS = SC_CORES * SC_SUBCORES
GATHER_ROWS = 64


def _sc_gather_rows(table, idx):
    m, w = idx.shape[0], table.shape[1]
    nchunks = m // GATHER_ROWS
    per_worker = nchunks // SC_WORKERS
    assert per_worker * SC_WORKERS * GATHER_ROWS == m and per_worker % 8 == 0
    mesh = plsc.VectorSubcoreMesh(core_axis_name="c", subcore_axis_name="s")

    @functools.partial(
        pl.kernel, mesh=mesh, out_type=jax.ShapeDtypeStruct((m, w), table.dtype),
        scratch_types=[pltpu.VMEM((per_worker, GATHER_ROWS), jnp.int32),
                       pltpu.VMEM((GATHER_ROWS, w), table.dtype),
                       pltpu.SemaphoreType.DMA])
    def gather(tab_hbm, idx_hbm, out_hbm, idx_v, rows_v, sem):
        wid = lax.axis_index("s") * SC_CORES + lax.axis_index("c")
        first = wid * per_worker
        pltpu.sync_copy(idx_hbm.at[pl.ds(first, per_worker)], idx_v)

        @pl.loop(0, per_worker)
        def _(c):
            pltpu.async_copy(tab_hbm.at[idx_v.at[c]], rows_v, sem).wait()
            pltpu.sync_copy(rows_v, out_hbm.at[pl.ds((first + c) * GATHER_ROWS, GATHER_ROWS)])

    return gather(table, idx.reshape(nchunks, GATHER_ROWS))


def _inproj_kernel(x_ref, g_ref, w_ref, cos_ref, sa_ref, sb_ref,
                   qa_ref, ka_ref, va_ref, qb_ref, kb_ref, vb_ref):
    def normed(rows):
        x = x_ref[rows, :]
        ms = jnp.mean(x * x, axis=-1, keepdims=True)
        return (x * lax.rsqrt(ms + RMS_EPS) * g_ref[...]).astype(BF16)

    def project(rows, hn):
        c, sa, sb = cos_ref[rows, :], sa_ref[rows, :], sb_ref[rows, :]

        def proj(lo, n):
            return jnp.dot(hn, w_ref[:, lo:lo + n], preferred_element_type=F32)

        def rot(blk):
            return blk * c + pltpu.roll(blk, LANES - ROT_DIM // 2, 1) * sa + pltpu.roll(blk, ROT_DIM // 2, 1) * sb

        for j in range(DQA // LANES):
            qa_ref[rows, j * LANES:(j + 1) * LANES] = (rot(proj(j * LANES, LANES)) * SCALE).astype(BF16)
        ka_ref[rows, :] = rot(proj(DQA, DKA)).astype(BF16)
        va_ref[rows, :] = proj(DQA + DKA, DKA).astype(BF16)
        qb_ref[rows, :] = (proj(DQA + 2 * DKA, DB) * SCALE).astype(BF16)
        kb_ref[rows, :] = proj(DQA + 2 * DKA + DB, DB).astype(BF16)
        vb_ref[rows, :] = proj(DQA + 2 * DKA + 2 * DB, DB).astype(BF16)

    subs = [slice(r0, r0 + TSUB) for r0 in range(0, TM, TSUB)]
    hn = normed(subs[0])
    for k, rows in enumerate(subs):
        nxt = normed(subs[k + 1]) if k + 1 < len(subs) else None
        project(rows, hn)
        hn = nxt


def _inproj(x2, g, w_bf, cos_t, sa_t, sb_t, seq):
    n = x2.shape[0]
    per_seq = seq // TM
    tok = lambda w: pl.BlockSpec((TM, w), lambda i: (i, 0))
    full = lambda a: pl.BlockSpec(a.shape, lambda i: (0,) * a.ndim)
    pos = pl.BlockSpec((TM, LANES), lambda i: (i % per_seq, 0))
    widths = (DQA, DKA, DKA, DB, DB, DB)
    return pl.pallas_call(
        _inproj_kernel,
        grid=(n // TM,),
        in_specs=[tok(D_MODEL), full(g), full(w_bf), pos, pos, pos],
        out_specs=[tok(w) for w in widths],
        out_shape=[jax.ShapeDtypeStruct((n, w), BF16) for w in widths],
        compiler_params=_cparams(("parallel",)),
        name="inproj",
    )(x2, g, w_bf, cos_t, sa_t, sb_t)


def _rotary_tables(seq):
    half = ROT_DIM // 2
    inv_freq = jnp.float32(ROPE_THETA) ** (-(jnp.arange(half, dtype=F32) * 2.0) / ROT_DIM)
    ang = jnp.arange(seq, dtype=F32)[:, None] * inv_freq[None, :]
    cos, sin = jnp.cos(ang), jnp.sin(ang)
    ones = jnp.ones((seq, HEAD_DIM - ROT_DIM), F32)
    zeros = jnp.zeros((seq, HEAD_DIM - ROT_DIM), F32)
    zh = jnp.zeros((seq, half), F32)
    rep = LANES // HEAD_DIM
    cos_t = jnp.tile(jnp.concatenate([cos, cos, ones], axis=1), (1, rep))
    sa_t = jnp.tile(jnp.concatenate([-sin, zh, zeros], axis=1), (1, rep))
    sb_t = jnp.tile(jnp.concatenate([zh, sin, zeros], axis=1), (1, rep))
    return cos_t, sa_t, sb_t


SM_ROWS = 32


def _sm_chunks(shape):
    nh, nq, nk = shape
    chunks = [(h, slice(r0, r0 + SM_ROWS)) for h in range(nh) for r0 in range(0, nq, SM_ROWS)]
    return chunks, [slice(l0, l0 + LANES) for l0 in range(0, nk, LANES)]


def _softmax_max(score, shape, pm_ref, mb_ref, floor=None):
    chunks, lanes = _sm_chunks(shape)
    for h, rc in chunks:
        pm_ref[h, rc, :] = functools.reduce(jnp.maximum, [score(h, rc, ls) for ls in lanes])
    m = jnp.max(pm_ref[...], axis=-1, keepdims=True)
    if floor is not None:
        m = jnp.maximum(m, floor)
    mb_ref[...] = jnp.broadcast_to(m, mb_ref.shape)
    return m


def _softmax_exp(score, p_ref, mb_ref, ps_ref):
    chunks, lanes = _sm_chunks(p_ref.shape)
    for h, rc in chunks:
        mb = mb_ref[h, rc, :]
        total = None
        for ls in lanes:
            p = jnp.exp(score(h, rc, ls) - mb)
            p_ref[h, rc, ls] = p.astype(BF16)
            total = p if total is None else total + p
        ps_ref[h, rc, :] = total
    return jnp.sum(ps_ref[...], axis=-1, keepdims=True)


def _window_kernel(q_ref, kp_ref, kc_ref, kn_ref, vp_ref, vc_ref, vn_ref, sink_ref, g_ref, o_ref,
                   k_ref, v_ref, s_ref, p_ref, pm_ref, mb_ref, ps_ref, *, nb):
    i = pl.program_id(1)
    nsub = TQ // BLK
    k_ref[0:BLK] = kp_ref[0]
    k_ref[BLK:BLK + TQ] = kc_ref[0]
    k_ref[BLK + TQ:TQ + 2 * BLK] = kn_ref[0]
    v_ref[0:BLK] = vp_ref[0]
    v_ref[BLK:BLK + TQ] = vc_ref[0]
    v_ref[BLK + TQ:TQ + 2 * BLK] = vn_ref[0]
    qi = lax.broadcasted_iota(jnp.int32, (BLK, 3 * BLK), 0)
    kj = lax.broadcasted_iota(jnp.int32, (BLK, 3 * BLK), 1)
    band = jnp.abs(kj - BLK - qi) <= WINDOW
    sink = sink_ref[...]

    def scores(j):
        n = i * nsub + j
        k_lo = jnp.where(n >= 1, 0, BLK)
        k_hi = jnp.where(n <= nb - 2, 3 * BLK, 2 * BLK)
        bias = jnp.where(band & (kj >= k_lo) & (kj < k_hi), 0.0, NEG_INF)
        for h in range(HKV_A):
            kh = k_ref[j * BLK:(j + 3) * BLK, h * HEAD_DIM:(h + 1) * HEAD_DIM]
            qs = jnp.concatenate([q_ref[0, j * BLK:(j + 1) * BLK, hd * HEAD_DIM:(hd + 1) * HEAD_DIM]
                                  for hd in range(h * GRP, (h + 1) * GRP)], axis=0)
            s = lax.dot_general(qs, kh, _NT, preferred_element_type=F32)
            s_ref[j % 3, h * GRP:(h + 1) * GRP] = s.reshape(GRP, BLK, 3 * BLK) + bias[None]

    def score_fn(j):
        sb = s_ref.at[j % 3]
        return lambda hd, rc, ls: sb[hd, rc, ls]

    def values(j, denom):
        outs = []
        for h in range(HKV_A):
            vh = v_ref[j * BLK:(j + 3) * BLK, h * HEAD_DIM:(h + 1) * HEAD_DIM]
            ph = p_ref[j % 2, h * GRP:(h + 1) * GRP].reshape(GRP * BLK, 3 * BLK)
            o = jnp.dot(ph, vh, preferred_element_type=F32)
            outs += [o[g * BLK:(g + 1) * BLK] / denom[h * GRP + g] for g in range(GRP)]
        ob = jnp.concatenate(outs, axis=1)
        ms = jnp.mean(ob * ob, axis=-1, keepdims=True)
        o_ref[0, j * BLK:(j + 1) * BLK, :] = (ob * lax.rsqrt(ms + RMS_EPS) * g_ref[...]).astype(BF16)

    maxima, denoms = {}, {}
    for t in range(nsub + 3):
        if t < nsub:
            scores(t)
        if 0 <= t - 2 < nsub:
            j = t - 2
            rowsum = _softmax_exp(score_fn(j), p_ref.at[j % 2], mb_ref.at[j % 2], ps_ref.at[j % 2])
            denoms[j] = rowsum + jnp.exp(sink - maxima.pop(j))
        if 0 <= t - 3 < nsub:
            values(t - 3, denoms.pop(t - 3))
        if 0 <= t - 1 < nsub:
            j = t - 1
            maxima[j] = _softmax_max(score_fn(j), s_ref.shape[1:], pm_ref.at[j % 2], mb_ref.at[j % 2], floor=sink)


def _window(qa, ka, va, sink, g):
    b, seq, _ = qa.shape
    nb = seq // BLK
    r = TQ // BLK
    cur = lambda w: pl.BlockSpec((1, TQ, w), lambda bi, i: (bi, i, 0))
    prev = pl.BlockSpec((1, BLK, DKA), lambda bi, i: (bi, jnp.maximum(i * r - 1, 0), 0))
    nxt = pl.BlockSpec((1, BLK, DKA), lambda bi, i: (bi, jnp.minimum((i + 1) * r, nb - 1), 0))
    sink3 = sink.reshape(HQ_A, 1, 1)
    return pl.pallas_call(
        functools.partial(_window_kernel, nb=nb),
        grid=(b, seq // TQ),
        in_specs=[cur(DQA), prev, cur(DKA), nxt, prev, cur(DKA), nxt,
                  pl.BlockSpec(sink3.shape, lambda bi, i: (0, 0, 0)), pl.BlockSpec(g.shape, lambda bi, i: (0, 0))],
        out_specs=cur(DQA),
        out_shape=jax.ShapeDtypeStruct((b, seq, DQA), BF16),
        scratch_shapes=[pltpu.VMEM((TQ + 2 * BLK, DKA), BF16), pltpu.VMEM((TQ + 2 * BLK, DKA), BF16),
                        pltpu.VMEM((3, HQ_A, BLK, 3 * BLK), F32), pltpu.VMEM((2, HQ_A, BLK, 3 * BLK), BF16),
                        ] + [pltpu.VMEM((2, HQ_A, BLK, LANES), F32)] * 3,
        compiler_params=_cparams(("parallel", "parallel")),
        name="window_attn",
    )(qa, ka, ka, ka, va, va, va, sink3, g)


NA_RB = 8
NA_TILE = NA_RB * NA_Q
NA_TILE_ROWS = NA_RB * NA_QROWS
NA_HALO_ROWS = NA_ROWS
NA_HALO = NA_HALO_ROWS * GRID_W


def _natten_kernel(q_ref, kp_ref, kc_ref, kn_ref, vp_ref, vc_ref, vn_ref, tab_ref, g_ref, o_ref,
                   k_ref, v_ref, s_ref, p_ref, pm_ref, mb_ref, ps_ref, *, rows):
    i = pl.program_id(1)
    nrb = rows // NA_QROWS
    for dst, (before, cur, after) in ((k_ref, (kp_ref, kc_ref, kn_ref)), (v_ref, (vp_ref, vc_ref, vn_ref))):
        dst[0:NA_HALO] = before[0]
        dst[NA_HALO:NA_HALO + NA_TILE] = cur[0]
        dst[NA_HALO + NA_TILE:2 * NA_HALO + NA_TILE] = after[0]

    def block(r):
        rb = i * NA_RB + r
        kstart = jnp.clip(NA_QROWS * rb - NA_ROWS // 2, 0, rows - NA_KROWS)
        off = (kstart - (i * NA_TILE_ROWS - NA_HALO_ROWS)) * GRID_W
        variant = jnp.where(rb < 2, rb, jnp.where(rb >= nrb - 2, rb - (nrb - 5), 2))
        return pl.ds(pl.multiple_of(off, GRID_W), NA_KEYS), variant

    def scores(r):
        keys, variant = block(r)
        for h in range(H_B):
            sl = slice(h * HEAD_DIM, (h + 1) * HEAD_DIM)
            s = lax.dot_general(q_ref[0, r * NA_Q:(r + 1) * NA_Q, sl], k_ref[keys, sl], _NT,
                                preferred_element_type=F32)
            s_ref[r % 3, h] = s + tab_ref[variant, h]

    def score_fn(r):
        sb = s_ref.at[r % 3]
        return lambda h, rc, ls: sb[h, rc, ls]

    def values(r, denom):
        keys, _ = block(r)
        outs = []
        for h in range(H_B):
            sl = slice(h * HEAD_DIM, (h + 1) * HEAD_DIM)
            outs.append(jnp.dot(p_ref[r % 2, h], v_ref[keys, sl], preferred_element_type=F32) / denom[h])
        ob = jnp.concatenate(outs, axis=1)
        ms = jnp.mean(ob * ob, axis=-1, keepdims=True)
        o_ref[0, r * NA_Q:(r + 1) * NA_Q, :] = (ob * lax.rsqrt(ms + RMS_EPS) * g_ref[...]).astype(BF16)

    denoms = {}
    for t in range(NA_RB + 3):
        if t < NA_RB:
            scores(t)
        if 0 <= t - 2 < NA_RB:
            r = t - 2
            denoms[r] = _softmax_exp(score_fn(r), p_ref.at[r % 2], mb_ref.at[r % 2], ps_ref.at[r % 2])
        if 0 <= t - 3 < NA_RB:
            values(t - 3, denoms.pop(t - 3))
        if 0 <= t - 1 < NA_RB:
            r = t - 1
            _softmax_max(score_fn(r), s_ref.shape[1:], pm_ref.at[r % 2], mb_ref.at[r % 2])


def _natten_variant_rowblocks(nrb):
    return (0, 1, 2, nrb - 2, nrb - 1)


def _natten_tables(rpb, rows):
    nrb = rows // NA_QROWS
    qc = np.arange(GRID_W)
    kc = np.arange(GRID_W)
    cs = np.clip(qc - NA_COLS // 2, 0, GRID_W - NA_COLS)
    col_ok = (kc[None, :] >= cs[:, None]) & (kc[None, :] < cs[:, None] + NA_COLS)
    cidx = np.clip(kc[None, :] - qc[:, None] + NA_COLS - 1, 0, 2 * NA_COLS - 2)
    tabs = []
    for rb in _natten_variant_rowblocks(nrb):
        r0 = rb * NA_QROWS
        qrows = r0 + np.arange(NA_QROWS)
        rs = np.clip(qrows - NA_ROWS // 2, 0, rows - NA_ROWS)
        kstart = int(np.clip(r0 - NA_ROWS // 2, 0, rows - NA_KROWS))
        krows = kstart + np.arange(NA_KROWS)
        row_ok = (krows[None, :] >= rs[:, None]) & (krows[None, :] < rs[:, None] + NA_ROWS)
        ridx = np.clip(krows[None, :] - qrows[:, None] + NA_ROWS - 1, 0, 2 * NA_ROWS - 2)
        valid = row_ok[:, None, :, None] & col_ok[None, :, None, :]
        r1h = (ridx[:, :, None] == np.arange(2 * NA_ROWS - 1)).astype(np.float32)
        c1h = (cidx[:, :, None] == np.arange(2 * NA_COLS - 1)).astype(np.float32)
        bias = jnp.einsum("rka,hab,qcb->hrqkc", r1h, rpb.astype(F32), c1h, precision=lax.Precision.HIGHEST)
        bias = jnp.where(valid[None], bias, NEG_INF)
        tabs.append(bias.reshape(H_B, NA_Q, NA_KEYS))
    return jnp.stack(tabs)


def _natten(qb, kb, vb, tabs, g):
    b, seq, _ = qb.shape
    rows = seq // GRID_W
    nrb = rows // NA_QROWS
    nt = seq // NA_TILE
    assert rows >= NA_KROWS and nrb >= 5 and nt * NA_TILE == seq
    cur = pl.BlockSpec((1, NA_TILE, DB), lambda bi, i: (bi, i, 0))
    hpt = NA_TILE // NA_HALO
    prev = pl.BlockSpec((1, NA_HALO, DB), lambda bi, i: (bi, jnp.maximum(i * hpt - 1, 0), 0))
    nxt = pl.BlockSpec((1, NA_HALO, DB), lambda bi, i: (bi, jnp.minimum((i + 1) * hpt, nt * hpt - 1), 0))
    return pl.pallas_call(
        functools.partial(_natten_kernel, rows=rows),
        grid=(b, nt),
        in_specs=[cur, prev, cur, nxt, prev, cur, nxt,
                  pl.BlockSpec(tabs.shape, lambda bi, i: (0, 0, 0, 0)),
                  pl.BlockSpec(g.shape, lambda bi, i: (0, 0))],
        out_specs=cur,
        out_shape=jax.ShapeDtypeStruct((b, seq, DB), BF16),
        scratch_shapes=[pltpu.VMEM((NA_TILE + 2 * NA_HALO, DB), BF16), pltpu.VMEM((NA_TILE + 2 * NA_HALO, DB), BF16),
                        pltpu.VMEM((3, H_B, NA_Q, NA_KEYS), F32), pltpu.VMEM((2, H_B, NA_Q, NA_KEYS), BF16),
                        ] + [pltpu.VMEM((2, H_B, NA_Q, LANES), F32)] * 3,
        compiler_params=_cparams(("parallel", "arbitrary")),
        name="natten",
    )(qb, kb, kb, kb, vb, vb, vb, tabs, g)


def _outproj_kernel(x_ref, oa_ref, ob_ref, wo_ref, g_ref, wr_ref, x1_ref, h_ref, aff_ref):
    def residual(rows):
        return (x_ref[rows, :]
                + jnp.dot(oa_ref[rows, :], wo_ref[0:DQA, :], preferred_element_type=F32)
                + jnp.dot(ob_ref[rows, :], wo_ref[DQA:DQA + DB, :], preferred_element_type=F32))

    def route(rows, x1):
        x1_ref[rows, :] = x1
        ms = jnp.mean(x1 * x1, axis=-1, keepdims=True)
        hf = x1 * lax.rsqrt(ms + RMS_EPS) * g_ref[...]
        h_hi = hf.astype(BF16)
        h_lo = (hf - h_hi.astype(F32)).astype(BF16)
        h_ref[rows, :] = _pack_rows(hf)
        l_hi = lax.dot_general(wr_ref[...], h_hi, _NT, preferred_element_type=F32)
        l_lo = lax.dot_general(wr_ref[0:N_EXPERTS, :], h_lo, _NT, preferred_element_type=F32)
        logits = l_hi[0:N_EXPERTS] + l_hi[N_EXPERTS:2 * N_EXPERTS] + l_lo
        m = jnp.max(logits, axis=0, keepdims=True)
        e = jnp.exp(logits - m)
        aff_ref[:, rows] = e / jnp.sum(e, axis=0, keepdims=True)

    subs = [slice(r0, r0 + TSUB) for r0 in range(0, TM, TSUB)]
    x1 = residual(subs[0])
    for k, rows in enumerate(subs):
        nxt = residual(subs[k + 1]) if k + 1 < len(subs) else None
        route(rows, x1)
        x1 = nxt


def _outproj(x2, oa, ob, wo_bf, g, wr_t):
    n = x2.shape[0]
    tok = lambda w: pl.BlockSpec((TM, w), lambda i: (i, 0))
    full = lambda a: pl.BlockSpec(a.shape, lambda i: (0,) * a.ndim)
    return pl.pallas_call(
        _outproj_kernel,
        grid=(n // TM,),
        in_specs=[tok(D_MODEL), tok(DQA), tok(DB), full(wo_bf), full(g), full(wr_t)],
        out_specs=[tok(D_MODEL), tok(HALF), pl.BlockSpec((N_EXPERTS, TM), lambda i: (0, i))],
        out_shape=[jax.ShapeDtypeStruct((n, D_MODEL), F32), jax.ShapeDtypeStruct((n, HALF), U32),
                   jax.ShapeDtypeStruct((N_EXPERTS, n), F32)],
        compiler_params=_cparams(("parallel",)),
        name="outproj_router",
    )(x2, oa, ob, wo_bf, g, wr_t)


def _cumsum_tokens(x, u_ref, l_ref):
    within = jnp.dot(x.astype(BF16), u_ref[...], preferred_element_type=F32)
    rowtot = jnp.broadcast_to(within[:, LANES - 1:LANES], within.shape).astype(BF16)
    return within + jnp.dot(l_ref[...], rowtot, preferred_element_type=F32)


def _route_select_kernel(aff_ref, u_ref, l_ref, pos_ref, cexcl_ref, sincl_ref, sexcl_ref, run_ref, *, cap):
    e = pl.program_id(0)
    bits = pltpu.bitcast(aff_ref[0], jnp.int32)

    def count(mask):
        c = jnp.sum(jnp.where(mask, 1.0, 0.0), axis=0, keepdims=True)
        return jnp.sum(c, axis=1, keepdims=True)

    def step(i, t):
        cand = t | jnp.left_shift(jnp.int32(1), 30 - i)
        return jnp.where(count(bits >= cand) >= cap, cand, t)

    thr = lax.fori_loop(0, 31, step, jnp.zeros((1, 1), jnp.int32))
    gt = bits > thr
    eq = bits == thr
    need = cap - count(gt)
    eqf = jnp.where(eq, 1.0, 0.0)
    tie_rank = _cumsum_tokens(eqf, u_ref, l_ref) - eqf
    sel = jnp.where(gt | (eq & (tie_rank < need)), 1.0, 0.0)
    pos = _cumsum_tokens(sel, u_ref, l_ref)

    @pl.when(e == 0)
    def _():
        run_ref[...] = jnp.zeros_like(run_ref)

    cnt_before = run_ref[0]
    s_incl = run_ref[1] + pos
    pos_ref[0] = pos
    cexcl_ref[0] = cnt_before
    cnt = cnt_before + sel
    run_ref[0] = cnt
    run_ref[1] = s_incl
    sincl_ref[0] = s_incl
    sexcl_ref[0] = s_incl - cnt


def _route_select(aff3, cap):
    e, r, _ = aff3.shape
    u = jnp.asarray(np.triu(np.ones((LANES, LANES), np.float32)), BF16)
    lo = jnp.asarray(np.tril(np.ones((r, r), np.float32), -1), BF16)
    per_e = pl.BlockSpec((1, r, LANES), lambda ei: (ei, 0, 0))
    shared = pl.BlockSpec((1, r, LANES), lambda ei: (0, 0, 0))
    full = lambda a: pl.BlockSpec(a.shape, lambda ei: (0,) * a.ndim)
    return pl.pallas_call(
        functools.partial(_route_select_kernel, cap=cap),
        grid=(e,),
        in_specs=[per_e, full(u), full(lo)],
        out_specs=[per_e, per_e, shared, shared],
        out_shape=[jax.ShapeDtypeStruct((e, r, LANES), F32), jax.ShapeDtypeStruct((e, r, LANES), F32),
                   jax.ShapeDtypeStruct((1, r, LANES), F32), jax.ShapeDtypeStruct((1, r, LANES), F32)],
        scratch_shapes=[pltpu.VMEM((2, r, LANES), F32)],
        compiler_params=_cparams(("arbitrary",)),
        name="route_select",
    )(aff3, u, lo)


SEARCH_CHUNK = 1024
_INT3, _INT1, _F32 = "int3", "int1", "f32"
_NPARTS = {_INT3: 3, _INT1: 1, _F32: 3}


def _bf16_parts(x, kind):
    if kind == _INT1:
        return [x]
    if kind == _INT3:
        d2 = jnp.floor(x * (1.0 / 65536.0))
        r = x - d2 * 65536.0
        d1 = jnp.floor(r * (1.0 / 256.0))
        return [r - d1 * 256.0, d1, d2]
    a1 = x.astype(BF16).astype(F32)
    r1 = x - a1
    a2 = r1.astype(BF16).astype(F32)
    return [a1, a2, r1 - a2]


def _join_parts(parts, kind):
    if kind == _INT1:
        return parts[0]
    if kind == _INT3:
        return parts[0] + 256.0 * parts[1] + 65536.0 * parts[2]
    return (parts[0] + parts[1]) + parts[2]


def _rank_search_kernel(*refs, kinds, chunk):
    npay = len(kinds)
    cnt_ref, pay_refs = refs[0], refs[1:1 + npay]
    tok_ref, out_refs = refs[1 + npay], refs[2 + npay:2 + 2 * npay]
    lhs_ref = refs[-1]
    c = pl.program_id(1)
    r = cnt_ref.shape[1]
    all_kinds = (_INT3,) + tuple(kinds)

    @pl.when(c == 0)
    def _():
        row = 0
        for ref, kind in zip((cnt_ref,) + tuple(pay_refs), all_kinds):
            for part in _bf16_parts(ref[0], kind):
                lhs_ref[row:row + LANES, :] = part.T.astype(BF16)
                row += LANES

    target = (c * chunk + 1 + lax.broadcasted_iota(jnp.int32, (1, chunk), 1)).astype(F32)
    row_end = cnt_ref[0, :, LANES - 1:LANES]
    before = row_end < target
    rho = jnp.sum(jnp.where(before, 1.0, 0.0), axis=0, keepdims=True).astype(jnp.int32)
    onehot = jnp.where(lax.broadcasted_iota(jnp.int32, (r, chunk), 0) == rho, 1.0, 0.0).astype(BF16)
    fetched = jnp.dot(lhs_ref[...], onehot, preferred_element_type=F32)

    def take(first_part, kind):
        parts = [fetched[(first_part + k) * LANES:(first_part + k + 1) * LANES] for k in range(_NPARTS[kind])]
        return _join_parts(parts, kind)

    lam = jnp.sum(jnp.where(take(0, _INT3) < target, 1.0, 0.0), axis=0, keepdims=True).astype(jnp.int32)
    tok_ref[0] = rho * LANES + lam
    at_lane = lax.broadcasted_iota(jnp.int32, (LANES, chunk), 0) == lam
    first_part = _NPARTS[_INT3]
    for out_ref, kind in zip(out_refs, kinds):
        out_ref[0] = jnp.sum(jnp.where(at_lane, take(first_part, kind), 0.0), axis=0, keepdims=True)
        first_part += _NPARTS[kind]


def _rank_search(counts, nslots, payloads=()):
    g, r, _ = counts.shape
    chunk = min(SEARCH_CHUNK, nslots)
    kinds = tuple(k for _, k in payloads)
    nparts = _NPARTS[_INT3] + sum(_NPARTS[k] for k in kinds)

    def in_spec(a):
        if a.shape[0] == 1:
            return pl.BlockSpec((1, r, LANES), lambda gi, ci: (0, 0, 0))
        return pl.BlockSpec((1, r, LANES), lambda gi, ci: (gi, 0, 0))

    out_spec = pl.BlockSpec((1, 1, chunk), lambda gi, ci: (gi, 0, ci))
    outs = pl.pallas_call(
        functools.partial(_rank_search_kernel, kinds=kinds, chunk=chunk),
        grid=(g, nslots // chunk),
        in_specs=[in_spec(counts)] + [in_spec(a) for a, _ in payloads],
        out_specs=[out_spec] * (1 + len(kinds)),
        out_shape=[jax.ShapeDtypeStruct((g, 1, nslots), jnp.int32)]
        + [jax.ShapeDtypeStruct((g, 1, nslots), F32)] * len(kinds),
        scratch_shapes=[pltpu.VMEM((nparts * LANES, r), BF16)],
        compiler_params=_cparams(("parallel", "arbitrary")),
        name="rank_search",
    )(counts, *[a for a, _ in payloads])
    return outs[0], outs[1:]


def _sc_scatter_rows(rows, idx):
    m, w = rows.shape
    nchunks = m // GATHER_ROWS
    per_worker = nchunks // SC_WORKERS
    assert per_worker * SC_WORKERS * GATHER_ROWS == m and per_worker % 8 == 0
    mesh = plsc.VectorSubcoreMesh(core_axis_name="c", subcore_axis_name="s")

    @functools.partial(
        pl.kernel, mesh=mesh, out_type=jax.ShapeDtypeStruct((m, w), rows.dtype),
        scratch_types=[pltpu.VMEM((per_worker, GATHER_ROWS), jnp.int32),
                       pltpu.VMEM((GATHER_ROWS, w), rows.dtype),
                       pltpu.SemaphoreType.DMA])
    def scatter(rows_hbm, idx_hbm, out_hbm, idx_v, rows_v, sem):
        wid = lax.axis_index("s") * SC_CORES + lax.axis_index("c")
        first = wid * per_worker
        pltpu.sync_copy(idx_hbm.at[pl.ds(first, per_worker)], idx_v)

        @pl.loop(0, per_worker)
        def _(c):
            pltpu.sync_copy(rows_hbm.at[pl.ds((first + c) * GATHER_ROWS, GATHER_ROWS)], rows_v)
            pltpu.async_copy(rows_v, out_hbm.at[idx_v.at[c]], sem).wait()

    return scatter(rows, idx.reshape(nchunks, GATHER_ROWS))


def _moe_kernel(x_ref, gate_ref, wg_ref, wu_ref, wd_ref, o_ref):
    x = _unpack_rows(x_ref[0])
    y = jnp.zeros((x.shape[0], D_MODEL), F32)
    for c in range(D_EXPERT // FC):
        sl = slice(c * FC, (c + 1) * FC)
        a = jnp.dot(x, wg_ref[0, :, sl], preferred_element_type=F32)
        u = jnp.dot(x, wu_ref[0, :, sl], preferred_element_type=F32)
        hid = (a * jax.nn.sigmoid(a) * u).astype(BF16)
        y = y + jnp.dot(hid, wd_ref[0, sl, :], preferred_element_type=F32)
    o_ref[0] = _pack_rows(y * gate_ref[0])


def _moe(xe, gate3, wg, wu, wd):
    e, cap, _ = xe.shape
    tc = min(TC, cap)
    tokb = lambda w: pl.BlockSpec((1, tc, w), lambda ei, i: (ei, i, 0))
    wspec = lambda a: pl.BlockSpec((1,) + a.shape[1:], lambda ei, i: (ei, 0, 0))
    return pl.pallas_call(
        _moe_kernel,
        grid=(e, cap // tc),
        in_specs=[tokb(HALF), tokb(1), wspec(wg), wspec(wu), wspec(wd)],
        out_specs=tokb(HALF),
        out_shape=jax.ShapeDtypeStruct((e, cap, HALF), U32),
        compiler_params=_cparams(("parallel", "arbitrary")),
        name="moe_ffn",
    )(xe, gate3, wg, wu, wd)


_FIRST, _LAST, _ACTIVE = 1, 2, 4


def _combine_kernel(tile_ref, blk_ref, flag_ref, tok_ref, yg_ref, x1_ref, g_ref, o_ref, acc_ref):
    s = pl.program_id(0)
    flag = flag_ref[s]

    @pl.when((flag & _FIRST) != 0)
    def _():
        acc_ref[...] = jnp.zeros_like(acc_ref)

    @pl.when((flag & _ACTIVE) != 0)
    def _():
        tok_row = tile_ref[s] * TS + lax.broadcasted_iota(jnp.int32, (TS, SB), 0)
        onehot = jnp.where(tok_ref[0] == tok_row, 1.0, 0.0).astype(BF16)
        acc_ref[...] += jnp.dot(onehot, _unpack_rows(yg_ref[...]), preferred_element_type=F32)

    @pl.when((flag & _LAST) != 0)
    def _():
        y = x1_ref[...] + acc_ref[...]
        ms = jnp.mean(y * y, axis=-1, keepdims=True)
        o_ref[...] = y * lax.rsqrt(ms + RMS_EPS) * g_ref[...]


def _combine_schedule(bounds, nslots):
    nt, nblk = bounds.shape[0] - 1, nslots // SB
    steps = nt + nblk
    lo, hi = bounds[:-1], bounds[1:]
    b_lo = jnp.minimum(lo // SB, nblk - 1)
    b_hi = jnp.where(hi > lo, (hi - 1) // SB, b_lo)
    nst = b_hi - b_lo + 1
    cum = jnp.cumsum(nst)
    start = cum - nst
    s = jnp.arange(steps, dtype=jnp.int32)
    tile = jnp.minimum(jnp.searchsorted(cum, s, side="right").astype(jnp.int32), nt - 1)
    active = s < cum[-1]
    blk = jnp.where(active, b_lo[tile] + s - start[tile], b_hi[nt - 1])
    first = active & (s == start[tile])
    last = active & (s == cum[tile] - 1)
    flag = first * _FIRST + last * _LAST + active * _ACTIVE
    return tile, blk.astype(jnp.int32), flag.astype(jnp.int32)


def _combine(tile, blk, flag, tok3, yg, x1, g):
    n = x1.shape[0]
    steps = tile.shape[0]
    grid_spec = pltpu.PrefetchScalarGridSpec(
        num_scalar_prefetch=3,
        grid=(steps,),
        in_specs=[pl.BlockSpec((1, 1, SB), lambda s, t, b, f: (b[s], 0, 0)),
                  pl.BlockSpec((SB, HALF), lambda s, t, b, f: (b[s], 0)),
                  pl.BlockSpec((TS, D_MODEL), lambda s, t, b, f: (t[s], 0)),
                  pl.BlockSpec(g.shape, lambda s, t, b, f: (0, 0))],
        out_specs=pl.BlockSpec((TS, D_MODEL), lambda s, t, b, f: (t[s], 0)),
        scratch_shapes=[pltpu.VMEM((TS, D_MODEL), F32)],
    )
    return pl.pallas_call(
        _combine_kernel,
        grid_spec=grid_spec,
        out_shape=jax.ShapeDtypeStruct((n, D_MODEL), F32),
        compiler_params=_cparams(("arbitrary",)),
        name="combine_norm",
    )(tile, blk, flag, tok3, yg, x1, g)


def _group_forward(x, p):
    b, seq, _ = x.shape
    n = b * seq
    cap = CAP_FACTOR * n // N_EXPERTS
    x2 = x.reshape(n, D_MODEL)
    cos_t, sa_t, sb_t = p["rot"]
    qa, ka, va, qb, kb, vb = _inproj(x2, p["g_attn"], p["w_in"], cos_t[:seq], sa_t[:seq], sb_t[:seq], seq)
    r3 = lambda a: a.reshape(b, seq, a.shape[-1])
    oa = _window(r3(qa), r3(ka), r3(va), p["sink"], p["g_out_a"])
    ob = _natten(r3(qb), r3(kb), r3(vb), p["na_tabs"], p["g_out_b"])
    x1, h, aff_t = _outproj(x2, oa.reshape(n, DQA), ob.reshape(n, DB), p["w_out"], p["g_ffn"], p["w_router_t"])

    nslots = N_EXPERTS * cap
    aff3 = aff_t.reshape(N_EXPERTS, n // LANES, LANES)
    pos, c_excl, s_incl, s_excl = _route_select(aff3, cap)
    idx, (gate, s_at, c_at) = _rank_search(pos, cap, [(aff3, _F32), (s_excl, _INT3), (c_excl, _INT1)])
    dest = (s_at + c_at).astype(jnp.int32).reshape(nslots)
    xe = _sc_gather_rows(h, idx.reshape(nslots)).reshape(N_EXPERTS, cap, HALF)
    ye = _moe(xe, gate.reshape(N_EXPERTS, cap, 1), p["w_gate"], p["w_up"], p["w_down"])
    yg = _sc_scatter_rows(ye.reshape(nslots, HALF), dest)

    tok_sorted, _ = _rank_search(s_incl, nslots)
    tile_end = s_incl.reshape(n)[TS - 1::TS].astype(jnp.int32)
    bounds = jnp.concatenate([jnp.zeros((1,), jnp.int32), tile_end])
    tile, blk, flag = _combine_schedule(bounds, nslots)
    y = _combine(tile, blk, flag, tok_sorted.reshape(-1, 1, SB), yg, x1, p["g_final"])
    return y.reshape(b, seq, D_MODEL)


def kernel(x_prompt, x_sample, g_attn, w_in, g_out_a, g_out_b, sink_a, rpb_b, w_out, g_ffn, w_router,
           w_gate, w_up, w_down, g_final):
    assert g_attn.shape[0] == 1, "single trunk layer"
    wr = w_router[0].T
    wr_hi = wr.astype(BF16)
    wr_lo = (wr - wr_hi.astype(F32)).astype(BF16)
    p = {
        "g_attn": g_attn[0][None, :], "w_in": w_in[0].astype(BF16),
        "g_out_a": g_out_a[0][None, :], "g_out_b": g_out_b[0][None, :],
        "sink": sink_a[0], "na_tabs": _natten_tables(rpb_b[0], 4 * NA_ROWS), "w_out": w_out[0].astype(BF16),
        "g_ffn": g_ffn[0][None, :], "w_router_t": jnp.concatenate([wr_hi, wr_lo], axis=0),
        "w_gate": w_gate[0].astype(BF16), "w_up": w_up[0].astype(BF16), "w_down": w_down[0].astype(BF16),
        "g_final": g_final[None, :],
        "rot": _rotary_tables(max(x_prompt.shape[1], x_sample.shape[1])),
    }
    return (_group_forward(x_prompt, p), _group_forward(x_sample, p))
```

```python
import functools

import jax
import jax.numpy as jnp
import numpy as np
from jax import lax
from jax.experimental import pallas as pl
from jax.experimental.pallas import tpu as pltpu
from jax.experimental.pallas import tpu_sc as plsc

D_MODEL = 1024
HEAD_DIM = 64
HQ_A = 8
HKV_A = 2
H_B = 8
GRP = HQ_A // HKV_A
DQA = HQ_A * HEAD_DIM
DKA = HKV_A * HEAD_DIM
DB = H_B * HEAD_DIM
D_IN = DQA + 2 * DKA + 3 * DB
WINDOW = 128
BLK = 128
ROPE_THETA = 500000.0
ROT_DIM = HEAD_DIM // 4
GRID_W = 64
NA_ROWS = 8
NA_COLS = 16
NA_QROWS = 2
NA_KROWS = NA_ROWS + NA_QROWS
NA_KEYS = NA_KROWS * GRID_W
NA_Q = NA_QROWS * GRID_W
N_EXPERTS = 16
CAP_FACTOR = 2
D_EXPERT = 2 * D_MODEL
RMS_EPS = 1e-6
NEG_INF = -1e30
SCALE = HEAD_DIM ** -0.5

LANES = 128
TM = 1024
TSUB = 512
TQ = 1024
TC = 512
FC = 512
TS = 512
SB = 1024
VMEM_LIMIT = 56 * 1024 * 1024

BF16 = jnp.bfloat16
F32 = jnp.float32
_NT = (((1,), (1,)), ((), ()))


def _cparams(sem):
    return pltpu.CompilerParams(dimension_semantics=sem, vmem_limit_bytes=VMEM_LIMIT)


HALF = D_MODEL // 2
U32 = jnp.uint32


def _pack_rows(v):
    r = v.astype(BF16).astype(F32)
    hi = pltpu.bitcast(r[:, :HALF], U32)
    lo = pltpu.bitcast(r[:, HALF:], U32)
    return (hi & jnp.uint32(0xFFFF0000)) | (lo >> 16)


def _unpack_rows(w):
    hi = pltpu.bitcast(w & jnp.uint32(0xFFFF0000), F32)
    lo = pltpu.bitcast(w << 16, F32)
    return jnp.concatenate([hi, lo], axis=1).astype(BF16)


SC_CORES = 2
SC_SUBCORES = 16
SC_WORKERS = SC_CORES * SC_SUBCORES
GATHER_ROWS = 64


def _sc_gather_rows(table, idx):
    m, w = idx.shape[0], table.shape[1]
    nchunks = m // GATHER_ROWS
    per_worker = nchunks // SC_WORKERS
    assert per_worker * SC_WORKERS * GATHER_ROWS == m and per_worker % 8 == 0
    mesh = plsc.VectorSubcoreMesh(core_axis_name="c", subcore_axis_name="s")

    @functools.partial(
        pl.kernel, mesh=mesh, out_type=jax.ShapeDtypeStruct((m, w), table.dtype),
        scratch_types=[pltpu.VMEM((per_worker, GATHER_ROWS), jnp.int32),
                       pltpu.VMEM((GATHER_ROWS, w), table.dtype),
                       pltpu.SemaphoreType.DMA])
    def gather(tab_hbm, idx_hbm, out_hbm, idx_v, rows_v, sem):
        wid = lax.axis_index("s") * SC_CORES + lax.axis_index("c")
        first = wid * per_worker
        pltpu.sync_copy(idx_hbm.at[pl.ds(first, per_worker)], idx_v)

        @pl.loop(0, per_worker)
        def _(c):
            pltpu.async_copy(tab_hbm.at[idx_v.at[c]], rows_v, sem).wait()
            pltpu.sync_copy(rows_v, out_hbm.at[pl.ds((first + c) * GATHER_ROWS, GATHER_ROWS)])

    return gather(table, idx.reshape(nchunks, GATHER_ROWS))


def _inproj_kernel(x_ref, g_ref, w_ref, cos_ref, sa_ref, sb_ref,
                   qa_ref, ka_ref, va_ref, qb_ref, kb_ref, vb_ref):
    def normed(rows):
        x = x_ref[rows, :]
        ms = jnp.mean(x * x, axis=-1, keepdims=True)
        return (x * lax.rsqrt(ms + RMS_EPS) * g_ref[...]).astype(BF16)

    def project(rows, hn):
        c, sa, sb = cos_ref[rows, :], sa_ref[rows, :], sb_ref[rows, :]

        def proj(lo, n):
            return jnp.dot(hn, w_ref[:, lo:lo + n], preferred_element_type=F32)

        def rot(blk):
            return blk * c + pltpu.roll(blk, LANES - ROT_DIM // 2, 1) * sa + pltpu.roll(blk, ROT_DIM // 2, 1) * sb

        for j in range(DQA // LANES):
            qa_ref[rows, j * LANES:(j + 1) * LANES] = (rot(proj(j * LANES, LANES)) * SCALE).astype(BF16)
        ka_ref[rows, :] = rot(proj(DQA, DKA)).astype(BF16)
        va_ref[rows, :] = proj(DQA + DKA, DKA).astype(BF16)
        qb_ref[rows, :] = (proj(DQA + 2 * DKA, DB) * SCALE).astype(BF16)
        kb_ref[rows, :] = proj(DQA + 2 * DKA + DB, DB).astype(BF16)
        vb_ref[rows, :] = proj(DQA + 2 * DKA + 2 * DB, DB).astype(BF16)

    subs = [slice(r0, r0 + TSUB) for r0 in range(0, TM, TSUB)]
    hn = normed(subs[0])
    for k, rows in enumerate(subs):
        nxt = normed(subs[k + 1]) if k + 1 < len(subs) else None
        project(rows, hn)
        hn = nxt


def _inproj(x2, g, w_bf, cos_t, sa_t, sb_t, seq):
    n = x2.shape[0]
    per_seq = seq // TM
    tok = lambda w: pl.BlockSpec((TM, w), lambda i: (i, 0))
    full = lambda a: pl.BlockSpec(a.shape, lambda i: (0,) * a.ndim)
    pos = pl.BlockSpec((TM, LANES), lambda i: (i % per_seq, 0))
    widths = (DQA, DKA, DKA, DB, DB, DB)
    return pl.pallas_call(
        _inproj_kernel,
        grid=(n // TM,),
        in_specs=[tok(D_MODEL), full(g), full(w_bf), pos, pos, pos],
        out_specs=[tok(w) for w in widths],
        out_shape=[jax.ShapeDtypeStruct((n, w), BF16) for w in widths],
        compiler_params=_cparams(("parallel",)),
        name="inproj",
    )(x2, g, w_bf, cos_t, sa_t, sb_t)


def _rotary_tables(seq):
    half = ROT_DIM // 2
    inv_freq = jnp.float32(ROPE_THETA) ** (-(jnp.arange(half, dtype=F32) * 2.0) / ROT_DIM)
    ang = jnp.arange(seq, dtype=F32)[:, None] * inv_freq[None, :]
    cos, sin = jnp.cos(ang), jnp.sin(ang)
    ones = jnp.ones((seq, HEAD_DIM - ROT_DIM), F32)
    zeros = jnp.zeros((seq, HEAD_DIM - ROT_DIM), F32)
    zh = jnp.zeros((seq, half), F32)
    rep = LANES // HEAD_DIM
    cos_t = jnp.tile(jnp.concatenate([cos, cos, ones], axis=1), (1, rep))
    sa_t = jnp.tile(jnp.concatenate([-sin, zh, zeros], axis=1), (1, rep))
    sb_t = jnp.tile(jnp.concatenate([zh, sin, zeros], axis=1), (1, rep))
    return cos_t, sa_t, sb_t


SM_ROWS = 32


def _sm_chunks(shape):
    nh, nq, nk = shape
    chunks = [(h, slice(r0, r0 + SM_ROWS)) for h in range(nh) for r0 in range(0, nq, SM_ROWS)]
    return chunks, [slice(l0, l0 + LANES) for l0 in range(0, nk, LANES)]


def _softmax_max(score, shape, pm_ref, mb_ref, floor=None):
    chunks, lanes = _sm_chunks(shape)
    for h, rc in chunks:
        pm_ref[h, rc, :] = functools.reduce(jnp.maximum, [score(h, rc, ls) for ls in lanes])
    m = jnp.max(pm_ref[...], axis=-1, keepdims=True)
    if floor is not None:
        m = jnp.maximum(m, floor)
    mb_ref[...] = jnp.broadcast_to(m, mb_ref.shape)
    return m


def _softmax_exp(score, p_ref, mb_ref, ps_ref):
    chunks, lanes = _sm_chunks(p_ref.shape)
    for h, rc in chunks:
        mb = mb_ref[h, rc, :]
        total = None
        for ls in lanes:
            p = jnp.exp(score(h, rc, ls) - mb)
            p_ref[h, rc, ls] = p.astype(BF16)
            total = p if total is None else total + p
        ps_ref[h, rc, :] = total
    return jnp.sum(ps_ref[...], axis=-1, keepdims=True)


def _window_kernel(q_ref, kp_ref, kc_ref, kn_ref, vp_ref, vc_ref, vn_ref, sink_ref, g_ref, o_ref,
                   k_ref, v_ref, s_ref, p_ref, pm_ref, mb_ref, ps_ref, *, nb):
    i = pl.program_id(1)
    nsub = TQ // BLK
    k_ref[0:BLK] = kp_ref[0]
    k_ref[BLK:BLK + TQ] = kc_ref[0]
    k_ref[BLK + TQ:TQ + 2 * BLK] = kn_ref[0]
    v_ref[0:BLK] = vp_ref[0]
    v_ref[BLK:BLK + TQ] = vc_ref[0]
    v_ref[BLK + TQ:TQ + 2 * BLK] = vn_ref[0]
    qi = lax.broadcasted_iota(jnp.int32, (BLK, 3 * BLK), 0)
    kj = lax.broadcasted_iota(jnp.int32, (BLK, 3 * BLK), 1)
    band = jnp.abs(kj - BLK - qi) <= WINDOW
    sink = sink_ref[...]

    def scores(j):
        n = i * nsub + j
        k_lo = jnp.where(n >= 1, 0, BLK)
        k_hi = jnp.where(n <= nb - 2, 3 * BLK, 2 * BLK)
        bias = jnp.where(band & (kj >= k_lo) & (kj < k_hi), 0.0, NEG_INF)
        for h in range(HKV_A):
            kh = k_ref[j * BLK:(j + 3) * BLK, h * HEAD_DIM:(h + 1) * HEAD_DIM]
            qs = jnp.concatenate([q_ref[0, j * BLK:(j + 1) * BLK, hd * HEAD_DIM:(hd + 1) * HEAD_DIM]
                                  for hd in range(h * GRP, (h + 1) * GRP)], axis=0)
            s = lax.dot_general(qs, kh, _NT, preferred_element_type=F32)
            s_ref[j % 3, h * GRP:(h + 1) * GRP] = s.reshape(GRP, BLK, 3 * BLK) + bias[None]

    def score_fn(j):
        sb = s_ref.at[j % 3]
        return lambda hd, rc, ls: sb[hd, rc, ls]

    def values(j, denom):
        outs = []
        for h in range(HKV_A):
            vh = v_ref[j * BLK:(j + 3) * BLK, h * HEAD_DIM:(h + 1) * HEAD_DIM]
            ph = p_ref[j % 2, h * GRP:(h + 1) * GRP].reshape(GRP * BLK, 3 * BLK)
            o = jnp.dot(ph, vh, preferred_element_type=F32)
            outs += [o[g * BLK:(g + 1) * BLK] / denom[h * GRP + g] for g in range(GRP)]
        ob = jnp.concatenate(outs, axis=1)
        ms = jnp.mean(ob * ob, axis=-1, keepdims=True)
        o_ref[0, j * BLK:(j + 1) * BLK, :] = (ob * lax.rsqrt(ms + RMS_EPS) * g_ref[...]).astype(BF16)

    maxima, denoms = {}, {}
    for t in range(nsub + 3):
        if t < nsub:
            scores(t)
        if 0 <= t - 2 < nsub:
            j = t - 2
            rowsum = _softmax_exp(score_fn(j), p_ref.at[j % 2], mb_ref.at[j % 2], ps_ref.at[j % 2])
            denoms[j] = rowsum + jnp.exp(sink - maxima.pop(j))
        if 0 <= t - 3 < nsub:
            values(t - 3, denoms.pop(t - 3))
        if 0 <= t - 1 < nsub:
            j = t - 1
            maxima[j] = _softmax_max(score_fn(j), s_ref.shape[1:], pm_ref.at[j % 2], mb_ref.at[j % 2], floor=sink)


def _window(qa, ka, va, sink, g):
    b, seq, _ = qa.shape
    nb = seq // BLK
    r = TQ // BLK
    cur = lambda w: pl.BlockSpec((1, TQ, w), lambda bi, i: (bi, i, 0))
    prev = pl.BlockSpec((1, BLK, DKA), lambda bi, i: (bi, jnp.maximum(i * r - 1, 0), 0))
    nxt = pl.BlockSpec((1, BLK, DKA), lambda bi, i: (bi, jnp.minimum((i + 1) * r, nb - 1), 0))
    sink3 = sink.reshape(HQ_A, 1, 1)
    return pl.pallas_call(
        functools.partial(_window_kernel, nb=nb),
        grid=(b, seq // TQ),
        in_specs=[cur(DQA), prev, cur(DKA), nxt, prev, cur(DKA), nxt,
                  pl.BlockSpec(sink3.shape, lambda bi, i: (0, 0, 0)), pl.BlockSpec(g.shape, lambda bi, i: (0, 0))],
        out_specs=cur(DQA),
        out_shape=jax.ShapeDtypeStruct((b, seq, DQA), BF16),
        scratch_shapes=[pltpu.VMEM((TQ + 2 * BLK, DKA), BF16), pltpu.VMEM((TQ + 2 * BLK, DKA), BF16),
                        pltpu.VMEM((3, HQ_A, BLK, 3 * BLK), F32), pltpu.VMEM((2, HQ_A, BLK, 3 * BLK), BF16),
                        ] + [pltpu.VMEM((2, HQ_A, BLK, LANES), F32)] * 3,
        compiler_params=_cparams(("parallel", "parallel")),
        name="window_attn",
    )(qa, ka, ka, ka, va, va, va, sink3, g)


NA_RB = 8
NA_TILE = NA_RB * NA_Q
NA_TILE_ROWS = NA_RB * NA_QROWS
NA_HALO_ROWS = NA_ROWS
NA_HALO = NA_HALO_ROWS * GRID_W


def _natten_kernel(q_ref, kp_ref, kc_ref, kn_ref, vp_ref, vc_ref, vn_ref, tab_ref, g_ref, o_ref,
                   k_ref, v_ref, s_ref, p_ref, pm_ref, mb_ref, ps_ref, *, rows):
    i = pl.program_id(1)
    nrb = rows // NA_QROWS
    for dst, (before, cur, after) in ((k_ref, (kp_ref, kc_ref, kn_ref)), (v_ref, (vp_ref, vc_ref, vn_ref))):
        dst[0:NA_HALO] = before[0]
        dst[NA_HALO:NA_HALO + NA_TILE] = cur[0]
        dst[NA_HALO + NA_TILE:2 * NA_HALO + NA_TILE] = after[0]

    def block(r):
        rb = i * NA_RB + r
        kstart = jnp.clip(NA_QROWS * rb - NA_ROWS // 2, 0, rows - NA_KROWS)
        off = (kstart - (i * NA_TILE_ROWS - NA_HALO_ROWS)) * GRID_W
        variant = jnp.where(rb < 2, rb, jnp.where(rb >= nrb - 2, rb - (nrb - 5), 2))
        return pl.ds(pl.multiple_of(off, GRID_W), NA_KEYS), variant

    def scores(r):
        keys, variant = block(r)
        for h in range(H_B):
            sl = slice(h * HEAD_DIM, (h + 1) * HEAD_DIM)
            s = lax.dot_general(q_ref[0, r * NA_Q:(r + 1) * NA_Q, sl], k_ref[keys, sl], _NT,
                                preferred_element_type=F32)
            s_ref[r % 3, h] = s + tab_ref[variant, h]

    def score_fn(r):
        sb = s_ref.at[r % 3]
        return lambda h, rc, ls: sb[h, rc, ls]

    def values(r, denom):
        keys, _ = block(r)
        outs = []
        for h in range(H_B):
            sl = slice(h * HEAD_DIM, (h + 1) * HEAD_DIM)
            outs.append(jnp.dot(p_ref[r % 2, h], v_ref[keys, sl], preferred_element_type=F32) / denom[h])
        ob = jnp.concatenate(outs, axis=1)
        ms = jnp.mean(ob * ob, axis=-1, keepdims=True)
        o_ref[0, r * NA_Q:(r + 1) * NA_Q, :] = (ob * lax.rsqrt(ms + RMS_EPS) * g_ref[...]).astype(BF16)

    denoms = {}
    for t in range(NA_RB + 3):
        if t < NA_RB:
            scores(t)
        if 0 <= t - 2 < NA_RB:
            r = t - 2
            denoms[r] = _softmax_exp(score_fn(r), p_ref.at[r % 2], mb_ref.at[r % 2], ps_ref.at[r % 2])
        if 0 <= t - 3 < NA_RB:
            values(t - 3, denoms.pop(t - 3))
        if 0 <= t - 1 < NA_RB:
            r = t - 1
            _softmax_max(score_fn(r), s_ref.shape[1:], pm_ref.at[r % 2], mb_ref.at[r % 2])


def _natten_variant_rowblocks(nrb):
    return (0, 1, 2, nrb - 2, nrb - 1)


def _natten_tables(rpb, rows):
    nrb = rows // NA_QROWS
    qc = np.arange(GRID_W)
    kc = np.arange(GRID_W)
    cs = np.clip(qc - NA_COLS // 2, 0, GRID_W - NA_COLS)
    col_ok = (kc[None, :] >= cs[:, None]) & (kc[None, :] < cs[:, None] + NA_COLS)
    cidx = np.clip(kc[None, :] - qc[:, None] + NA_COLS - 1, 0, 2 * NA_COLS - 2)
    tabs = []
    for rb in _natten_variant_rowblocks(nrb):
        r0 = rb * NA_QROWS
        qrows = r0 + np.arange(NA_QROWS)
        rs = np.clip(qrows - NA_ROWS // 2, 0, rows - NA_ROWS)
        kstart = int(np.clip(r0 - NA_ROWS // 2, 0, rows - NA_KROWS))
        krows = kstart + np.arange(NA_KROWS)
        row_ok = (krows[None, :] >= rs[:, None]) & (krows[None, :] < rs[:, None] + NA_ROWS)
        ridx = np.clip(krows[None, :] - qrows[:, None] + NA_ROWS - 1, 0, 2 * NA_ROWS - 2)
        valid = row_ok[:, None, :, None] & col_ok[None, :, None, :]
        r1h = (ridx[:, :, None] == np.arange(2 * NA_ROWS - 1)).astype(np.float32)
        c1h = (cidx[:, :, None] == np.arange(2 * NA_COLS - 1)).astype(np.float32)
        bias = jnp.einsum("rka,hab,qcb->hrqkc", r1h, rpb.astype(F32), c1h, precision=lax.Precision.HIGHEST)
        bias = jnp.where(valid[None], bias, NEG_INF)
        tabs.append(bias.reshape(H_B, NA_Q, NA_KEYS))
    return jnp.stack(tabs)


def _natten(qb, kb, vb, tabs, g):
    b, seq, _ = qb.shape
    rows = seq // GRID_W
    nrb = rows // NA_QROWS
    nt = seq // NA_TILE
    assert rows >= NA_KROWS and nrb >= 5 and nt * NA_TILE == seq
    cur = pl.BlockSpec((1, NA_TILE, DB), lambda bi, i: (bi, i, 0))
    hpt = NA_TILE // NA_HALO
    prev = pl.BlockSpec((1, NA_HALO, DB), lambda bi, i: (bi, jnp.maximum(i * hpt - 1, 0), 0))
    nxt = pl.BlockSpec((1, NA_HALO, DB), lambda bi, i: (bi, jnp.minimum((i + 1) * hpt, nt * hpt - 1), 0))
    return pl.pallas_call(
        functools.partial(_natten_kernel, rows=rows),
        grid=(b, nt),
        in_specs=[cur, prev, cur, nxt, prev, cur, nxt,
                  pl.BlockSpec(tabs.shape, lambda bi, i: (0, 0, 0, 0)),
                  pl.BlockSpec(g.shape, lambda bi, i: (0, 0))],
        out_specs=cur,
        out_shape=jax.ShapeDtypeStruct((b, seq, DB), BF16),
        scratch_shapes=[pltpu.VMEM((NA_TILE + 2 * NA_HALO, DB), BF16), pltpu.VMEM((NA_TILE + 2 * NA_HALO, DB), BF16),
                        pltpu.VMEM((3, H_B, NA_Q, NA_KEYS), F32), pltpu.VMEM((2, H_B, NA_Q, NA_KEYS), BF16),
                        ] + [pltpu.VMEM((2, H_B, NA_Q, LANES), F32)] * 3,
        compiler_params=_cparams(("parallel", "arbitrary")),
        name="natten",
    )(qb, kb, kb, kb, vb, vb, vb, tabs, g)


def _outproj_kernel(x_ref, oa_ref, ob_ref, wo_ref, g_ref, wr_ref, x1_ref, h_ref, aff_ref):
    def residual(rows):
        return (x_ref[rows, :]
                + jnp.dot(oa_ref[rows, :], wo_ref[0:DQA, :], preferred_element_type=F32)
                + jnp.dot(ob_ref[rows, :], wo_ref[DQA:DQA + DB, :], preferred_element_type=F32))

    def route(rows, x1):
        x1_ref[rows, :] = x1
        ms = jnp.mean(x1 * x1, axis=-1, keepdims=True)
        hf = x1 * lax.rsqrt(ms + RMS_EPS) * g_ref[...]
        h_hi = hf.astype(BF16)
        h_lo = (hf - h_hi.astype(F32)).astype(BF16)
        h_ref[rows, :] = _pack_rows(hf)
        l_hi = lax.dot_general(wr_ref[...], h_hi, _NT, preferred_element_type=F32)
        l_lo = lax.dot_general(wr_ref[0:N_EXPERTS, :], h_lo, _NT, preferred_element_type=F32)
        logits = l_hi[0:N_EXPERTS] + l_hi[N_EXPERTS:2 * N_EXPERTS] + l_lo
        m = jnp.max(logits, axis=0, keepdims=True)
        e = jnp.exp(logits - m)
        aff = e / jnp.sum(e, axis=0, keepdims=True)
        for k in range(TSUB // LANES):
            aff_ref[:, rows.start // LANES + k, :] = aff[:, k * LANES:(k + 1) * LANES]

    subs = [slice(r0, r0 + TSUB) for r0 in range(0, TM, TSUB)]
    x1 = residual(subs[0])
    for k, rows in enumerate(subs):
        nxt = residual(subs[k + 1]) if k + 1 < len(subs) else None
        route(rows, x1)
        x1 = nxt


def _outproj(x2, oa, ob, wo_bf, g, wr_t):
    n = x2.shape[0]
    tok = lambda w: pl.BlockSpec((TM, w), lambda i: (i, 0))
    full = lambda a: pl.BlockSpec(a.shape, lambda i: (0,) * a.ndim)
    return pl.pallas_call(
        _outproj_kernel,
        grid=(n // TM,),
        in_specs=[tok(D_MODEL), tok(DQA), tok(DB), full(wo_bf), full(g), full(wr_t)],
        out_specs=[tok(D_MODEL), tok(HALF), pl.BlockSpec((N_EXPERTS, TM // LANES, LANES), lambda i: (0, i, 0))],
        out_shape=[jax.ShapeDtypeStruct((n, D_MODEL), F32), jax.ShapeDtypeStruct((n, HALF), U32),
                   jax.ShapeDtypeStruct((N_EXPERTS, n // LANES, LANES), F32)],
        compiler_params=_cparams(("parallel",)),
        name="outproj_router",
    )(x2, oa, ob, wo_bf, g, wr_t)


def _cumsum_tokens(x, u_ref, l_ref):
    within = jnp.dot(x.astype(BF16), u_ref[...], preferred_element_type=F32)
    rowtot = jnp.broadcast_to(within[:, LANES - 1:LANES], within.shape).astype(BF16)
    return within + jnp.dot(l_ref[...], rowtot, preferred_element_type=F32)


def _route_select_kernel(aff_ref, u_ref, l_ref, pos_ref, cexcl_ref, sincl_ref, sexcl_ref, run_ref, *, cap):
    e = pl.program_id(0)
    bits = pltpu.bitcast(aff_ref[0], jnp.int32)

    def count(mask):
        c = jnp.sum(jnp.where(mask, 1.0, 0.0), axis=0, keepdims=True)
        return jnp.sum(c, axis=1, keepdims=True)

    def step(i, t):
        cand = t | jnp.left_shift(jnp.int32(1), 30 - i)
        return jnp.where(count(bits >= cand) >= cap, cand, t)

    thr = lax.fori_loop(0, 31, step, jnp.zeros((1, 1), jnp.int32))
    gt = bits > thr
    eq = bits == thr
    need = cap - count(gt)
    eqf = jnp.where(eq, 1.0, 0.0)
    tie_rank = _cumsum_tokens(eqf, u_ref, l_ref) - eqf
    sel = jnp.where(gt | (eq & (tie_rank < need)), 1.0, 0.0)
    pos = _cumsum_tokens(sel, u_ref, l_ref)

    @pl.when(e == 0)
    def _():
        run_ref[...] = jnp.zeros_like(run_ref)

    cnt_before = run_ref[0]
    s_incl = run_ref[1] + pos
    pos_ref[0] = pos
    cexcl_ref[0] = cnt_before
    cnt = cnt_before + sel
    run_ref[0] = cnt
    run_ref[1] = s_incl
    sincl_ref[0] = s_incl
    sexcl_ref[0] = s_incl - cnt


def _route_select(aff3, cap):
    e, r, _ = aff3.shape
    u = jnp.asarray(np.triu(np.ones((LANES, LANES), np.float32)), BF16)
    lo = jnp.asarray(np.tril(np.ones((r, r), np.float32), -1), BF16)
    per_e = pl.BlockSpec((1, r, LANES), lambda ei: (ei, 0, 0))
    shared = pl.BlockSpec((1, r, LANES), lambda ei: (0, 0, 0))
    full = lambda a: pl.BlockSpec(a.shape, lambda ei: (0,) * a.ndim)
    return pl.pallas_call(
        functools.partial(_route_select_kernel, cap=cap),
        grid=(e,),
        in_specs=[per_e, full(u), full(lo)],
        out_specs=[per_e, per_e, shared, shared],
        out_shape=[jax.ShapeDtypeStruct((e, r, LANES), F32), jax.ShapeDtypeStruct((e, r, LANES), F32),
                   jax.ShapeDtypeStruct((1, r, LANES), F32), jax.ShapeDtypeStruct((1, r, LANES), F32)],
        scratch_shapes=[pltpu.VMEM((2, r, LANES), F32)],
        compiler_params=_cparams(("arbitrary",)),
        name="route_select",
    )(aff3, u, lo)


SEARCH_CHUNK = 1024
_INT3, _INT2, _INT1, _F32 = "int3", "int2", "int1", "f32"
_NPARTS = {_INT3: 3, _INT2: 2, _INT1: 1, _F32: 3}


def _bf16_parts(x, kind):
    if kind == _INT1:
        return [x]
    if kind == _INT2:
        d1 = jnp.floor(x * (1.0 / 256.0))
        return [x - d1 * 256.0, d1]
    if kind == _INT3:
        d2 = jnp.floor(x * (1.0 / 65536.0))
        r = x - d2 * 65536.0
        d1 = jnp.floor(r * (1.0 / 256.0))
        return [r - d1 * 256.0, d1, d2]
    a1 = x.astype(BF16).astype(F32)
    r1 = x - a1
    a2 = r1.astype(BF16).astype(F32)
    return [a1, a2, r1 - a2]


def _join_parts(parts, kind):
    if kind == _INT1:
        return parts[0]
    if kind == _INT2:
        return parts[0] + 256.0 * parts[1]
    if kind == _INT3:
        return parts[0] + 256.0 * parts[1] + 65536.0 * parts[2]
    return (parts[0] + parts[1]) + parts[2]


def _rank_search_kernel(*refs, kinds, count_kind, chunk):
    npay = len(kinds)
    cnt_ref, pay_refs = refs[0], refs[1:1 + npay]
    tok_ref, out_refs = refs[1 + npay], refs[2 + npay:2 + 2 * npay]
    lhs_ref = refs[-1]
    c = pl.program_id(1)
    r = cnt_ref.shape[1]

    @pl.when(c == 0)
    def _():
        counts = cnt_ref[0]
        ends = jnp.broadcast_to(counts[:, LANES - 1:LANES], counts.shape)
        first_row = lax.broadcasted_iota(jnp.int32, counts.shape, 0) == 0
        row_start = jnp.where(first_row, 0.0, pltpu.roll(ends, 1, 0))
        row = 0
        for val, kind in zip([counts - row_start] + [ref[0] for ref in pay_refs], (count_kind,) + tuple(kinds)):
            for part in _bf16_parts(val, kind):
                lhs_ref[row:row + LANES, :] = part.T.astype(BF16)
                row += LANES

    target = (c * chunk + 1 + lax.broadcasted_iota(jnp.int32, (1, chunk), 1)).astype(F32)
    row_end = cnt_ref[0, :, LANES - 1:LANES]
    before = row_end < target
    rho = jnp.sum(jnp.where(before, 1.0, 0.0), axis=0, keepdims=True).astype(jnp.int32)
    base = jnp.max(jnp.where(before, row_end, 0.0), axis=0, keepdims=True)
    onehot = jnp.where(lax.broadcasted_iota(jnp.int32, (r, chunk), 0) == rho, 1.0, 0.0).astype(BF16)
    fetched = jnp.dot(lhs_ref[...], onehot, preferred_element_type=F32)

    def take(first_part, kind):
        parts = [fetched[(first_part + k) * LANES:(first_part + k + 1) * LANES] for k in range(_NPARTS[kind])]
        return _join_parts(parts, kind)

    lam = jnp.sum(jnp.where(take(0, count_kind) < target - base, 1.0, 0.0), axis=0, keepdims=True).astype(jnp.int32)
    tok_ref[0] = rho * LANES + lam
    at_lane = lax.broadcasted_iota(jnp.int32, (LANES, chunk), 0) == lam
    first_part = _NPARTS[count_kind]
    for out_ref, kind in zip(out_refs, kinds):
        out_ref[0] = jnp.sum(jnp.where(at_lane, take(first_part, kind), 0.0), axis=0, keepdims=True)
        first_part += _NPARTS[kind]


def _rank_search(counts, per_token, nslots, payloads=()):
    g, r, _ = counts.shape
    chunk = min(SEARCH_CHUNK, nslots)
    kinds = tuple(k for _, k in payloads)
    count_kind = _INT1 if per_token * LANES < 256 else _INT2
    assert per_token * LANES < 65536
    nparts = _NPARTS[count_kind] + sum(_NPARTS[k] for k in kinds)

    def in_spec(a):
        if a.shape[0] == 1:
            return pl.BlockSpec((1, r, LANES), lambda gi, ci: (0, 0, 0))
        return pl.BlockSpec((1, r, LANES), lambda gi, ci: (gi, 0, 0))

    out_spec = pl.BlockSpec((1, 1, chunk), lambda gi, ci: (gi, 0, ci))
    outs = pl.pallas_call(
        functools.partial(_rank_search_kernel, kinds=kinds, count_kind=count_kind, chunk=chunk),
        grid=(g, nslots // chunk),
        in_specs=[in_spec(counts)] + [in_spec(a) for a, _ in payloads],
        out_specs=[out_spec] * (1 + len(kinds)),
        out_shape=[jax.ShapeDtypeStruct((g, 1, nslots), jnp.int32)]
        + [jax.ShapeDtypeStruct((g, 1, nslots), F32)] * len(kinds),
        scratch_shapes=[pltpu.VMEM((nparts * LANES, r), BF16)],
        compiler_params=_cparams(("parallel", "arbitrary")),
        name="rank_search",
    )(counts, *[a for a, _ in payloads])
    return outs[0], outs[1:]


def _sc_scatter_rows(rows, idx):
    m, w = rows.shape
    nchunks = m // GATHER_ROWS
    per_worker = nchunks // SC_WORKERS
    assert per_worker * SC_WORKERS * GATHER_ROWS == m and per_worker % 8 == 0
    mesh = plsc.VectorSubcoreMesh(core_axis_name="c", subcore_axis_name="s")

    @functools.partial(
        pl.kernel, mesh=mesh, out_type=jax.ShapeDtypeStruct((m, w), rows.dtype),
        scratch_types=[pltpu.VMEM((per_worker, GATHER_ROWS), jnp.int32),
                       pltpu.VMEM((GATHER_ROWS, w), rows.dtype),
                       pltpu.SemaphoreType.DMA])
    def scatter(rows_hbm, idx_hbm, out_hbm, idx_v, rows_v, sem):
        wid = lax.axis_index("s") * SC_CORES + lax.axis_index("c")
        first = wid * per_worker
        pltpu.sync_copy(idx_hbm.at[pl.ds(first, per_worker)], idx_v)

        @pl.loop(0, per_worker)
        def _(c):
            pltpu.sync_copy(rows_hbm.at[pl.ds((first + c) * GATHER_ROWS, GATHER_ROWS)], rows_v)
            pltpu.async_copy(rows_v, out_hbm.at[idx_v.at[c]], sem).wait()

    return scatter(rows, idx.reshape(nchunks, GATHER_ROWS))


def _moe_kernel(x_ref, gate_ref, wg_ref, wu_ref, wd_ref, o_ref):
    x = _unpack_rows(x_ref[0])
    y = jnp.zeros((x.shape[0], D_MODEL), F32)
    for c in range(D_EXPERT // FC):
        sl = slice(c * FC, (c + 1) * FC)
        a = jnp.dot(x, wg_ref[0, :, sl], preferred_element_type=F32)
        u = jnp.dot(x, wu_ref[0, :, sl], preferred_element_type=F32)
        hid = (a * jax.nn.sigmoid(a) * u).astype(BF16)
        y = y + jnp.dot(hid, wd_ref[0, sl, :], preferred_element_type=F32)
    o_ref[0] = _pack_rows(y * gate_ref[0])


def _moe(xe, gate3, wg, wu, wd):
    e, cap, _ = xe.shape
    tc = min(TC, cap)
    tokb = lambda w: pl.BlockSpec((1, tc, w), lambda ei, i: (ei, i, 0))
    wspec = lambda a: pl.BlockSpec((1,) + a.shape[1:], lambda ei, i: (ei, 0, 0))
    return pl.pallas_call(
        _moe_kernel,
        grid=(e, cap // tc),
        in_specs=[tokb(HALF), tokb(1), wspec(wg), wspec(wu), wspec(wd)],
        out_specs=tokb(HALF),
        out_shape=jax.ShapeDtypeStruct((e, cap, HALF), U32),
        compiler_params=_cparams(("parallel", "arbitrary")),
        name="moe_ffn",
    )(xe, gate3, wg, wu, wd)


_FIRST, _LAST, _ACTIVE = 1, 2, 4


def _combine_kernel(tile_ref, blk_ref, flag_ref, tok_ref, yg_ref, x1_ref, g_ref, o_ref, acc_ref):
    s = pl.program_id(0)
    flag = flag_ref[s]

    @pl.when((flag & _FIRST) != 0)
    def _():
        acc_ref[...] = jnp.zeros_like(acc_ref)

    @pl.when((flag & _ACTIVE) != 0)
    def _():
        tok_row = tile_ref[s] * TS + lax.broadcasted_iota(jnp.int32, (TS, SB), 0)
        onehot = jnp.where(tok_ref[0] == tok_row, 1.0, 0.0).astype(BF16)
        acc_ref[...] += jnp.dot(onehot, _unpack_rows(yg_ref[...]), preferred_element_type=F32)

    @pl.when((flag & _LAST) != 0)
    def _():
        y = x1_ref[...] + acc_ref[...]
        ms = jnp.mean(y * y, axis=-1, keepdims=True)
        o_ref[...] = y * lax.rsqrt(ms + RMS_EPS) * g_ref[...]


def _combine_schedule(bounds, nslots):
    nt, nblk = bounds.shape[0] - 1, nslots // SB
    steps = nt + nblk
    lo, hi = bounds[:-1], bounds[1:]
    b_lo = jnp.minimum(lo // SB, nblk - 1)
    b_hi = jnp.where(hi > lo, (hi - 1) // SB, b_lo)
    nst = b_hi - b_lo + 1
    cum = jnp.cumsum(nst)
    start = cum - nst
    s = jnp.arange(steps, dtype=jnp.int32)
    tile = jnp.minimum(jnp.searchsorted(cum, s, side="right").astype(jnp.int32), nt - 1)
    active = s < cum[-1]
    blk = jnp.where(active, b_lo[tile] + s - start[tile], b_hi[nt - 1])
    first = active & (s == start[tile])
    last = active & (s == cum[tile] - 1)
    flag = first * _FIRST + last * _LAST + active * _ACTIVE
    return tile, blk.astype(jnp.int32), flag.astype(jnp.int32)


def _combine(tile, blk, flag, tok3, yg, x1, g):
    n = x1.shape[0]
    steps = tile.shape[0]
    grid_spec = pltpu.PrefetchScalarGridSpec(
        num_scalar_prefetch=3,
        grid=(steps,),
        in_specs=[pl.BlockSpec((1, 1, SB), lambda s, t, b, f: (b[s], 0, 0)),
                  pl.BlockSpec((SB, HALF), lambda s, t, b, f: (b[s], 0)),
                  pl.BlockSpec((TS, D_MODEL), lambda s, t, b, f: (t[s], 0)),
                  pl.BlockSpec(g.shape, lambda s, t, b, f: (0, 0))],
        out_specs=pl.BlockSpec((TS, D_MODEL), lambda s, t, b, f: (t[s], 0)),
        scratch_shapes=[pltpu.VMEM((TS, D_MODEL), F32)],
    )
    return pl.pallas_call(
        _combine_kernel,
        grid_spec=grid_spec,
        out_shape=jax.ShapeDtypeStruct((n, D_MODEL), F32),
        compiler_params=_cparams(("arbitrary",)),
        name="combine_norm",
    )(tile, blk, flag, tok3, yg, x1, g)


def _group_forward(x, p):
    b, seq, _ = x.shape
    n = b * seq
    cap = CAP_FACTOR * n // N_EXPERTS
    x2 = x.reshape(n, D_MODEL)
    cos_t, sa_t, sb_t = p["rot"]
    qa, ka, va, qb, kb, vb = _inproj(x2, p["g_attn"], p["w_in"], cos_t[:seq], sa_t[:seq], sb_t[:seq], seq)
    r3 = lambda a: a.reshape(b, seq, a.shape[-1])
    oa = _window(r3(qa), r3(ka), r3(va), p["sink"], p["g_out_a"])
    ob = _natten(r3(qb), r3(kb), r3(vb), p["na_tabs"], p["g_out_b"])
    x1, h, aff3 = _outproj(x2, oa.reshape(n, DQA), ob.reshape(n, DB), p["w_out"], p["g_ffn"], p["w_router_t"])

    nslots = N_EXPERTS * cap
    pos, c_excl, s_incl, s_excl = _route_select(aff3, cap)
    idx, (gate, s_at, c_at) = _rank_search(pos, 1, cap, [(aff3, _F32), (s_excl, _INT3), (c_excl, _INT1)])
    dest = (s_at + c_at).astype(jnp.int32).reshape(nslots)
    xe = _sc_gather_rows(h, idx.reshape(nslots)).reshape(N_EXPERTS, cap, HALF)
    ye = _moe(xe, gate.reshape(N_EXPERTS, cap, 1), p["w_gate"], p["w_up"], p["w_down"])
    yg = _sc_scatter_rows(ye.reshape(nslots, HALF), dest)

    tok_sorted, _ = _rank_search(s_incl, N_EXPERTS, nslots)
    tile_end = s_incl.reshape(n)[TS - 1::TS].astype(jnp.int32)
    bounds = jnp.concatenate([jnp.zeros((1,), jnp.int32), tile_end])
    tile, blk, flag = _combine_schedule(bounds, nslots)
    y = _combine(tile, blk, flag, tok_sorted.reshape(-1, 1, SB), yg, x1, p["g_final"])
    return y.reshape(b, seq, D_MODEL)


def kernel(x_prompt, x_sample, g_attn, w_in, g_out_a, g_out_b, sink_a, rpb_b, w_out, g_ffn, w_router,
           w_gate, w_up, w_down, g_final):
    assert g_attn.shape[0] == 1, "single trunk layer"
    wr = w_router[0].T
    wr_hi = wr.astype(BF16)
    wr_lo = (wr - wr_hi.astype(F32)).astype(BF16)
    p = {
        "g_attn": g_attn[0][None, :], "w_in": w_in[0].astype(BF16),
        "g_out_a": g_out_a[0][None, :], "g_out_b": g_out_b[0][None, :],
        "sink": sink_a[0], "na_tabs": _natten_tables(rpb_b[0], 4 * NA_ROWS), "w_out": w_out[0].astype(BF16),
        "g_ffn": g_ffn[0][None, :], "w_router_t": jnp.concatenate([wr_hi, wr_lo], axis=0),
        "w_gate": w_gate[0].astype(BF16), "w_up": w_up[0].astype(BF16), "w_down": w_down[0].astype(BF16),
        "g_final": g_final[None, :],
        "rot": _rotary_tables(max(x_prompt.shape[1], x_sample.shape[1])),
    }
    return (_group_forward(x_prompt, p), _group_forward(x_sample, p))
```

```python
import functools

import jax
import jax.numpy as jnp
import numpy as np
from jax import lax
from jax.experimental import pallas as pl
from jax.experimental.pallas import tpu as pltpu
from jax.experimental.pallas import tpu_sc as plsc

D_MODEL = 1024
HEAD_DIM = 64
HQ_A = 8
HKV_A = 2
H_B = 8
GRP = HQ_A // HKV_A
DQA = HQ_A * HEAD_DIM
DKA = HKV_A * HEAD_DIM
DB = H_B * HEAD_DIM
D_IN = DQA + 2 * DKA + 3 * DB
WINDOW = 128
BLK = 128
ROPE_THETA = 500000.0
ROT_DIM = HEAD_DIM // 4
GRID_W = 64
NA_ROWS = 8
NA_COLS = 16
NA_QROWS = 2
NA_KROWS = NA_ROWS + NA_QROWS
NA_KEYS = NA_KROWS * GRID_W
NA_Q = NA_QROWS * GRID_W
N_EXPERTS = 16
CAP_FACTOR = 2
D_EXPERT = 2 * D_MODEL
RMS_EPS = 1e-6
NEG_INF = -1e30
SCALE = HEAD_DIM ** -0.5

LANES = 128
TM = 1024
TSUB = 512
TQ = 1024
TC = 1024
FC = 512
TS = 512
SB = 1024
VMEM_LIMIT = 56 * 1024 * 1024

BF16 = jnp.bfloat16
F32 = jnp.float32
_NT = (((1,), (1,)), ((), ()))


def _cparams(sem):
    return pltpu.CompilerParams(dimension_semantics=sem, vmem_limit_bytes=VMEM_LIMIT)


HALF = D_MODEL // 2
U32 = jnp.uint32


def _pack_rows(v):
    r = v.astype(BF16).astype(F32)
    hi = pltpu.bitcast(r[:, :HALF], U32)
    lo = pltpu.bitcast(r[:, HALF:], U32)
    return (hi & jnp.uint32(0xFFFF0000)) | (lo >> 16)


def _unpack_rows(w):
    hi = pltpu.bitcast(w & jnp.uint32(0xFFFF0000), F32)
    lo = pltpu.bitcast(w << 16, F32)
    return jnp.concatenate([hi, lo], axis=1).astype(BF16)


SC_CORES = 2
SC_SUBCORES = 16
SC_WORKERS = SC_CORES * SC_SUBCORES
GATHER_ROWS = 64


def _sc_gather_rows(table, idx):
    m, w = idx.shape[0], table.shape[1]
    nchunks = m // GATHER_ROWS
    per_worker = nchunks // SC_WORKERS
    assert per_worker * SC_WORKERS * GATHER_ROWS == m and per_worker % 8 == 0
    mesh = plsc.VectorSubcoreMesh(core_axis_name="c", subcore_axis_name="s")

    @functools.partial(
        pl.kernel, mesh=mesh, out_type=jax.ShapeDtypeStruct((m, w), table.dtype),
        scratch_types=[pltpu.VMEM((per_worker, GATHER_ROWS), jnp.int32),
                       pltpu.VMEM((GATHER_ROWS, w), table.dtype),
                       pltpu.SemaphoreType.DMA])
    def gather(tab_hbm, idx_hbm, out_hbm, idx_v, rows_v, sem):
        wid = lax.axis_index("s") * SC_CORES + lax.axis_index("c")
        first = wid * per_worker
        pltpu.sync_copy(idx_hbm.at[pl.ds(first, per_worker)], idx_v)

        @pl.loop(0, per_worker)
        def _(c):
            pltpu.async_copy(tab_hbm.at[idx_v.at[c]], rows_v, sem).wait()
            pltpu.sync_copy(rows_v, out_hbm.at[pl.ds((first + c) * GATHER_ROWS, GATHER_ROWS)])

    return gather(table, idx.reshape(nchunks, GATHER_ROWS))


def _inproj_kernel(x_ref, g_ref, w_ref, cos_ref, sa_ref, sb_ref,
                   qa_ref, ka_ref, va_ref, qb_ref, kb_ref, vb_ref):
    def normed(rows):
        x = x_ref[rows, :]
        ms = jnp.mean(x * x, axis=-1, keepdims=True)
        return (x * lax.rsqrt(ms + RMS_EPS) * g_ref[...]).astype(BF16)

    def project(rows, hn):
        c, sa, sb = cos_ref[rows, :], sa_ref[rows, :], sb_ref[rows, :]

        def proj(lo, n):
            return jnp.dot(hn, w_ref[:, lo:lo + n], preferred_element_type=F32)

        def rot(blk):
            return blk * c + pltpu.roll(blk, LANES - ROT_DIM // 2, 1) * sa + pltpu.roll(blk, ROT_DIM // 2, 1) * sb

        for j in range(DQA // LANES):
            qa_ref[rows, j * LANES:(j + 1) * LANES] = (rot(proj(j * LANES, LANES)) * SCALE).astype(BF16)
        ka_ref[rows, :] = rot(proj(DQA, DKA)).astype(BF16)
        va_ref[rows, :] = proj(DQA + DKA, DKA).astype(BF16)
        qb_ref[rows, :] = (proj(DQA + 2 * DKA, DB) * SCALE).astype(BF16)
        kb_ref[rows, :] = proj(DQA + 2 * DKA + DB, DB).astype(BF16)
        vb_ref[rows, :] = proj(DQA + 2 * DKA + 2 * DB, DB).astype(BF16)

    subs = [slice(r0, r0 + TSUB) for r0 in range(0, TM, TSUB)]
    hn = normed(subs[0])
    for k, rows in enumerate(subs):
        nxt = normed(subs[k + 1]) if k + 1 < len(subs) else None
        project(rows, hn)
        hn = nxt


def _inproj(x2, g, w_bf, cos_t, sa_t, sb_t, seq):
    n = x2.shape[0]
    per_seq = seq // TM
    tok = lambda w: pl.BlockSpec((TM, w), lambda i: (i, 0))
    full = lambda a: pl.BlockSpec(a.shape, lambda i: (0,) * a.ndim)
    pos = pl.BlockSpec((TM, LANES), lambda i: (i % per_seq, 0))
    widths = (DQA, DKA, DKA, DB, DB, DB)
    return pl.pallas_call(
        _inproj_kernel,
        grid=(n // TM,),
        in_specs=[tok(D_MODEL), full(g), full(w_bf), pos, pos, pos],
        out_specs=[tok(w) for w in widths],
        out_shape=[jax.ShapeDtypeStruct((n, w), BF16) for w in widths],
        compiler_params=_cparams(("parallel",)),
        name="inproj",
    )(x2, g, w_bf, cos_t, sa_t, sb_t)


def _rotary_tables(seq):
    half = ROT_DIM // 2
    inv_freq = jnp.float32(ROPE_THETA) ** (-(jnp.arange(half, dtype=F32) * 2.0) / ROT_DIM)
    ang = jnp.arange(seq, dtype=F32)[:, None] * inv_freq[None, :]
    cos, sin = jnp.cos(ang), jnp.sin(ang)
    ones = jnp.ones((seq, HEAD_DIM - ROT_DIM), F32)
    zeros = jnp.zeros((seq, HEAD_DIM - ROT_DIM), F32)
    zh = jnp.zeros((seq, half), F32)
    rep = LANES // HEAD_DIM
    cos_t = jnp.tile(jnp.concatenate([cos, cos, ones], axis=1), (1, rep))
    sa_t = jnp.tile(jnp.concatenate([-sin, zh, zeros], axis=1), (1, rep))
    sb_t = jnp.tile(jnp.concatenate([zh, sin, zeros], axis=1), (1, rep))
    return cos_t, sa_t, sb_t


SM_ROWS = 32


def _sm_chunks(shape):
    nh, nq, nk = shape
    chunks = [(h, slice(r0, r0 + SM_ROWS)) for h in range(nh) for r0 in range(0, nq, SM_ROWS)]
    return chunks, [slice(l0, l0 + LANES) for l0 in range(0, nk, LANES)]


def _softmax_max(score, shape, pm_ref, mb_ref, floor=None):
    chunks, lanes = _sm_chunks(shape)
    for h, rc in chunks:
        pm_ref[h, rc, :] = functools.reduce(jnp.maximum, [score(h, rc, ls) for ls in lanes])
    m = jnp.max(pm_ref[...], axis=-1, keepdims=True)
    if floor is not None:
        m = jnp.maximum(m, floor)
    mb_ref[...] = jnp.broadcast_to(m, mb_ref.shape)
    return m


def _softmax_exp(score, p_ref, mb_ref, ps_ref):
    chunks, lanes = _sm_chunks(p_ref.shape)
    for h, rc in chunks:
        mb = mb_ref[h, rc, :]
        total = None
        for ls in lanes:
            p = jnp.exp(score(h, rc, ls) - mb)
            p_ref[h, rc, ls] = p.astype(BF16)
            total = p if total is None else total + p
        ps_ref[h, rc, :] = total
    return jnp.sum(ps_ref[...], axis=-1, keepdims=True)


def _window_kernel(q_ref, kp_ref, kc_ref, kn_ref, vp_ref, vc_ref, vn_ref, sink_ref, g_ref, o_ref,
                   k_ref, v_ref, s_ref, p_ref, pm_ref, mb_ref, ps_ref, *, nb):
    i = pl.program_id(1)
    nsub = TQ // BLK
    k_ref[0:BLK] = kp_ref[0]
    k_ref[BLK:BLK + TQ] = kc_ref[0]
    k_ref[BLK + TQ:TQ + 2 * BLK] = kn_ref[0]
    v_ref[0:BLK] = vp_ref[0]
    v_ref[BLK:BLK + TQ] = vc_ref[0]
    v_ref[BLK + TQ:TQ + 2 * BLK] = vn_ref[0]
    qi = lax.broadcasted_iota(jnp.int32, (BLK, 3 * BLK), 0)
    kj = lax.broadcasted_iota(jnp.int32, (BLK, 3 * BLK), 1)
    band = jnp.abs(kj - BLK - qi) <= WINDOW
    sink = sink_ref[...]

    def scores(j):
        n = i * nsub + j
        k_lo = jnp.where(n >= 1, 0, BLK)
        k_hi = jnp.where(n <= nb - 2, 3 * BLK, 2 * BLK)
        bias = jnp.where(band & (kj >= k_lo) & (kj < k_hi), 0.0, NEG_INF)
        for h in range(HKV_A):
            kh = k_ref[j * BLK:(j + 3) * BLK, h * HEAD_DIM:(h + 1) * HEAD_DIM]
            qs = jnp.concatenate([q_ref[0, j * BLK:(j + 1) * BLK, hd * HEAD_DIM:(hd + 1) * HEAD_DIM]
                                  for hd in range(h * GRP, (h + 1) * GRP)], axis=0)
            s = lax.dot_general(qs, kh, _NT, preferred_element_type=F32)
            s_ref[j % 3, h * GRP:(h + 1) * GRP] = s.reshape(GRP, BLK, 3 * BLK) + bias[None]

    def score_fn(j):
        sb = s_ref.at[j % 3]
        return lambda hd, rc, ls: sb[hd, rc, ls]

    def values(j, denom):
        outs = []
        for h in range(HKV_A):
            vh = v_ref[j * BLK:(j + 3) * BLK, h * HEAD_DIM:(h + 1) * HEAD_DIM]
            ph = p_ref[j % 2, h * GRP:(h + 1) * GRP].reshape(GRP * BLK, 3 * BLK)
            o = jnp.dot(ph, vh, preferred_element_type=F32)
            outs += [o[g * BLK:(g + 1) * BLK] / denom[h * GRP + g] for g in range(GRP)]
        ob = jnp.concatenate(outs, axis=1)
        ms = jnp.mean(ob * ob, axis=-1, keepdims=True)
        o_ref[0, j * BLK:(j + 1) * BLK, :] = (ob * lax.rsqrt(ms + RMS_EPS) * g_ref[...]).astype(BF16)

    maxima, denoms = {}, {}
    for t in range(nsub + 3):
        if t < nsub:
            scores(t)
        if 0 <= t - 2 < nsub:
            j = t - 2
            rowsum = _softmax_exp(score_fn(j), p_ref.at[j % 2], mb_ref.at[j % 2], ps_ref.at[j % 2])
            denoms[j] = rowsum + jnp.exp(sink - maxima.pop(j))
        if 0 <= t - 3 < nsub:
            values(t - 3, denoms.pop(t - 3))
        if 0 <= t - 1 < nsub:
            j = t - 1
            maxima[j] = _softmax_max(score_fn(j), s_ref.shape[1:], pm_ref.at[j % 2], mb_ref.at[j % 2], floor=sink)


def _window(qa, ka, va, sink, g):
    b, seq, _ = qa.shape
    nb = seq // BLK
    r = TQ // BLK
    cur = lambda w: pl.BlockSpec((1, TQ, w), lambda bi, i: (bi, i, 0))
    prev = pl.BlockSpec((1, BLK, DKA), lambda bi, i: (bi, jnp.maximum(i * r - 1, 0), 0))
    nxt = pl.BlockSpec((1, BLK, DKA), lambda bi, i: (bi, jnp.minimum((i + 1) * r, nb - 1), 0))
    sink3 = sink.reshape(HQ_A, 1, 1)
    return pl.pallas_call(
        functools.partial(_window_kernel, nb=nb),
        grid=(b, seq // TQ),
        in_specs=[cur(DQA), prev, cur(DKA), nxt, prev, cur(DKA), nxt,
                  pl.BlockSpec(sink3.shape, lambda bi, i: (0, 0, 0)), pl.BlockSpec(g.shape, lambda bi, i: (0, 0))],
        out_specs=cur(DQA),
        out_shape=jax.ShapeDtypeStruct((b, seq, DQA), BF16),
        scratch_shapes=[pltpu.VMEM((TQ + 2 * BLK, DKA), BF16), pltpu.VMEM((TQ + 2 * BLK, DKA), BF16),
                        pltpu.VMEM((3, HQ_A, BLK, 3 * BLK), F32), pltpu.VMEM((2, HQ_A, BLK, 3 * BLK), BF16),
                        ] + [pltpu.VMEM((2, HQ_A, BLK, LANES), F32)] * 3,
        compiler_params=_cparams(("parallel", "parallel")),
        name="window_attn",
    )(qa, ka, ka, ka, va, va, va, sink3, g)


NA_RB = 8
NA_TILE = NA_RB * NA_Q
NA_TILE_ROWS = NA_RB * NA_QROWS
NA_HALO_ROWS = NA_ROWS
NA_HALO = NA_HALO_ROWS * GRID_W


def _natten_kernel(q_ref, kp_ref, kc_ref, kn_ref, vp_ref, vc_ref, vn_ref, tab_ref, g_ref, o_ref,
                   k_ref, v_ref, s_ref, p_ref, pm_ref, mb_ref, ps_ref, *, rows):
    i = pl.program_id(1)
    nrb = rows // NA_QROWS
    for dst, (before, cur, after) in ((k_ref, (kp_ref, kc_ref, kn_ref)), (v_ref, (vp_ref, vc_ref, vn_ref))):
        dst[0:NA_HALO] = before[0]
        dst[NA_HALO:NA_HALO + NA_TILE] = cur[0]
        dst[NA_HALO + NA_TILE:2 * NA_HALO + NA_TILE] = after[0]

    def block(r):
        rb = i * NA_RB + r
        kstart = jnp.clip(NA_QROWS * rb - NA_ROWS // 2, 0, rows - NA_KROWS)
        off = (kstart - (i * NA_TILE_ROWS - NA_HALO_ROWS)) * GRID_W
        variant = jnp.where(rb < 2, rb, jnp.where(rb >= nrb - 2, rb - (nrb - 5), 2))
        return pl.ds(pl.multiple_of(off, GRID_W), NA_KEYS), variant

    def scores(r):
        keys, variant = block(r)
        for h in range(H_B):
            sl = slice(h * HEAD_DIM, (h + 1) * HEAD_DIM)
            s = lax.dot_general(q_ref[0, r * NA_Q:(r + 1) * NA_Q, sl], k_ref[keys, sl], _NT,
                                preferred_element_type=F32)
            s_ref[r % 3, h] = s + tab_ref[variant, h]

    def score_fn(r):
        sb = s_ref.at[r % 3]
        return lambda h, rc, ls: sb[h, rc, ls]

    def values(r, denom):
        keys, _ = block(r)
        outs = []
        for h in range(H_B):
            sl = slice(h * HEAD_DIM, (h + 1) * HEAD_DIM)
            outs.append(jnp.dot(p_ref[r % 2, h], v_ref[keys, sl], preferred_element_type=F32) / denom[h])
        ob = jnp.concatenate(outs, axis=1)
        ms = jnp.mean(ob * ob, axis=-1, keepdims=True)
        o_ref[0, r * NA_Q:(r + 1) * NA_Q, :] = (ob * lax.rsqrt(ms + RMS_EPS) * g_ref[...]).astype(BF16)

    denoms = {}
    for t in range(NA_RB + 3):
        if t < NA_RB:
            scores(t)
        if 0 <= t - 2 < NA_RB:
            r = t - 2
            denoms[r] = _softmax_exp(score_fn(r), p_ref.at[r % 2], mb_ref.at[r % 2], ps_ref.at[r % 2])
        if 0 <= t - 3 < NA_RB:
            values(t - 3, denoms.pop(t - 3))
        if 0 <= t - 1 < NA_RB:
            r = t - 1
            _softmax_max(score_fn(r), s_ref.shape[1:], pm_ref.at[r % 2], mb_ref.at[r % 2])


def _natten_variant_rowblocks(nrb):
    return (0, 1, 2, nrb - 2, nrb - 1)


def _natten_tables(rpb, rows):
    nrb = rows // NA_QROWS
    qc = np.arange(GRID_W)
    kc = np.arange(GRID_W)
    cs = np.clip(qc - NA_COLS // 2, 0, GRID_W - NA_COLS)
    col_ok = (kc[None, :] >= cs[:, None]) & (kc[None, :] < cs[:, None] + NA_COLS)
    cidx = np.clip(kc[None, :] - qc[:, None] + NA_COLS - 1, 0, 2 * NA_COLS - 2)
    tabs = []
    for rb in _natten_variant_rowblocks(nrb):
        r0 = rb * NA_QROWS
        qrows = r0 + np.arange(NA_QROWS)
        rs = np.clip(qrows - NA_ROWS // 2, 0, rows - NA_ROWS)
        kstart = int(np.clip(r0 - NA_ROWS // 2, 0, rows - NA_KROWS))
        krows = kstart + np.arange(NA_KROWS)
        row_ok = (krows[None, :] >= rs[:, None]) & (krows[None, :] < rs[:, None] + NA_ROWS)
        ridx = np.clip(krows[None, :] - qrows[:, None] + NA_ROWS - 1, 0, 2 * NA_ROWS - 2)
        valid = row_ok[:, None, :, None] & col_ok[None, :, None, :]
        r1h = (ridx[:, :, None] == np.arange(2 * NA_ROWS - 1)).astype(np.float32)
        c1h = (cidx[:, :, None] == np.arange(2 * NA_COLS - 1)).astype(np.float32)
        bias = jnp.einsum("rka,hab,qcb->hrqkc", r1h, rpb.astype(F32), c1h, precision=lax.Precision.HIGHEST)
        bias = jnp.where(valid[None], bias, NEG_INF)
        tabs.append(bias.reshape(H_B, NA_Q, NA_KEYS))
    return jnp.stack(tabs)


def _natten(qb, kb, vb, tabs, g):
    b, seq, _ = qb.shape
    rows = seq // GRID_W
    nrb = rows // NA_QROWS
    nt = seq // NA_TILE
    assert rows >= NA_KROWS and nrb >= 5 and nt * NA_TILE == seq
    cur = pl.BlockSpec((1, NA_TILE, DB), lambda bi, i: (bi, i, 0))
    hpt = NA_TILE // NA_HALO
    prev = pl.BlockSpec((1, NA_HALO, DB), lambda bi, i: (bi, jnp.maximum(i * hpt - 1, 0), 0))
    nxt = pl.BlockSpec((1, NA_HALO, DB), lambda bi, i: (bi, jnp.minimum((i + 1) * hpt, nt * hpt - 1), 0))
    return pl.pallas_call(
        functools.partial(_natten_kernel, rows=rows),
        grid=(b, nt),
        in_specs=[cur, prev, cur, nxt, prev, cur, nxt,
                  pl.BlockSpec(tabs.shape, lambda bi, i: (0, 0, 0, 0)),
                  pl.BlockSpec(g.shape, lambda bi, i: (0, 0))],
        out_specs=cur,
        out_shape=jax.ShapeDtypeStruct((b, seq, DB), BF16),
        scratch_shapes=[pltpu.VMEM((NA_TILE + 2 * NA_HALO, DB), BF16), pltpu.VMEM((NA_TILE + 2 * NA_HALO, DB), BF16),
                        pltpu.VMEM((3, H_B, NA_Q, NA_KEYS), F32), pltpu.VMEM((2, H_B, NA_Q, NA_KEYS), BF16),
                        ] + [pltpu.VMEM((2, H_B, NA_Q, LANES), F32)] * 3,
        compiler_params=_cparams(("parallel", "arbitrary")),
        name="natten",
    )(qb, kb, kb, kb, vb, vb, vb, tabs, g)


def _outproj_kernel(x_ref, oa_ref, ob_ref, wo_ref, g_ref, wr_ref, x1_ref, h_ref, aff_ref):
    def residual(rows):
        return (x_ref[rows, :]
                + jnp.dot(oa_ref[rows, :], wo_ref[0:DQA, :], preferred_element_type=F32)
                + jnp.dot(ob_ref[rows, :], wo_ref[DQA:DQA + DB, :], preferred_element_type=F32))

    def route(rows, x1):
        x1_ref[rows, :] = x1
        ms = jnp.mean(x1 * x1, axis=-1, keepdims=True)
        hf = x1 * lax.rsqrt(ms + RMS_EPS) * g_ref[...]
        h_hi = hf.astype(BF16)
        h_lo = (hf - h_hi.astype(F32)).astype(BF16)
        h_ref[rows, :] = _pack_rows(hf)
        l_hi = lax.dot_general(wr_ref[...], h_hi, _NT, preferred_element_type=F32)
        l_lo = lax.dot_general(wr_ref[0:N_EXPERTS, :], h_lo, _NT, preferred_element_type=F32)
        logits = l_hi[0:N_EXPERTS] + l_hi[N_EXPERTS:2 * N_EXPERTS] + l_lo
        m = jnp.max(logits, axis=0, keepdims=True)
        e = jnp.exp(logits - m)
        aff = e / jnp.sum(e, axis=0, keepdims=True)
        for k in range(TSUB // LANES):
            aff_ref[:, rows.start // LANES + k, :] = aff[:, k * LANES:(k + 1) * LANES]

    subs = [slice(r0, r0 + TSUB) for r0 in range(0, TM, TSUB)]
    x1 = residual(subs[0])
    for k, rows in enumerate(subs):
        nxt = residual(subs[k + 1]) if k + 1 < len(subs) else None
        route(rows, x1)
        x1 = nxt


def _outproj(x2, oa, ob, wo_bf, g, wr_t):
    n = x2.shape[0]
    tok = lambda w: pl.BlockSpec((TM, w), lambda i: (i, 0))
    full = lambda a: pl.BlockSpec(a.shape, lambda i: (0,) * a.ndim)
    return pl.pallas_call(
        _outproj_kernel,
        grid=(n // TM,),
        in_specs=[tok(D_MODEL), tok(DQA), tok(DB), full(wo_bf), full(g), full(wr_t)],
        out_specs=[tok(D_MODEL), tok(HALF), pl.BlockSpec((N_EXPERTS, TM // LANES, LANES), lambda i: (0, i, 0))],
        out_shape=[jax.ShapeDtypeStruct((n, D_MODEL), F32), jax.ShapeDtypeStruct((n, HALF), U32),
                   jax.ShapeDtypeStruct((N_EXPERTS, n // LANES, LANES), F32)],
        compiler_params=_cparams(("parallel",)),
        name="outproj_router",
    )(x2, oa, ob, wo_bf, g, wr_t)


def _cumsum_tokens(x, u_ref, l_ref):
    within = jnp.dot(x.astype(BF16), u_ref[...], preferred_element_type=F32)
    rowtot = jnp.broadcast_to(within[:, LANES - 1:LANES], within.shape).astype(BF16)
    return within + jnp.dot(l_ref[...], rowtot, preferred_element_type=F32)


def _route_select_kernel(aff_ref, u_ref, l_ref, pos_ref, cexcl_ref, sincl_ref, sexcl_ref, run_ref, *, cap):
    e = pl.program_id(0)
    bits = pltpu.bitcast(aff_ref[0], jnp.int32)

    def count(mask):
        c = jnp.sum(jnp.where(mask, 1.0, 0.0), axis=0, keepdims=True)
        return jnp.sum(c, axis=1, keepdims=True)

    def step(i, t):
        cand = t | jnp.left_shift(jnp.int32(1), 30 - i)
        return jnp.where(count(bits >= cand) >= cap, cand, t)

    thr = lax.fori_loop(0, 31, step, jnp.zeros((1, 1), jnp.int32))
    gt = bits > thr
    eq = bits == thr
    need = cap - count(gt)
    eqf = jnp.where(eq, 1.0, 0.0)
    tie_rank = _cumsum_tokens(eqf, u_ref, l_ref) - eqf
    sel = jnp.where(gt | (eq & (tie_rank < need)), 1.0, 0.0)
    pos = _cumsum_tokens(sel, u_ref, l_ref)

    @pl.when(e == 0)
    def _():
        run_ref[...] = jnp.zeros_like(run_ref)

    cnt_before = run_ref[0]
    s_incl = run_ref[1] + pos
    pos_ref[0] = pos
    cexcl_ref[0] = cnt_before
    cnt = cnt_before + sel
    run_ref[0] = cnt
    run_ref[1] = s_incl
    sincl_ref[0] = s_incl
    sexcl_ref[0] = s_incl - cnt


def _route_select(aff3, cap):
    e, r, _ = aff3.shape
    u = jnp.asarray(np.triu(np.ones((LANES, LANES), np.float32)), BF16)
    lo = jnp.asarray(np.tril(np.ones((r, r), np.float32), -1), BF16)
    per_e = pl.BlockSpec((1, r, LANES), lambda ei: (ei, 0, 0))
    shared = pl.BlockSpec((1, r, LANES), lambda ei: (0, 0, 0))
    full = lambda a: pl.BlockSpec(a.shape, lambda ei: (0,) * a.ndim)
    return pl.pallas_call(
        functools.partial(_route_select_kernel, cap=cap),
        grid=(e,),
        in_specs=[per_e, full(u), full(lo)],
        out_specs=[per_e, per_e, shared, shared],
        out_shape=[jax.ShapeDtypeStruct((e, r, LANES), F32), jax.ShapeDtypeStruct((e, r, LANES), F32),
                   jax.ShapeDtypeStruct((1, r, LANES), F32), jax.ShapeDtypeStruct((1, r, LANES), F32)],
        scratch_shapes=[pltpu.VMEM((2, r, LANES), F32)],
        compiler_params=_cparams(("arbitrary",)),
        name="route_select",
    )(aff3, u, lo)


SEARCH_CHUNK = 1024
_INT3, _INT2, _INT1, _F32 = "int3", "int2", "int1", "f32"
_NPARTS = {_INT3: 3, _INT2: 2, _INT1: 1, _F32: 3}


def _bf16_parts(x, kind):
    if kind == _INT1:
        return [x]
    if kind == _INT2:
        d1 = jnp.floor(x * (1.0 / 256.0))
        return [x - d1 * 256.0, d1]
    if kind == _INT3:
        d2 = jnp.floor(x * (1.0 / 65536.0))
        r = x - d2 * 65536.0
        d1 = jnp.floor(r * (1.0 / 256.0))
        return [r - d1 * 256.0, d1, d2]
    a1 = x.astype(BF16).astype(F32)
    r1 = x - a1
    a2 = r1.astype(BF16).astype(F32)
    return [a1, a2, r1 - a2]


def _join_parts(parts, kind):
    if kind == _INT1:
        return parts[0]
    if kind == _INT2:
        return parts[0] + 256.0 * parts[1]
    if kind == _INT3:
        return parts[0] + 256.0 * parts[1] + 65536.0 * parts[2]
    return (parts[0] + parts[1]) + parts[2]


def _rank_search_kernel(*refs, kinds, count_kind, chunk):
    npay = len(kinds)
    cnt_ref, pay_refs = refs[0], refs[1:1 + npay]
    tok_ref, out_refs = refs[1 + npay], refs[2 + npay:2 + 2 * npay]
    lhs_ref = refs[-1]
    c = pl.program_id(1)
    r = cnt_ref.shape[1]

    @pl.when(c == 0)
    def _():
        counts = cnt_ref[0]
        ends = jnp.broadcast_to(counts[:, LANES - 1:LANES], counts.shape)
        first_row = lax.broadcasted_iota(jnp.int32, counts.shape, 0) == 0
        row_start = jnp.where(first_row, 0.0, pltpu.roll(ends, 1, 0))
        row = 0
        for val, kind in zip([counts - row_start] + [ref[0] for ref in pay_refs], (count_kind,) + tuple(kinds)):
            for part in _bf16_parts(val, kind):
                lhs_ref[row:row + LANES, :] = part.T.astype(BF16)
                row += LANES

    target = (c * chunk + 1 + lax.broadcasted_iota(jnp.int32, (1, chunk), 1)).astype(F32)
    row_end = cnt_ref[0, :, LANES - 1:LANES]
    before = row_end < target
    rho = jnp.sum(jnp.where(before, 1.0, 0.0), axis=0, keepdims=True).astype(jnp.int32)
    base = jnp.max(jnp.where(before, row_end, 0.0), axis=0, keepdims=True)
    onehot = jnp.where(lax.broadcasted_iota(jnp.int32, (r, chunk), 0) == rho, 1.0, 0.0).astype(BF16)
    fetched = jnp.dot(lhs_ref[...], onehot, preferred_element_type=F32)

    def take(first_part, kind):
        parts = [fetched[(first_part + k) * LANES:(first_part + k + 1) * LANES] for k in range(_NPARTS[kind])]
        return _join_parts(parts, kind)

    lam = jnp.sum(jnp.where(take(0, count_kind) < target - base, 1.0, 0.0), axis=0, keepdims=True).astype(jnp.int32)
    tok_ref[0] = rho * LANES + lam
    at_lane = lax.broadcasted_iota(jnp.int32, (LANES, chunk), 0) == lam
    first_part = _NPARTS[count_kind]
    for out_ref, kind in zip(out_refs, kinds):
        out_ref[0] = jnp.sum(jnp.where(at_lane, take(first_part, kind), 0.0), axis=0, keepdims=True)
        first_part += _NPARTS[kind]


def _rank_search(counts, per_token, nslots, payloads=()):
    g, r, _ = counts.shape
    chunk = min(SEARCH_CHUNK, nslots)
    kinds = tuple(k for _, k in payloads)
    count_kind = _INT1 if per_token * LANES < 256 else _INT2
    assert per_token * LANES < 65536
    nparts = _NPARTS[count_kind] + sum(_NPARTS[k] for k in kinds)

    def in_spec(a):
        if a.shape[0] == 1:
            return pl.BlockSpec((1, r, LANES), lambda gi, ci: (0, 0, 0))
        return pl.BlockSpec((1, r, LANES), lambda gi, ci: (gi, 0, 0))

    out_spec = pl.BlockSpec((1, 1, chunk), lambda gi, ci: (gi, 0, ci))
    outs = pl.pallas_call(
        functools.partial(_rank_search_kernel, kinds=kinds, count_kind=count_kind, chunk=chunk),
        grid=(g, nslots // chunk),
        in_specs=[in_spec(counts)] + [in_spec(a) for a, _ in payloads],
        out_specs=[out_spec] * (1 + len(kinds)),
        out_shape=[jax.ShapeDtypeStruct((g, 1, nslots), jnp.int32)]
        + [jax.ShapeDtypeStruct((g, 1, nslots), F32)] * len(kinds),
        scratch_shapes=[pltpu.VMEM((nparts * LANES, r), BF16)],
        compiler_params=_cparams(("parallel", "arbitrary")),
        name="rank_search",
    )(counts, *[a for a, _ in payloads])
    return outs[0], outs[1:]


def _sc_scatter_rows(rows, idx):
    m, w = rows.shape
    nchunks = m // GATHER_ROWS
    per_worker = nchunks // SC_WORKERS
    assert per_worker * SC_WORKERS * GATHER_ROWS == m and per_worker % 8 == 0
    mesh = plsc.VectorSubcoreMesh(core_axis_name="c", subcore_axis_name="s")

    @functools.partial(
        pl.kernel, mesh=mesh, out_type=jax.ShapeDtypeStruct((m, w), rows.dtype),
        scratch_types=[pltpu.VMEM((per_worker, GATHER_ROWS), jnp.int32),
                       pltpu.VMEM((GATHER_ROWS, w), rows.dtype),
                       pltpu.SemaphoreType.DMA])
    def scatter(rows_hbm, idx_hbm, out_hbm, idx_v, rows_v, sem):
        wid = lax.axis_index("s") * SC_CORES + lax.axis_index("c")
        first = wid * per_worker
        pltpu.sync_copy(idx_hbm.at[pl.ds(first, per_worker)], idx_v)

        @pl.loop(0, per_worker)
        def _(c):
            pltpu.sync_copy(rows_hbm.at[pl.ds((first + c) * GATHER_ROWS, GATHER_ROWS)], rows_v)
            pltpu.async_copy(rows_v, out_hbm.at[idx_v.at[c]], sem).wait()

    return scatter(rows, idx.reshape(nchunks, GATHER_ROWS))


def _moe_kernel(x_ref, gate_ref, wg_ref, wu_ref, wd_ref, o_ref):
    x = _unpack_rows(x_ref[0])
    y = jnp.zeros((x.shape[0], D_MODEL), F32)
    for c in range(D_EXPERT // FC):
        sl = slice(c * FC, (c + 1) * FC)
        a = jnp.dot(x, wg_ref[0, :, sl], preferred_element_type=F32)
        u = jnp.dot(x, wu_ref[0, :, sl], preferred_element_type=F32)
        hid = (a * jax.nn.sigmoid(a) * u).astype(BF16)
        y = y + jnp.dot(hid, wd_ref[0, sl, :], preferred_element_type=F32)
    gate = jnp.broadcast_to(gate_ref[0], (LANES, x.shape[0])).T
    o_ref[0] = _pack_rows(y * jnp.tile(gate, (1, D_MODEL // LANES)))


def _moe(xe, gate3, wg, wu, wd):
    e, cap, _ = xe.shape
    tc = min(TC, cap)
    tokb = lambda w: pl.BlockSpec((1, tc, w), lambda ei, i: (ei, i, 0))
    wspec = lambda a: pl.BlockSpec((1,) + a.shape[1:], lambda ei, i: (ei, 0, 0))
    return pl.pallas_call(
        _moe_kernel,
        grid=(e, cap // tc),
        in_specs=[tokb(HALF), pl.BlockSpec((1, 1, tc), lambda ei, i: (ei, 0, i)), wspec(wg), wspec(wu), wspec(wd)],
        out_specs=tokb(HALF),
        out_shape=jax.ShapeDtypeStruct((e, cap, HALF), U32),
        compiler_params=_cparams(("parallel", "arbitrary")),
        name="moe_ffn",
    )(xe, gate3, wg, wu, wd)


_FIRST, _LAST, _ACTIVE = 1, 2, 4


def _combine_kernel(tile_ref, blk_ref, flag_ref, tok_ref, yg_ref, x1_ref, g_ref, o_ref, acc_ref):
    s = pl.program_id(0)
    flag = flag_ref[s]

    @pl.when((flag & _FIRST) != 0)
    def _():
        acc_ref[...] = jnp.zeros_like(acc_ref)

    @pl.when((flag & _ACTIVE) != 0)
    def _():
        tok_row = tile_ref[s] * TS + lax.broadcasted_iota(jnp.int32, (TS, SB), 0)
        onehot = jnp.where(tok_ref[0] == tok_row, 1.0, 0.0).astype(BF16)
        acc_ref[...] += jnp.dot(onehot, _unpack_rows(yg_ref[...]), preferred_element_type=F32)

    @pl.when((flag & _LAST) != 0)
    def _():
        y = x1_ref[...] + acc_ref[...]
        ms = jnp.mean(y * y, axis=-1, keepdims=True)
        o_ref[...] = y * lax.rsqrt(ms + RMS_EPS) * g_ref[...]


def _combine_schedule(bounds, nslots):
    nt, nblk = bounds.shape[0] - 1, nslots // SB
    steps = nt + nblk
    lo, hi = bounds[:-1], bounds[1:]
    b_lo = jnp.minimum(lo // SB, nblk - 1)
    b_hi = jnp.where(hi > lo, (hi - 1) // SB, b_lo)
    nst = b_hi - b_lo + 1
    cum = jnp.cumsum(nst)
    start = cum - nst
    s = jnp.arange(steps, dtype=jnp.int32)
    tile = jnp.minimum(jnp.searchsorted(cum, s, side="right").astype(jnp.int32), nt - 1)
    active = s < cum[-1]
    blk = jnp.where(active, b_lo[tile] + s - start[tile], b_hi[nt - 1])
    first = active & (s == start[tile])
    last = active & (s == cum[tile] - 1)
    flag = first * _FIRST + last * _LAST + active * _ACTIVE
    return tile, blk.astype(jnp.int32), flag.astype(jnp.int32)


def _combine(tile, blk, flag, tok3, yg, x1, g):
    n = x1.shape[0]
    steps = tile.shape[0]
    grid_spec = pltpu.PrefetchScalarGridSpec(
        num_scalar_prefetch=3,
        grid=(steps,),
        in_specs=[pl.BlockSpec((1, 1, SB), lambda s, t, b, f: (b[s], 0, 0)),
                  pl.BlockSpec((SB, HALF), lambda s, t, b, f: (b[s], 0)),
                  pl.BlockSpec((TS, D_MODEL), lambda s, t, b, f: (t[s], 0)),
                  pl.BlockSpec(g.shape, lambda s, t, b, f: (0, 0))],
        out_specs=pl.BlockSpec((TS, D_MODEL), lambda s, t, b, f: (t[s], 0)),
        scratch_shapes=[pltpu.VMEM((TS, D_MODEL), F32)],
    )
    return pl.pallas_call(
        _combine_kernel,
        grid_spec=grid_spec,
        out_shape=jax.ShapeDtypeStruct((n, D_MODEL), F32),
        compiler_params=_cparams(("arbitrary",)),
        name="combine_norm",
    )(tile, blk, flag, tok3, yg, x1, g)


def _group_forward(x, p):
    b, seq, _ = x.shape
    n = b * seq
    cap = CAP_FACTOR * n // N_EXPERTS
    x2 = x.reshape(n, D_MODEL)
    cos_t, sa_t, sb_t = p["rot"]
    qa, ka, va, qb, kb, vb = _inproj(x2, p["g_attn"], p["w_in"], cos_t[:seq], sa_t[:seq], sb_t[:seq], seq)
    r3 = lambda a: a.reshape(b, seq, a.shape[-1])
    oa = _window(r3(qa), r3(ka), r3(va), p["sink"], p["g_out_a"])
    ob = _natten(r3(qb), r3(kb), r3(vb), p["na_tabs"], p["g_out_b"])
    x1, h, aff3 = _outproj(x2, oa.reshape(n, DQA), ob.reshape(n, DB), p["w_out"], p["g_ffn"], p["w_router_t"])

    nslots = N_EXPERTS * cap
    pos, c_excl, s_incl, s_excl = _route_select(aff3, cap)
    idx, (gate, s_at, c_at) = _rank_search(pos, 1, cap, [(aff3, _F32), (s_excl, _INT3), (c_excl, _INT1)])
    dest = (s_at + c_at).astype(jnp.int32).reshape(nslots)
    xe = _sc_gather_rows(h, idx.reshape(nslots)).reshape(N_EXPERTS, cap, HALF)
    ye = _moe(xe, gate, p["w_gate"], p["w_up"], p["w_down"])
    yg = _sc_scatter_rows(ye.reshape(nslots, HALF), dest)

    tok_sorted, _ = _rank_search(s_incl, N_EXPERTS, nslots)
    tile_end = s_incl.reshape(n)[TS - 1::TS].astype(jnp.int32)
    bounds = jnp.concatenate([jnp.zeros((1,), jnp.int32), tile_end])
    tile, blk, flag = _combine_schedule(bounds, nslots)
    y = _combine(tile, blk, flag, tok_sorted.reshape(-1, 1, SB), yg, x1, p["g_final"])
    return y.reshape(b, seq, D_MODEL)


def kernel(x_prompt, x_sample, g_attn, w_in, g_out_a, g_out_b, sink_a, rpb_b, w_out, g_ffn, w_router,
           w_gate, w_up, w_down, g_final):
    assert g_attn.shape[0] == 1, "single trunk layer"
    wr = w_router[0].T
    wr_hi = wr.astype(BF16)
    wr_lo = (wr - wr_hi.astype(F32)).astype(BF16)
    p = {
        "g_attn": g_attn[0][None, :], "w_in": w_in[0].astype(BF16),
        "g_out_a": g_out_a[0][None, :], "g_out_b": g_out_b[0][None, :],
        "sink": sink_a[0], "na_tabs": _natten_tables(rpb_b[0], 4 * NA_ROWS), "w_out": w_out[0].astype(BF16),
        "g_ffn": g_ffn[0][None, :], "w_router_t": jnp.concatenate([wr_hi, wr_lo], axis=0),
        "w_gate": w_gate[0].astype(BF16), "w_up": w_up[0].astype(BF16), "w_down": w_down[0].astype(BF16),
        "g_final": g_final[None, :],
        "rot": _rotary_tables(max(x_prompt.shape[1], x_sample.shape[1])),
    }
    return (_group_forward(x_prompt, p), _group_forward(x_sample, p))
```

```python
import functools

import jax
import jax.numpy as jnp
import numpy as np
from jax import lax
from jax.experimental import pallas as pl
from jax.experimental.pallas import tpu as pltpu
from jax.experimental.pallas import tpu_sc as plsc

D_MODEL = 1024
HEAD_DIM = 64
HQ_A = 8
HKV_A = 2
H_B = 8
GRP = HQ_A // HKV_A
DQA = HQ_A * HEAD_DIM
DKA = HKV_A * HEAD_DIM
DB = H_B * HEAD_DIM
D_IN = DQA + 2 * DKA + 3 * DB
WINDOW = 128
BLK = 128
ROPE_THETA = 500000.0
ROT_DIM = HEAD_DIM // 4
GRID_W = 64
NA_ROWS = 8
NA_COLS = 16
NA_QROWS = 2
NA_KROWS = NA_ROWS + NA_QROWS
NA_KEYS = NA_KROWS * GRID_W
NA_Q = NA_QROWS * GRID_W
N_EXPERTS = 16
CAP_FACTOR = 2
D_EXPERT = 2 * D_MODEL
RMS_EPS = 1e-6
NEG_INF = -1e30
SCALE = HEAD_DIM ** -0.5

LANES = 128
TM = 1024
TSUB = 512
TQ = 1024
TC = 1024
FC = 512
TS = 512
SB = 1024
VMEM_LIMIT = 56 * 1024 * 1024

BF16 = jnp.bfloat16
F32 = jnp.float32
_NT = (((1,), (1,)), ((), ()))


def _cparams(sem):
    return pltpu.CompilerParams(dimension_semantics=sem, vmem_limit_bytes=VMEM_LIMIT)


HALF = D_MODEL // 2
U32 = jnp.uint32


def _pack_rows(v):
    r = v.astype(BF16).astype(F32)
    hi = pltpu.bitcast(r[:, :HALF], U32)
    lo = pltpu.bitcast(r[:, HALF:], U32)
    return (hi & jnp.uint32(0xFFFF0000)) | (lo >> 16)


def _unpack_rows(w):
    hi = pltpu.bitcast(w & jnp.uint32(0xFFFF0000), F32)
    lo = pltpu.bitcast(w << 16, F32)
    return jnp.concatenate([hi, lo], axis=1).astype(BF16)


SC_CORES = 2
SC_SUBCORES = 16
SC_WORKERS = SC_CORES * SC_SUBCORES
GATHER_ROWS = 64


def _sc_gather_rows(table, idx):
    m, w = idx.shape[0], table.shape[1]
    nchunks = m // GATHER_ROWS
    per_worker = nchunks // SC_WORKERS
    assert per_worker * SC_WORKERS * GATHER_ROWS == m and per_worker % 8 == 0
    mesh = plsc.VectorSubcoreMesh(core_axis_name="c", subcore_axis_name="s")

    @functools.partial(
        pl.kernel, mesh=mesh, out_type=jax.ShapeDtypeStruct((m, w), table.dtype),
        scratch_types=[pltpu.VMEM((per_worker, GATHER_ROWS), jnp.int32),
                       pltpu.VMEM((GATHER_ROWS, w), table.dtype),
                       pltpu.SemaphoreType.DMA])
    def gather(tab_hbm, idx_hbm, out_hbm, idx_v, rows_v, sem):
        wid = lax.axis_index("s") * SC_CORES + lax.axis_index("c")
        first = wid * per_worker
        pltpu.sync_copy(idx_hbm.at[pl.ds(first, per_worker)], idx_v)

        @pl.loop(0, per_worker)
        def _(c):
            pltpu.async_copy(tab_hbm.at[idx_v.at[c]], rows_v, sem).wait()
            pltpu.sync_copy(rows_v, out_hbm.at[pl.ds((first + c) * GATHER_ROWS, GATHER_ROWS)])

    return gather(table, idx.reshape(nchunks, GATHER_ROWS))


def _inproj_kernel(x_ref, g_ref, w_ref, wkt_ref, cos_ref, sa_ref, sb_ref,
                   qa_ref, ka_ref, va_ref, qb_ref, kbt_ref, vb_ref):
    def normed(rows):
        x = x_ref[rows, :]
        ms = jnp.mean(x * x, axis=-1, keepdims=True)
        return (x * lax.rsqrt(ms + RMS_EPS) * g_ref[...]).astype(BF16)

    def project(rows, hn):
        c, sa, sb = cos_ref[rows, :], sa_ref[rows, :], sb_ref[rows, :]

        def proj(lo, n):
            return jnp.dot(hn, w_ref[:, lo:lo + n], preferred_element_type=F32)

        def rot(blk):
            return blk * c + pltpu.roll(blk, LANES - ROT_DIM // 2, 1) * sa + pltpu.roll(blk, ROT_DIM // 2, 1) * sb

        for j in range(DQA // LANES):
            qa_ref[rows, j * LANES:(j + 1) * LANES] = (rot(proj(j * LANES, LANES)) * SCALE).astype(BF16)
        ka_ref[rows, :] = rot(proj(DQA, DKA)).astype(BF16)
        va_ref[rows, :] = proj(DQA + DKA, DKA).astype(BF16)
        qb_ref[rows, :] = (proj(DQA + 2 * DKA, DB) * SCALE).astype(BF16)
        kbt_ref[:, rows] = lax.dot_general(wkt_ref[...], hn, _NT, preferred_element_type=F32).astype(BF16)
        vb_ref[rows, :] = proj(DQA + 2 * DKA + 2 * DB, DB).astype(BF16)

    subs = [slice(r0, r0 + TSUB) for r0 in range(0, TM, TSUB)]
    hn = normed(subs[0])
    for k, rows in enumerate(subs):
        nxt = normed(subs[k + 1]) if k + 1 < len(subs) else None
        project(rows, hn)
        hn = nxt


def _inproj(x2, g, w_bf, cos_t, sa_t, sb_t, seq):
    n = x2.shape[0]
    per_seq = seq // TM
    tok = lambda w: pl.BlockSpec((TM, w), lambda i: (i, 0))
    full = lambda a: pl.BlockSpec(a.shape, lambda i: (0,) * a.ndim)
    pos = pl.BlockSpec((TM, LANES), lambda i: (i % per_seq, 0))
    kb_lo = DQA + 2 * DKA + DB
    wkt = w_bf[:, kb_lo:kb_lo + DB].T
    feat = pl.BlockSpec((DB, TM), lambda i: (0, i))
    tshape = lambda w: jax.ShapeDtypeStruct((n, w), BF16)
    return pl.pallas_call(
        _inproj_kernel,
        grid=(n // TM,),
        in_specs=[tok(D_MODEL), full(g), full(w_bf), full(wkt), pos, pos, pos],
        out_specs=[tok(DQA), tok(DKA), tok(DKA), tok(DB), feat, tok(DB)],
        out_shape=[tshape(DQA), tshape(DKA), tshape(DKA), tshape(DB), jax.ShapeDtypeStruct((DB, n), BF16),
                   tshape(DB)],
        compiler_params=_cparams(("parallel",)),
        name="inproj",
    )(x2, g, w_bf, wkt, cos_t, sa_t, sb_t)


def _rotary_tables(seq):
    half = ROT_DIM // 2
    inv_freq = jnp.float32(ROPE_THETA) ** (-(jnp.arange(half, dtype=F32) * 2.0) / ROT_DIM)
    ang = jnp.arange(seq, dtype=F32)[:, None] * inv_freq[None, :]
    cos, sin = jnp.cos(ang), jnp.sin(ang)
    ones = jnp.ones((seq, HEAD_DIM - ROT_DIM), F32)
    zeros = jnp.zeros((seq, HEAD_DIM - ROT_DIM), F32)
    zh = jnp.zeros((seq, half), F32)
    rep = LANES // HEAD_DIM
    cos_t = jnp.tile(jnp.concatenate([cos, cos, ones], axis=1), (1, rep))
    sa_t = jnp.tile(jnp.concatenate([-sin, zh, zeros], axis=1), (1, rep))
    sb_t = jnp.tile(jnp.concatenate([zh, sin, zeros], axis=1), (1, rep))
    return cos_t, sa_t, sb_t


SM_ROWS = 32


def _sm_chunks(shape):
    nh, nq, nk = shape
    chunks = [(h, slice(r0, r0 + SM_ROWS)) for h in range(nh) for r0 in range(0, nq, SM_ROWS)]
    return chunks, [slice(l0, l0 + LANES) for l0 in range(0, nk, LANES)]


def _softmax_max(score, shape, pm_ref, mb_ref, floor=None):
    chunks, lanes = _sm_chunks(shape)
    for h, rc in chunks:
        pm_ref[h, rc, :] = functools.reduce(jnp.maximum, [score(h, rc, ls) for ls in lanes])
    m = jnp.max(pm_ref[...], axis=-1, keepdims=True)
    if floor is not None:
        m = jnp.maximum(m, floor)
    mb_ref[...] = jnp.broadcast_to(m, mb_ref.shape)
    return m


def _softmax_exp(score, p_ref, mb_ref, ps_ref):
    chunks, lanes = _sm_chunks(p_ref.shape)
    for h, rc in chunks:
        mb = mb_ref[h, rc, :]
        total = None
        for ls in lanes:
            p = jnp.exp(score(h, rc, ls) - mb)
            p_ref[h, rc, ls] = p.astype(BF16)
            total = p if total is None else total + p
        ps_ref[h, rc, :] = total
    return jnp.sum(ps_ref[...], axis=-1, keepdims=True)


def _window_kernel(q_ref, kp_ref, kc_ref, kn_ref, vp_ref, vc_ref, vn_ref, sink_ref, g_ref, o_ref,
                   k_ref, v_ref, s_ref, p_ref, pm_ref, mb_ref, ps_ref, *, nb):
    i = pl.program_id(1)
    nsub = TQ // BLK
    k_ref[0:BLK] = kp_ref[0]
    k_ref[BLK:BLK + TQ] = kc_ref[0]
    k_ref[BLK + TQ:TQ + 2 * BLK] = kn_ref[0]
    v_ref[0:BLK] = vp_ref[0]
    v_ref[BLK:BLK + TQ] = vc_ref[0]
    v_ref[BLK + TQ:TQ + 2 * BLK] = vn_ref[0]
    qi = lax.broadcasted_iota(jnp.int32, (BLK, 3 * BLK), 0)
    kj = lax.broadcasted_iota(jnp.int32, (BLK, 3 * BLK), 1)
    band = jnp.abs(kj - BLK - qi) <= WINDOW
    sink = sink_ref[...]

    def scores(j):
        n = i * nsub + j
        k_lo = jnp.where(n >= 1, 0, BLK)
        k_hi = jnp.where(n <= nb - 2, 3 * BLK, 2 * BLK)
        bias = jnp.where(band & (kj >= k_lo) & (kj < k_hi), 0.0, NEG_INF)
        for h in range(HKV_A):
            kh = k_ref[j * BLK:(j + 3) * BLK, h * HEAD_DIM:(h + 1) * HEAD_DIM]
            qs = jnp.concatenate([q_ref[0, j * BLK:(j + 1) * BLK, hd * HEAD_DIM:(hd + 1) * HEAD_DIM]
                                  for hd in range(h * GRP, (h + 1) * GRP)], axis=0)
            s = lax.dot_general(qs, kh, _NT, preferred_element_type=F32)
            s_ref[j % 3, h * GRP:(h + 1) * GRP] = s.reshape(GRP, BLK, 3 * BLK) + bias[None]

    def score_fn(j):
        sb = s_ref.at[j % 3]
        return lambda hd, rc, ls: sb[hd, rc, ls]

    def values(j, denom):
        outs = []
        for h in range(HKV_A):
            vh = v_ref[j * BLK:(j + 3) * BLK, h * HEAD_DIM:(h + 1) * HEAD_DIM]
            ph = p_ref[j % 2, h * GRP:(h + 1) * GRP].reshape(GRP * BLK, 3 * BLK)
            o = jnp.dot(ph, vh, preferred_element_type=F32)
            outs += [o[g * BLK:(g + 1) * BLK] / denom[h * GRP + g] for g in range(GRP)]
        ob = jnp.concatenate(outs, axis=1)
        ms = jnp.mean(ob * ob, axis=-1, keepdims=True)
        o_ref[0, j * BLK:(j + 1) * BLK, :] = (ob * lax.rsqrt(ms + RMS_EPS) * g_ref[...]).astype(BF16)

    maxima, denoms = {}, {}
    for t in range(nsub + 3):
        if t < nsub:
            scores(t)
        if 0 <= t - 2 < nsub:
            j = t - 2
            rowsum = _softmax_exp(score_fn(j), p_ref.at[j % 2], mb_ref.at[j % 2], ps_ref.at[j % 2])
            denoms[j] = rowsum + jnp.exp(sink - maxima.pop(j))
        if 0 <= t - 3 < nsub:
            values(t - 3, denoms.pop(t - 3))
        if 0 <= t - 1 < nsub:
            j = t - 1
            maxima[j] = _softmax_max(score_fn(j), s_ref.shape[1:], pm_ref.at[j % 2], mb_ref.at[j % 2], floor=sink)


def _window(qa, ka, va, sink, g):
    b, seq, _ = qa.shape
    nb = seq // BLK
    r = TQ // BLK
    cur = lambda w: pl.BlockSpec((1, TQ, w), lambda bi, i: (bi, i, 0))
    prev = pl.BlockSpec((1, BLK, DKA), lambda bi, i: (bi, jnp.maximum(i * r - 1, 0), 0))
    nxt = pl.BlockSpec((1, BLK, DKA), lambda bi, i: (bi, jnp.minimum((i + 1) * r, nb - 1), 0))
    sink3 = sink.reshape(HQ_A, 1, 1)
    return pl.pallas_call(
        functools.partial(_window_kernel, nb=nb),
        grid=(b, seq // TQ),
        in_specs=[cur(DQA), prev, cur(DKA), nxt, prev, cur(DKA), nxt,
                  pl.BlockSpec(sink3.shape, lambda bi, i: (0, 0, 0)), pl.BlockSpec(g.shape, lambda bi, i: (0, 0))],
        out_specs=cur(DQA),
        out_shape=jax.ShapeDtypeStruct((b, seq, DQA), BF16),
        scratch_shapes=[pltpu.VMEM((TQ + 2 * BLK, DKA), BF16), pltpu.VMEM((TQ + 2 * BLK, DKA), BF16),
                        pltpu.VMEM((3, HQ_A, BLK, 3 * BLK), F32), pltpu.VMEM((2, HQ_A, BLK, 3 * BLK), BF16),
                        ] + [pltpu.VMEM((2, HQ_A, BLK, LANES), F32)] * 3,
        compiler_params=_cparams(("parallel", "parallel")),
        name="window_attn",
    )(qa, ka, ka, ka, va, va, va, sink3, g)


NA_RB = 8
NA_TILE = NA_RB * NA_Q
NA_TILE_ROWS = NA_RB * NA_QROWS
NA_HALO_ROWS = NA_ROWS // 2
NA_HALO = NA_HALO_ROWS * GRID_W


def _natten_kernel(q_ref, kp_ref, kc_ref, kn_ref, vp_ref, vc_ref, vn_ref, tab_ref, g_ref, o_ref,
                   kt_ref, v_ref, s_ref, p_ref, pm_ref, mb_ref, ps_ref, *, rows):
    i = pl.program_id(1)
    nrb = rows // NA_QROWS
    kt_ref[:, 0:NA_HALO] = kp_ref[...]
    kt_ref[:, NA_HALO:NA_HALO + NA_TILE] = kc_ref[...]
    kt_ref[:, NA_HALO + NA_TILE:2 * NA_HALO + NA_TILE] = kn_ref[...]
    v_ref[0:NA_HALO] = vp_ref[0]
    v_ref[NA_HALO:NA_HALO + NA_TILE] = vc_ref[0]
    v_ref[NA_HALO + NA_TILE:2 * NA_HALO + NA_TILE] = vn_ref[0]

    def block(r):
        rb = i * NA_RB + r
        kstart = jnp.clip(NA_QROWS * rb - NA_ROWS // 2, 0, rows - NA_KROWS)
        off = (kstart - (i * NA_TILE_ROWS - NA_HALO_ROWS)) * GRID_W
        variant = jnp.where(rb < 2, rb, jnp.where(rb >= nrb - 2, rb - (nrb - 5), 2))
        return pl.ds(pl.multiple_of(off, LANES), NA_KEYS), variant

    def scores(r):
        keys, variant = block(r)
        for h in range(H_B):
            sl = slice(h * HEAD_DIM, (h + 1) * HEAD_DIM)
            s = jnp.dot(q_ref[0, r * NA_Q:(r + 1) * NA_Q, sl], kt_ref[sl, keys], preferred_element_type=F32)
            s_ref[r % 3, h] = s + tab_ref[variant, h]

    def score_fn(r):
        sb = s_ref.at[r % 3]
        return lambda h, rc, ls: sb[h, rc, ls]

    def values(r, denom):
        keys, _ = block(r)
        outs = []
        for h in range(H_B):
            sl = slice(h * HEAD_DIM, (h + 1) * HEAD_DIM)
            outs.append(jnp.dot(p_ref[r % 2, h], v_ref[keys, sl], preferred_element_type=F32) / denom[h])
        ob = jnp.concatenate(outs, axis=1)
        ms = jnp.mean(ob * ob, axis=-1, keepdims=True)
        o_ref[0, r * NA_Q:(r + 1) * NA_Q, :] = (ob * lax.rsqrt(ms + RMS_EPS) * g_ref[...]).astype(BF16)

    denoms = {}
    for t in range(NA_RB + 3):
        if t < NA_RB:
            scores(t)
        if 0 <= t - 2 < NA_RB:
            r = t - 2
            denoms[r] = _softmax_exp(score_fn(r), p_ref.at[r % 2], mb_ref.at[r % 2], ps_ref.at[r % 2])
        if 0 <= t - 3 < NA_RB:
            values(t - 3, denoms.pop(t - 3))
        if 0 <= t - 1 < NA_RB:
            r = t - 1
            _softmax_max(score_fn(r), s_ref.shape[1:], pm_ref.at[r % 2], mb_ref.at[r % 2])


def _natten_variant_rowblocks(nrb):
    return (0, 1, 2, nrb - 2, nrb - 1)


def _natten_tables(rpb, rows):
    nrb = rows // NA_QROWS
    qc = np.arange(GRID_W)
    kc = np.arange(GRID_W)
    cs = np.clip(qc - NA_COLS // 2, 0, GRID_W - NA_COLS)
    col_ok = (kc[None, :] >= cs[:, None]) & (kc[None, :] < cs[:, None] + NA_COLS)
    cidx = np.clip(kc[None, :] - qc[:, None] + NA_COLS - 1, 0, 2 * NA_COLS - 2)
    tabs = []
    for rb in _natten_variant_rowblocks(nrb):
        r0 = rb * NA_QROWS
        qrows = r0 + np.arange(NA_QROWS)
        rs = np.clip(qrows - NA_ROWS // 2, 0, rows - NA_ROWS)
        kstart = int(np.clip(r0 - NA_ROWS // 2, 0, rows - NA_KROWS))
        krows = kstart + np.arange(NA_KROWS)
        row_ok = (krows[None, :] >= rs[:, None]) & (krows[None, :] < rs[:, None] + NA_ROWS)
        ridx = np.clip(krows[None, :] - qrows[:, None] + NA_ROWS - 1, 0, 2 * NA_ROWS - 2)
        valid = row_ok[:, None, :, None] & col_ok[None, :, None, :]
        r1h = (ridx[:, :, None] == np.arange(2 * NA_ROWS - 1)).astype(np.float32)
        c1h = (cidx[:, :, None] == np.arange(2 * NA_COLS - 1)).astype(np.float32)
        bias = jnp.einsum("rka,hab,qcb->hrqkc", r1h, rpb.astype(F32), c1h, precision=lax.Precision.HIGHEST)
        bias = jnp.where(valid[None], bias, NEG_INF)
        tabs.append(bias.reshape(H_B, NA_Q, NA_KEYS))
    return jnp.stack(tabs)


def _natten(qb, kbt, vb, tabs, g):
    b, seq, _ = qb.shape
    rows = seq // GRID_W
    nrb = rows // NA_QROWS
    nt = seq // NA_TILE
    assert rows >= NA_KROWS and nrb >= 5 and nt * NA_TILE == seq
    hpt = NA_TILE // NA_HALO
    before = lambda i: jnp.maximum(i * hpt - 1, 0)
    after = lambda i: jnp.minimum((i + 1) * hpt, nt * hpt - 1)
    cur = pl.BlockSpec((1, NA_TILE, DB), lambda bi, i: (bi, i, 0))
    prev = pl.BlockSpec((1, NA_HALO, DB), lambda bi, i: (bi, before(i), 0))
    nxt = pl.BlockSpec((1, NA_HALO, DB), lambda bi, i: (bi, after(i), 0))
    cur_t = pl.BlockSpec((DB, NA_TILE), lambda bi, i: (0, bi * nt + i))
    prev_t = pl.BlockSpec((DB, NA_HALO), lambda bi, i: (0, bi * nt * hpt + before(i)))
    nxt_t = pl.BlockSpec((DB, NA_HALO), lambda bi, i: (0, bi * nt * hpt + after(i)))
    return pl.pallas_call(
        functools.partial(_natten_kernel, rows=rows),
        grid=(b, nt),
        in_specs=[cur, prev_t, cur_t, nxt_t, prev, cur, nxt,
                  pl.BlockSpec(tabs.shape, lambda bi, i: (0, 0, 0, 0)),
                  pl.BlockSpec(g.shape, lambda bi, i: (0, 0))],
        out_specs=cur,
        out_shape=jax.ShapeDtypeStruct((b, seq, DB), BF16),
        scratch_shapes=[pltpu.VMEM((DB, NA_TILE + 2 * NA_HALO), BF16), pltpu.VMEM((NA_TILE + 2 * NA_HALO, DB), BF16),
                        pltpu.VMEM((3, H_B, NA_Q, NA_KEYS), F32), pltpu.VMEM((2, H_B, NA_Q, NA_KEYS), BF16),
                        ] + [pltpu.VMEM((2, H_B, NA_Q, LANES), F32)] * 3,
        compiler_params=_cparams(("parallel", "arbitrary")),
        name="natten",
    )(qb, kbt, kbt, kbt, vb, vb, vb, tabs, g)


def _outproj_kernel(x_ref, oa_ref, ob_ref, wo_ref, g_ref, wr_ref, x1_ref, h_ref, aff_ref):
    def residual(rows):
        return (x_ref[rows, :]
                + jnp.dot(oa_ref[rows, :], wo_ref[0:DQA, :], preferred_element_type=F32)
                + jnp.dot(ob_ref[rows, :], wo_ref[DQA:DQA + DB, :], preferred_element_type=F32))

    def route(rows, x1):
        x1_ref[rows, :] = x1
        ms = jnp.mean(x1 * x1, axis=-1, keepdims=True)
        hf = x1 * lax.rsqrt(ms + RMS_EPS) * g_ref[...]
        h_hi = hf.astype(BF16)
        h_lo = (hf - h_hi.astype(F32)).astype(BF16)
        h_ref[rows, :] = _pack_rows(hf)
        l_hi = lax.dot_general(wr_ref[...], h_hi, _NT, preferred_element_type=F32)
        l_lo = lax.dot_general(wr_ref[0:N_EXPERTS, :], h_lo, _NT, preferred_element_type=F32)
        logits = l_hi[0:N_EXPERTS] + l_hi[N_EXPERTS:2 * N_EXPERTS] + l_lo
        m = jnp.max(logits, axis=0, keepdims=True)
        e = jnp.exp(logits - m)
        aff = e / jnp.sum(e, axis=0, keepdims=True)
        for k in range(TSUB // LANES):
            aff_ref[:, rows.start // LANES + k, :] = aff[:, k * LANES:(k + 1) * LANES]

    subs = [slice(r0, r0 + TSUB) for r0 in range(0, TM, TSUB)]
    x1 = residual(subs[0])
    for k, rows in enumerate(subs):
        nxt = residual(subs[k + 1]) if k + 1 < len(subs) else None
        route(rows, x1)
        x1 = nxt


def _outproj(x2, oa, ob, wo_bf, g, wr_t):
    n = x2.shape[0]
    tok = lambda w: pl.BlockSpec((TM, w), lambda i: (i, 0))
    full = lambda a: pl.BlockSpec(a.shape, lambda i: (0,) * a.ndim)
    return pl.pallas_call(
        _outproj_kernel,
        grid=(n // TM,),
        in_specs=[tok(D_MODEL), tok(DQA), tok(DB), full(wo_bf), full(g), full(wr_t)],
        out_specs=[tok(D_MODEL), tok(HALF), pl.BlockSpec((N_EXPERTS, TM // LANES, LANES), lambda i: (0, i, 0))],
        out_shape=[jax.ShapeDtypeStruct((n, D_MODEL), F32), jax.ShapeDtypeStruct((n, HALF), U32),
                   jax.ShapeDtypeStruct((N_EXPERTS, n // LANES, LANES), F32)],
        compiler_params=_cparams(("parallel",)),
        name="outproj_router",
    )(x2, oa, ob, wo_bf, g, wr_t)


def _cumsum_tokens(x, u_ref, l_ref):
    within = jnp.dot(x.astype(BF16), u_ref[...], preferred_element_type=F32)
    rowtot = jnp.broadcast_to(within[:, LANES - 1:LANES], within.shape).astype(BF16)
    return within + jnp.dot(l_ref[...], rowtot, preferred_element_type=F32)


def _route_select_kernel(aff_ref, u_ref, l_ref, pos_ref, cexcl_ref, sincl_ref, sexcl_ref, run_ref, *, cap):
    e = pl.program_id(0)
    bits = pltpu.bitcast(aff_ref[0], jnp.int32)

    def count(mask):
        c = jnp.sum(jnp.where(mask, 1.0, 0.0), axis=0, keepdims=True)
        return jnp.sum(c, axis=1, keepdims=True)

    def step(i, t):
        cand = t | jnp.left_shift(jnp.int32(1), 30 - i)
        return jnp.where(count(bits >= cand) >= cap, cand, t)

    thr = lax.fori_loop(0, 31, step, jnp.zeros((1, 1), jnp.int32))
    gt = bits > thr
    eq = bits == thr
    need = cap - count(gt)
    eqf = jnp.where(eq, 1.0, 0.0)
    tie_rank = _cumsum_tokens(eqf, u_ref, l_ref) - eqf
    sel = jnp.where(gt | (eq & (tie_rank < need)), 1.0, 0.0)
    pos = _cumsum_tokens(sel, u_ref, l_ref)

    @pl.when(e == 0)
    def _():
        run_ref[...] = jnp.zeros_like(run_ref)

    cnt_before = run_ref[0]
    s_incl = run_ref[1] + pos
    pos_ref[0] = pos
    cexcl_ref[0] = cnt_before
    cnt = cnt_before + sel
    run_ref[0] = cnt
    run_ref[1] = s_incl
    sincl_ref[0] = s_incl
    sexcl_ref[0] = s_incl - cnt


def _route_select(aff3, cap):
    e, r, _ = aff3.shape
    u = jnp.asarray(np.triu(np.ones((LANES, LANES), np.float32)), BF16)
    lo = jnp.asarray(np.tril(np.ones((r, r), np.float32), -1), BF16)
    per_e = pl.BlockSpec((1, r, LANES), lambda ei: (ei, 0, 0))
    shared = pl.BlockSpec((1, r, LANES), lambda ei: (0, 0, 0))
    full = lambda a: pl.BlockSpec(a.shape, lambda ei: (0,) * a.ndim)
    return pl.pallas_call(
        functools.partial(_route_select_kernel, cap=cap),
        grid=(e,),
        in_specs=[per_e, full(u), full(lo)],
        out_specs=[per_e, per_e, shared, shared],
        out_shape=[jax.ShapeDtypeStruct((e, r, LANES), F32), jax.ShapeDtypeStruct((e, r, LANES), F32),
                   jax.ShapeDtypeStruct((1, r, LANES), F32), jax.ShapeDtypeStruct((1, r, LANES), F32)],
        scratch_shapes=[pltpu.VMEM((2, r, LANES), F32)],
        compiler_params=_cparams(("arbitrary",)),
        name="route_select",
    )(aff3, u, lo)


SEARCH_CHUNK = 1024
_INT3, _INT2, _INT1, _F32 = "int3", "int2", "int1", "f32"
_NPARTS = {_INT3: 3, _INT2: 2, _INT1: 1, _F32: 3}


def _bf16_parts(x, kind):
    if kind == _INT1:
        return [x]
    if kind == _INT2:
        d1 = jnp.floor(x * (1.0 / 256.0))
        return [x - d1 * 256.0, d1]
    if kind == _INT3:
        d2 = jnp.floor(x * (1.0 / 65536.0))
        r = x - d2 * 65536.0
        d1 = jnp.floor(r * (1.0 / 256.0))
        return [r - d1 * 256.0, d1, d2]
    a1 = x.astype(BF16).astype(F32)
    r1 = x - a1
    a2 = r1.astype(BF16).astype(F32)
    return [a1, a2, r1 - a2]


def _join_parts(parts, kind):
    if kind == _INT1:
        return parts[0]
    if kind == _INT2:
        return parts[0] + 256.0 * parts[1]
    if kind == _INT3:
        return parts[0] + 256.0 * parts[1] + 65536.0 * parts[2]
    return (parts[0] + parts[1]) + parts[2]


def _rank_search_kernel(*refs, kinds, count_kind, chunk):
    npay = len(kinds)
    cnt_ref, pay_refs = refs[0], refs[1:1 + npay]
    tok_ref, out_refs = refs[1 + npay], refs[2 + npay:2 + 2 * npay]
    lhs_ref = refs[-1]
    c = pl.program_id(1)
    r = cnt_ref.shape[1]

    @pl.when(c == 0)
    def _():
        counts = cnt_ref[0]
        ends = jnp.broadcast_to(counts[:, LANES - 1:LANES], counts.shape)
        first_row = lax.broadcasted_iota(jnp.int32, counts.shape, 0) == 0
        row_start = jnp.where(first_row, 0.0, pltpu.roll(ends, 1, 0))
        row = 0
        for val, kind in zip([counts - row_start] + [ref[0] for ref in pay_refs], (count_kind,) + tuple(kinds)):
            for part in _bf16_parts(val, kind):
                lhs_ref[row:row + LANES, :] = part.T.astype(BF16)
                row += LANES

    target = (c * chunk + 1 + lax.broadcasted_iota(jnp.int32, (1, chunk), 1)).astype(F32)
    row_end = cnt_ref[0, :, LANES - 1:LANES]
    before = row_end < target
    rho = jnp.sum(jnp.where(before, 1.0, 0.0), axis=0, keepdims=True).astype(jnp.int32)
    base = jnp.max(jnp.where(before, row_end, 0.0), axis=0, keepdims=True)
    onehot = jnp.where(lax.broadcasted_iota(jnp.int32, (r, chunk), 0) == rho, 1.0, 0.0).astype(BF16)
    fetched = jnp.dot(lhs_ref[...], onehot, preferred_element_type=F32)

    def take(first_part, kind):
        parts = [fetched[(first_part + k) * LANES:(first_part + k + 1) * LANES] for k in range(_NPARTS[kind])]
        return _join_parts(parts, kind)

    lam = jnp.sum(jnp.where(take(0, count_kind) < target - base, 1.0, 0.0), axis=0, keepdims=True).astype(jnp.int32)
    tok_ref[0] = rho * LANES + lam
    at_lane = lax.broadcasted_iota(jnp.int32, (LANES, chunk), 0) == lam
    first_part = _NPARTS[count_kind]
    for out_ref, kind in zip(out_refs, kinds):
        out_ref[0] = jnp.sum(jnp.where(at_lane, take(first_part, kind), 0.0), axis=0, keepdims=True)
        first_part += _NPARTS[kind]


def _rank_search(counts, per_token, nslots, payloads=()):
    g, r, _ = counts.shape
    chunk = min(SEARCH_CHUNK, nslots)
    kinds = tuple(k for _, k in payloads)
    count_kind = _INT1 if per_token * LANES < 256 else _INT2
    assert per_token * LANES < 65536
    nparts = _NPARTS[count_kind] + sum(_NPARTS[k] for k in kinds)

    def in_spec(a):
        if a.shape[0] == 1:
            return pl.BlockSpec((1, r, LANES), lambda gi, ci: (0, 0, 0))
        return pl.BlockSpec((1, r, LANES), lambda gi, ci: (gi, 0, 0))

    out_spec = pl.BlockSpec((1, 1, chunk), lambda gi, ci: (gi, 0, ci))
    outs = pl.pallas_call(
        functools.partial(_rank_search_kernel, kinds=kinds, count_kind=count_kind, chunk=chunk),
        grid=(g, nslots // chunk),
        in_specs=[in_spec(counts)] + [in_spec(a) for a, _ in payloads],
        out_specs=[out_spec] * (1 + len(kinds)),
        out_shape=[jax.ShapeDtypeStruct((g, 1, nslots), jnp.int32)]
        + [jax.ShapeDtypeStruct((g, 1, nslots), F32)] * len(kinds),
        scratch_shapes=[pltpu.VMEM((nparts * LANES, r), BF16)],
        compiler_params=_cparams(("parallel", "arbitrary")),
        name="rank_search",
    )(counts, *[a for a, _ in payloads])
    return outs[0], outs[1:]


def _sc_scatter_rows(rows, idx):
    m, w = rows.shape
    nchunks = m // GATHER_ROWS
    per_worker = nchunks // SC_WORKERS
    assert per_worker * SC_WORKERS * GATHER_ROWS == m and per_worker % 8 == 0
    mesh = plsc.VectorSubcoreMesh(core_axis_name="c", subcore_axis_name="s")

    @functools.partial(
        pl.kernel, mesh=mesh, out_type=jax.ShapeDtypeStruct((m, w), rows.dtype),
        scratch_types=[pltpu.VMEM((per_worker, GATHER_ROWS), jnp.int32),
                       pltpu.VMEM((GATHER_ROWS, w), rows.dtype),
                       pltpu.SemaphoreType.DMA])
    def scatter(rows_hbm, idx_hbm, out_hbm, idx_v, rows_v, sem):
        wid = lax.axis_index("s") * SC_CORES + lax.axis_index("c")
        first = wid * per_worker
        pltpu.sync_copy(idx_hbm.at[pl.ds(first, per_worker)], idx_v)

        @pl.loop(0, per_worker)
        def _(c):
            pltpu.sync_copy(rows_hbm.at[pl.ds((first + c) * GATHER_ROWS, GATHER_ROWS)], rows_v)
            pltpu.async_copy(rows_v, out_hbm.at[idx_v.at[c]], sem).wait()

    return scatter(rows, idx.reshape(nchunks, GATHER_ROWS))


def _moe_kernel(x_ref, gate_ref, wg_ref, wu_ref, wd_ref, o_ref):
    x = _unpack_rows(x_ref[0])
    y = jnp.zeros((x.shape[0], D_MODEL), F32)
    for c in range(D_EXPERT // FC):
        sl = slice(c * FC, (c + 1) * FC)
        a = jnp.dot(x, wg_ref[0, :, sl], preferred_element_type=F32)
        u = jnp.dot(x, wu_ref[0, :, sl], preferred_element_type=F32)
        hid = (a * jax.nn.sigmoid(a) * u).astype(BF16)
        y = y + jnp.dot(hid, wd_ref[0, sl, :], preferred_element_type=F32)
    gate = jnp.broadcast_to(gate_ref[0], (LANES, x.shape[0])).T
    o_ref[0] = _pack_rows(y * jnp.tile(gate, (1, D_MODEL // LANES)))


def _moe(xe, gate3, wg, wu, wd):
    e, cap, _ = xe.shape
    tc = min(TC, cap)
    tokb = lambda w: pl.BlockSpec((1, tc, w), lambda ei, i: (ei, i, 0))
    wspec = lambda a: pl.BlockSpec((1,) + a.shape[1:], lambda ei, i: (ei, 0, 0))
    return pl.pallas_call(
        _moe_kernel,
        grid=(e, cap // tc),
        in_specs=[tokb(HALF), pl.BlockSpec((1, 1, tc), lambda ei, i: (ei, 0, i)), wspec(wg), wspec(wu), wspec(wd)],
        out_specs=tokb(HALF),
        out_shape=jax.ShapeDtypeStruct((e, cap, HALF), U32),
        compiler_params=_cparams(("parallel", "arbitrary")),
        name="moe_ffn",
    )(xe, gate3, wg, wu, wd)


_FIRST, _LAST, _ACTIVE = 1, 2, 4


def _combine_kernel(tile_ref, blk_ref, flag_ref, tok_ref, yg_ref, x1_ref, g_ref, o_ref, acc_ref):
    s = pl.program_id(0)
    flag = flag_ref[s]

    @pl.when((flag & _FIRST) != 0)
    def _():
        acc_ref[...] = jnp.zeros_like(acc_ref)

    @pl.when((flag & _ACTIVE) != 0)
    def _():
        tok_row = tile_ref[s] * TS + lax.broadcasted_iota(jnp.int32, (TS, SB), 0)
        onehot = jnp.where(tok_ref[0] == tok_row, 1.0, 0.0).astype(BF16)
        acc_ref[...] += jnp.dot(onehot, _unpack_rows(yg_ref[...]), preferred_element_type=F32)

    @pl.when((flag & _LAST) != 0)
    def _():
        y = x1_ref[...] + acc_ref[...]
        ms = jnp.mean(y * y, axis=-1, keepdims=True)
        o_ref[...] = y * lax.rsqrt(ms + RMS_EPS) * g_ref[...]


def _combine_schedule(bounds, nslots):
    nt, nblk = bounds.shape[0] - 1, nslots // SB
    steps = nt + nblk
    lo, hi = bounds[:-1], bounds[1:]
    b_lo = jnp.minimum(lo // SB, nblk - 1)
    b_hi = jnp.where(hi > lo, (hi - 1) // SB, b_lo)
    nst = b_hi - b_lo + 1
    cum = jnp.cumsum(nst)
    start = cum - nst
    s = jnp.arange(steps, dtype=jnp.int32)
    tile = jnp.minimum(jnp.searchsorted(cum, s, side="right").astype(jnp.int32), nt - 1)
    active = s < cum[-1]
    blk = jnp.where(active, b_lo[tile] + s - start[tile], b_hi[nt - 1])
    first = active & (s == start[tile])
    last = active & (s == cum[tile] - 1)
    flag = first * _FIRST + last * _LAST + active * _ACTIVE
    return tile, blk.astype(jnp.int32), flag.astype(jnp.int32)


def _combine(tile, blk, flag, tok3, yg, x1, g):
    n = x1.shape[0]
    steps = tile.shape[0]
    grid_spec = pltpu.PrefetchScalarGridSpec(
        num_scalar_prefetch=3,
        grid=(steps,),
        in_specs=[pl.BlockSpec((1, 1, SB), lambda s, t, b, f: (b[s], 0, 0)),
                  pl.BlockSpec((SB, HALF), lambda s, t, b, f: (b[s], 0)),
                  pl.BlockSpec((TS, D_MODEL), lambda s, t, b, f: (t[s], 0)),
                  pl.BlockSpec(g.shape, lambda s, t, b, f: (0, 0))],
        out_specs=pl.BlockSpec((TS, D_MODEL), lambda s, t, b, f: (t[s], 0)),
        scratch_shapes=[pltpu.VMEM((TS, D_MODEL), F32)],
    )
    return pl.pallas_call(
        _combine_kernel,
        grid_spec=grid_spec,
        out_shape=jax.ShapeDtypeStruct((n, D_MODEL), F32),
        compiler_params=_cparams(("arbitrary",)),
        name="combine_norm",
    )(tile, blk, flag, tok3, yg, x1, g)


def _group_forward(x, p):
    b, seq, _ = x.shape
    n = b * seq
    cap = CAP_FACTOR * n // N_EXPERTS
    x2 = x.reshape(n, D_MODEL)
    cos_t, sa_t, sb_t = p["rot"]
    qa, ka, va, qb, kbt, vb = _inproj(x2, p["g_attn"], p["w_in"], cos_t[:seq], sa_t[:seq], sb_t[:seq], seq)
    r3 = lambda a: a.reshape(b, seq, a.shape[-1])
    oa = _window(r3(qa), r3(ka), r3(va), p["sink"], p["g_out_a"])
    ob = _natten(r3(qb), kbt, r3(vb), p["na_tabs"], p["g_out_b"])
    x1, h, aff3 = _outproj(x2, oa.reshape(n, DQA), ob.reshape(n, DB), p["w_out"], p["g_ffn"], p["w_router_t"])

    nslots = N_EXPERTS * cap
    pos, c_excl, s_incl, s_excl = _route_select(aff3, cap)
    idx, (gate, s_at, c_at) = _rank_search(pos, 1, cap, [(aff3, _F32), (s_excl, _INT3), (c_excl, _INT1)])
    dest = (s_at + c_at).astype(jnp.int32).reshape(nslots)
    xe = _sc_gather_rows(h, idx.reshape(nslots)).reshape(N_EXPERTS, cap, HALF)
    ye = _moe(xe, gate, p["w_gate"], p["w_up"], p["w_down"])
    yg = _sc_scatter_rows(ye.reshape(nslots, HALF), dest)

    tok_sorted, _ = _rank_search(s_incl, N_EXPERTS, nslots)
    tile_end = s_incl.reshape(n)[TS - 1::TS].astype(jnp.int32)
    bounds = jnp.concatenate([jnp.zeros((1,), jnp.int32), tile_end])
    tile, blk, flag = _combine_schedule(bounds, nslots)
    y = _combine(tile, blk, flag, tok_sorted.reshape(-1, 1, SB), yg, x1, p["g_final"])
    return y.reshape(b, seq, D_MODEL)


def kernel(x_prompt, x_sample, g_attn, w_in, g_out_a, g_out_b, sink_a, rpb_b, w_out, g_ffn, w_router,
           w_gate, w_up, w_down, g_final):
    assert g_attn.shape[0] == 1, "single trunk layer"
    wr = w_router[0].T
    wr_hi = wr.astype(BF16)
    wr_lo = (wr - wr_hi.astype(F32)).astype(BF16)
    p = {
        "g_attn": g_attn[0][None, :], "w_in": w_in[0].astype(BF16),
        "g_out_a": g_out_a[0][None, :], "g_out_b": g_out_b[0][None, :],
        "sink": sink_a[0], "na_tabs": _natten_tables(rpb_b[0], 4 * NA_ROWS), "w_out": w_out[0].astype(BF16),
        "g_ffn": g_ffn[0][None, :], "w_router_t": jnp.concatenate([wr_hi, wr_lo], axis=0),
        "w_gate": w_gate[0].astype(BF16), "w_up": w_up[0].astype(BF16), "w_down": w_down[0].astype(BF16),
        "g_final": g_final[None, :],
        "rot": _rotary_tables(max(x_prompt.shape[1], x_sample.shape[1])),
    }
    return (_group_forward(x_prompt, p), _group_forward(x_sample, p))
```

```python
import functools

import jax
import jax.numpy as jnp
import numpy as np
from jax import lax
from jax.experimental import pallas as pl
from jax.experimental.pallas import tpu as pltpu
from jax.experimental.pallas import tpu_sc as plsc

D_MODEL = 1024
HEAD_DIM = 64
HQ_A = 8
HKV_A = 2
H_B = 8
GRP = HQ_A // HKV_A
DQA = HQ_A * HEAD_DIM
DKA = HKV_A * HEAD_DIM
DB = H_B * HEAD_DIM
D_IN = DQA + 2 * DKA + 3 * DB
WINDOW = 128
BLK = 128
ROPE_THETA = 500000.0
ROT_DIM = HEAD_DIM // 4
GRID_W = 64
NA_ROWS = 8
NA_COLS = 16
NA_QROWS = 2
NA_KROWS = NA_ROWS + NA_QROWS
NA_KEYS = NA_KROWS * GRID_W
NA_Q = NA_QROWS * GRID_W
N_EXPERTS = 16
CAP_FACTOR = 2
D_EXPERT = 2 * D_MODEL
RMS_EPS = 1e-6
NEG_INF = -1e30
SCALE = HEAD_DIM ** -0.5

LANES = 128
TM = 1024
TSUB = 512
TQ = 1024
TC = 1024
FC = 512
TS = 512
SB = 1024
VMEM_LIMIT = 56 * 1024 * 1024

BF16 = jnp.bfloat16
F32 = jnp.float32
_NT = (((1,), (1,)), ((), ()))


def _cparams(sem):
    return pltpu.CompilerParams(dimension_semantics=sem, vmem_limit_bytes=VMEM_LIMIT)


HALF = D_MODEL // 2
U32 = jnp.uint32


def _pack_rows(v):
    r = v.astype(BF16).astype(F32)
    hi = pltpu.bitcast(r[:, :HALF], U32)
    lo = pltpu.bitcast(r[:, HALF:], U32)
    return (hi & jnp.uint32(0xFFFF0000)) | (lo >> 16)


def _unpack_rows(w):
    hi = pltpu.bitcast(w & jnp.uint32(0xFFFF0000), F32)
    lo = pltpu.bitcast(w << 16, F32)
    return jnp.concatenate([hi, lo], axis=1).astype(BF16)


SC_CORES = 2
SC_SUBCORES = 16
SC_WORKERS = SC_CORES * SC_SUBCORES
GATHER_ROWS = 64


def _sc_gather_rows(table, idx):
    m, w = idx.shape[0], table.shape[1]
    nchunks = m // GATHER_ROWS
    per_worker = nchunks // SC_WORKERS
    assert per_worker * SC_WORKERS * GATHER_ROWS == m and per_worker % 8 == 0
    mesh = plsc.VectorSubcoreMesh(core_axis_name="c", subcore_axis_name="s")

    @functools.partial(
        pl.kernel, mesh=mesh, out_type=jax.ShapeDtypeStruct((m, w), table.dtype),
        scratch_types=[pltpu.VMEM((per_worker, GATHER_ROWS), jnp.int32),
                       pltpu.VMEM((GATHER_ROWS, w), table.dtype),
                       pltpu.SemaphoreType.DMA])
    def gather(tab_hbm, idx_hbm, out_hbm, idx_v, rows_v, sem):
        wid = lax.axis_index("s") * SC_CORES + lax.axis_index("c")
        first = wid * per_worker
        pltpu.sync_copy(idx_hbm.at[pl.ds(first, per_worker)], idx_v)

        @pl.loop(0, per_worker)
        def _(c):
            pltpu.async_copy(tab_hbm.at[idx_v.at[c]], rows_v, sem).wait()
            pltpu.sync_copy(rows_v, out_hbm.at[pl.ds((first + c) * GATHER_ROWS, GATHER_ROWS)])

    return gather(table, idx.reshape(nchunks, GATHER_ROWS))


def _inproj_kernel(x_ref, g_ref, w_ref, wkt_ref, cos_ref, sa_ref, sb_ref,
                   qa_ref, kat_ref, va_ref, qb_ref, kbt_ref, vb_ref):
    def normed(rows):
        x = x_ref[rows, :]
        ms = jnp.mean(x * x, axis=-1, keepdims=True)
        return (x * lax.rsqrt(ms + RMS_EPS) * g_ref[...]).astype(BF16)

    def project(rows, hn):
        c, sa, sb = cos_ref[rows, :], sa_ref[rows, :], sb_ref[rows, :]

        def proj(lo, n):
            return jnp.dot(hn, w_ref[:, lo:lo + n], preferred_element_type=F32)

        def rot(blk):
            return blk * c + pltpu.roll(blk, LANES - ROT_DIM // 2, 1) * sa + pltpu.roll(blk, ROT_DIM // 2, 1) * sb

        for j in range(DQA // LANES):
            qa_ref[rows, j * LANES:(j + 1) * LANES] = (rot(proj(j * LANES, LANES)) * SCALE).astype(BF16)
        kat_ref[:, rows] = rot(proj(DQA, DKA)).T.astype(BF16)
        va_ref[rows, :] = proj(DQA + DKA, DKA).astype(BF16)
        qb_ref[rows, :] = (proj(DQA + 2 * DKA, DB) * SCALE).astype(BF16)
        kbt_ref[:, rows] = lax.dot_general(wkt_ref[...], hn, _NT, preferred_element_type=F32).astype(BF16)
        vb_ref[rows, :] = proj(DQA + 2 * DKA + 2 * DB, DB).astype(BF16)

    subs = [slice(r0, r0 + TSUB) for r0 in range(0, TM, TSUB)]
    hn = normed(subs[0])
    for k, rows in enumerate(subs):
        nxt = normed(subs[k + 1]) if k + 1 < len(subs) else None
        project(rows, hn)
        hn = nxt


def _inproj(x2, g, w_bf, cos_t, sa_t, sb_t, seq):
    n = x2.shape[0]
    per_seq = seq // TM
    tok = lambda w: pl.BlockSpec((TM, w), lambda i: (i, 0))
    full = lambda a: pl.BlockSpec(a.shape, lambda i: (0,) * a.ndim)
    pos = pl.BlockSpec((TM, LANES), lambda i: (i % per_seq, 0))
    kb_lo = DQA + 2 * DKA + DB
    wkt = w_bf[:, kb_lo:kb_lo + DB].T
    feat = lambda w: pl.BlockSpec((w, TM), lambda i: (0, i))
    tshape = lambda w: jax.ShapeDtypeStruct((n, w), BF16)
    fshape = lambda w: jax.ShapeDtypeStruct((w, n), BF16)
    return pl.pallas_call(
        _inproj_kernel,
        grid=(n // TM,),
        in_specs=[tok(D_MODEL), full(g), full(w_bf), full(wkt), pos, pos, pos],
        out_specs=[tok(DQA), feat(DKA), tok(DKA), tok(DB), feat(DB), tok(DB)],
        out_shape=[tshape(DQA), fshape(DKA), tshape(DKA), tshape(DB), fshape(DB), tshape(DB)],
        compiler_params=_cparams(("parallel",)),
        name="inproj",
    )(x2, g, w_bf, wkt, cos_t, sa_t, sb_t)


def _rotary_tables(seq):
    half = ROT_DIM // 2
    inv_freq = jnp.float32(ROPE_THETA) ** (-(jnp.arange(half, dtype=F32) * 2.0) / ROT_DIM)
    ang = jnp.arange(seq, dtype=F32)[:, None] * inv_freq[None, :]
    cos, sin = jnp.cos(ang), jnp.sin(ang)
    ones = jnp.ones((seq, HEAD_DIM - ROT_DIM), F32)
    zeros = jnp.zeros((seq, HEAD_DIM - ROT_DIM), F32)
    zh = jnp.zeros((seq, half), F32)
    rep = LANES // HEAD_DIM
    cos_t = jnp.tile(jnp.concatenate([cos, cos, ones], axis=1), (1, rep))
    sa_t = jnp.tile(jnp.concatenate([-sin, zh, zeros], axis=1), (1, rep))
    sb_t = jnp.tile(jnp.concatenate([zh, sin, zeros], axis=1), (1, rep))
    return cos_t, sa_t, sb_t


SM_ROWS = 32


def _sm_chunks(shape):
    nh, nq, nk = shape
    chunks = [(h, slice(r0, r0 + SM_ROWS)) for h in range(nh) for r0 in range(0, nq, SM_ROWS)]
    return chunks, [slice(l0, l0 + LANES) for l0 in range(0, nk, LANES)]


def _softmax_max(score, shape, pm_ref, mb_ref, floor=None):
    chunks, lanes = _sm_chunks(shape)
    for h, rc in chunks:
        pm_ref[h, rc, :] = functools.reduce(jnp.maximum, [score(h, rc, ls) for ls in lanes])
    m = jnp.max(pm_ref[...], axis=-1, keepdims=True)
    if floor is not None:
        m = jnp.maximum(m, floor)
    mb_ref[...] = jnp.broadcast_to(m, mb_ref.shape)
    return m


def _softmax_exp(score, p_ref, mb_ref, ps_ref):
    chunks, lanes = _sm_chunks(p_ref.shape)
    for h, rc in chunks:
        mb = mb_ref[h, rc, :]
        total = None
        for ls in lanes:
            p = jnp.exp(score(h, rc, ls) - mb)
            p_ref[h, rc, ls] = p.astype(BF16)
            total = p if total is None else total + p
        ps_ref[h, rc, :] = total
    return jnp.sum(ps_ref[...], axis=-1, keepdims=True)


def _window_kernel(q_ref, kp_ref, kc_ref, kn_ref, vp_ref, vc_ref, vn_ref, sink_ref, g_ref, o_ref,
                   kt_ref, v_ref, s_ref, p_ref, pm_ref, mb_ref, ps_ref, *, nb):
    i = pl.program_id(1)
    nsub = TQ // BLK
    kt_ref[:, 0:BLK] = kp_ref[...]
    kt_ref[:, BLK:BLK + TQ] = kc_ref[...]
    kt_ref[:, BLK + TQ:TQ + 2 * BLK] = kn_ref[...]
    v_ref[0:BLK] = vp_ref[0]
    v_ref[BLK:BLK + TQ] = vc_ref[0]
    v_ref[BLK + TQ:TQ + 2 * BLK] = vn_ref[0]
    qi = lax.broadcasted_iota(jnp.int32, (BLK, 3 * BLK), 0)
    kj = lax.broadcasted_iota(jnp.int32, (BLK, 3 * BLK), 1)
    band = jnp.abs(kj - BLK - qi) <= WINDOW
    sink = sink_ref[...]

    def scores(j):
        n = i * nsub + j
        k_lo = jnp.where(n >= 1, 0, BLK)
        k_hi = jnp.where(n <= nb - 2, 3 * BLK, 2 * BLK)
        bias = jnp.where(band & (kj >= k_lo) & (kj < k_hi), 0.0, NEG_INF)
        for h in range(HKV_A):
            kh = kt_ref[h * HEAD_DIM:(h + 1) * HEAD_DIM, j * BLK:(j + 3) * BLK]
            qs = jnp.concatenate([q_ref[0, j * BLK:(j + 1) * BLK, hd * HEAD_DIM:(hd + 1) * HEAD_DIM]
                                  for hd in range(h * GRP, (h + 1) * GRP)], axis=0)
            s = jnp.dot(qs, kh, preferred_element_type=F32)
            s_ref[j % 3, h * GRP:(h + 1) * GRP] = s.reshape(GRP, BLK, 3 * BLK) + bias[None]

    def score_fn(j):
        sb = s_ref.at[j % 3]
        return lambda hd, rc, ls: sb[hd, rc, ls]

    def values(j, denom):
        outs = []
        for h in range(HKV_A):
            vh = v_ref[j * BLK:(j + 3) * BLK, h * HEAD_DIM:(h + 1) * HEAD_DIM]
            ph = p_ref[j % 2, h * GRP:(h + 1) * GRP].reshape(GRP * BLK, 3 * BLK)
            o = jnp.dot(ph, vh, preferred_element_type=F32)
            outs += [o[g * BLK:(g + 1) * BLK] / denom[h * GRP + g] for g in range(GRP)]
        ob = jnp.concatenate(outs, axis=1)
        ms = jnp.mean(ob * ob, axis=-1, keepdims=True)
        o_ref[0, j * BLK:(j + 1) * BLK, :] = (ob * lax.rsqrt(ms + RMS_EPS) * g_ref[...]).astype(BF16)

    maxima, denoms = {}, {}
    for t in range(nsub + 3):
        if t < nsub:
            scores(t)
        if 0 <= t - 2 < nsub:
            j = t - 2
            rowsum = _softmax_exp(score_fn(j), p_ref.at[j % 2], mb_ref.at[j % 2], ps_ref.at[j % 2])
            denoms[j] = rowsum + jnp.exp(sink - maxima.pop(j))
        if 0 <= t - 3 < nsub:
            values(t - 3, denoms.pop(t - 3))
        if 0 <= t - 1 < nsub:
            j = t - 1
            maxima[j] = _softmax_max(score_fn(j), s_ref.shape[1:], pm_ref.at[j % 2], mb_ref.at[j % 2], floor=sink)


def _window(qa, kat, va, sink, g):
    b, seq, _ = qa.shape
    nb = seq // BLK
    r = TQ // BLK
    nt = seq // TQ
    before = lambda i: jnp.maximum(i * r - 1, 0)
    after = lambda i: jnp.minimum((i + 1) * r, nb - 1)
    cur = lambda w: pl.BlockSpec((1, TQ, w), lambda bi, i: (bi, i, 0))
    prev = pl.BlockSpec((1, BLK, DKA), lambda bi, i: (bi, before(i), 0))
    nxt = pl.BlockSpec((1, BLK, DKA), lambda bi, i: (bi, after(i), 0))
    cur_t = pl.BlockSpec((DKA, TQ), lambda bi, i: (0, bi * nt + i))
    prev_t = pl.BlockSpec((DKA, BLK), lambda bi, i: (0, bi * nb + before(i)))
    nxt_t = pl.BlockSpec((DKA, BLK), lambda bi, i: (0, bi * nb + after(i)))
    sink3 = sink.reshape(HQ_A, 1, 1)
    return pl.pallas_call(
        functools.partial(_window_kernel, nb=nb),
        grid=(b, nt),
        in_specs=[cur(DQA), prev_t, cur_t, nxt_t, prev, cur(DKA), nxt,
                  pl.BlockSpec(sink3.shape, lambda bi, i: (0, 0, 0)), pl.BlockSpec(g.shape, lambda bi, i: (0, 0))],
        out_specs=cur(DQA),
        out_shape=jax.ShapeDtypeStruct((b, seq, DQA), BF16),
        scratch_shapes=[pltpu.VMEM((DKA, TQ + 2 * BLK), BF16), pltpu.VMEM((TQ + 2 * BLK, DKA), BF16),
                        pltpu.VMEM((3, HQ_A, BLK, 3 * BLK), F32), pltpu.VMEM((2, HQ_A, BLK, 3 * BLK), BF16),
                        ] + [pltpu.VMEM((2, HQ_A, BLK, LANES), F32)] * 3,
        compiler_params=_cparams(("parallel", "parallel")),
        name="window_attn",
    )(qa, kat, kat, kat, va, va, va, sink3, g)


NA_RB = 8
NA_TILE = NA_RB * NA_Q
NA_TILE_ROWS = NA_RB * NA_QROWS
NA_HALO_ROWS = NA_ROWS // 2
NA_HALO = NA_HALO_ROWS * GRID_W


def _natten_kernel(q_ref, kp_ref, kc_ref, kn_ref, vp_ref, vc_ref, vn_ref, tab_ref, g_ref, o_ref,
                   kt_ref, v_ref, s_ref, p_ref, pm_ref, mb_ref, ps_ref, *, rows):
    i = pl.program_id(1)
    nrb = rows // NA_QROWS
    kt_ref[:, 0:NA_HALO] = kp_ref[...]
    kt_ref[:, NA_HALO:NA_HALO + NA_TILE] = kc_ref[...]
    kt_ref[:, NA_HALO + NA_TILE:2 * NA_HALO + NA_TILE] = kn_ref[...]
    v_ref[0:NA_HALO] = vp_ref[0]
    v_ref[NA_HALO:NA_HALO + NA_TILE] = vc_ref[0]
    v_ref[NA_HALO + NA_TILE:2 * NA_HALO + NA_TILE] = vn_ref[0]

    def block(r):
        rb = i * NA_RB + r
        kstart = jnp.clip(NA_QROWS * rb - NA_ROWS // 2, 0, rows - NA_KROWS)
        off = (kstart - (i * NA_TILE_ROWS - NA_HALO_ROWS)) * GRID_W
        variant = jnp.where(rb < 2, rb, jnp.where(rb >= nrb - 2, rb - (nrb - 5), 2))
        return pl.ds(pl.multiple_of(off, LANES), NA_KEYS), variant

    def scores(r):
        keys, variant = block(r)
        for h in range(H_B):
            sl = slice(h * HEAD_DIM, (h + 1) * HEAD_DIM)
            s = jnp.dot(q_ref[0, r * NA_Q:(r + 1) * NA_Q, sl], kt_ref[sl, keys], preferred_element_type=F32)
            s_ref[r % 3, h] = s + tab_ref[variant, h]

    def score_fn(r):
        sb = s_ref.at[r % 3]
        return lambda h, rc, ls: sb[h, rc, ls]

    def values(r, denom):
        keys, _ = block(r)
        outs = []
        for h in range(H_B):
            sl = slice(h * HEAD_DIM, (h + 1) * HEAD_DIM)
            outs.append(jnp.dot(p_ref[r % 2, h], v_ref[keys, sl], preferred_element_type=F32) / denom[h])
        ob = jnp.concatenate(outs, axis=1)
        ms = jnp.mean(ob * ob, axis=-1, keepdims=True)
        o_ref[0, r * NA_Q:(r + 1) * NA_Q, :] = (ob * lax.rsqrt(ms + RMS_EPS) * g_ref[...]).astype(BF16)

    denoms = {}
    for t in range(NA_RB + 3):
        if t < NA_RB:
            scores(t)
        if 0 <= t - 2 < NA_RB:
            r = t - 2
            denoms[r] = _softmax_exp(score_fn(r), p_ref.at[r % 2], mb_ref.at[r % 2], ps_ref.at[r % 2])
        if 0 <= t - 3 < NA_RB:
            values(t - 3, denoms.pop(t - 3))
        if 0 <= t - 1 < NA_RB:
            r = t - 1
            _softmax_max(score_fn(r), s_ref.shape[1:], pm_ref.at[r % 2], mb_ref.at[r % 2])


def _natten_variant_rowblocks(nrb):
    return (0, 1, 2, nrb - 2, nrb - 1)


def _natten_tables(rpb, rows):
    nrb = rows // NA_QROWS
    qc = np.arange(GRID_W)
    kc = np.arange(GRID_W)
    cs = np.clip(qc - NA_COLS // 2, 0, GRID_W - NA_COLS)
    col_ok = (kc[None, :] >= cs[:, None]) & (kc[None, :] < cs[:, None] + NA_COLS)
    cidx = np.clip(kc[None, :] - qc[:, None] + NA_COLS - 1, 0, 2 * NA_COLS - 2)
    tabs = []
    for rb in _natten_variant_rowblocks(nrb):
        r0 = rb * NA_QROWS
        qrows = r0 + np.arange(NA_QROWS)
        rs = np.clip(qrows - NA_ROWS // 2, 0, rows - NA_ROWS)
        kstart = int(np.clip(r0 - NA_ROWS // 2, 0, rows - NA_KROWS))
        krows = kstart + np.arange(NA_KROWS)
        row_ok = (krows[None, :] >= rs[:, None]) & (krows[None, :] < rs[:, None] + NA_ROWS)
        ridx = np.clip(krows[None, :] - qrows[:, None] + NA_ROWS - 1, 0, 2 * NA_ROWS - 2)
        valid = row_ok[:, None, :, None] & col_ok[None, :, None, :]
        r1h = (ridx[:, :, None] == np.arange(2 * NA_ROWS - 1)).astype(np.float32)
        c1h = (cidx[:, :, None] == np.arange(2 * NA_COLS - 1)).astype(np.float32)
        bias = jnp.einsum("rka,hab,qcb->hrqkc", r1h, rpb.astype(F32), c1h, precision=lax.Precision.HIGHEST)
        bias = jnp.where(valid[None], bias, NEG_INF)
        tabs.append(bias.reshape(H_B, NA_Q, NA_KEYS))
    return jnp.stack(tabs)


def _natten(qb, kbt, vb, tabs, g):
    b, seq, _ = qb.shape
    rows = seq // GRID_W
    nrb = rows // NA_QROWS
    nt = seq // NA_TILE
    assert rows >= NA_KROWS and nrb >= 5 and nt * NA_TILE == seq
    hpt = NA_TILE // NA_HALO
    before = lambda i: jnp.maximum(i * hpt - 1, 0)
    after = lambda i: jnp.minimum((i + 1) * hpt, nt * hpt - 1)
    cur = pl.BlockSpec((1, NA_TILE, DB), lambda bi, i: (bi, i, 0))
    prev = pl.BlockSpec((1, NA_HALO, DB), lambda bi, i: (bi, before(i), 0))
    nxt = pl.BlockSpec((1, NA_HALO, DB), lambda bi, i: (bi, after(i), 0))
    cur_t = pl.BlockSpec((DB, NA_TILE), lambda bi, i: (0, bi * nt + i))
    prev_t = pl.BlockSpec((DB, NA_HALO), lambda bi, i: (0, bi * nt * hpt + before(i)))
    nxt_t = pl.BlockSpec((DB, NA_HALO), lambda bi, i: (0, bi * nt * hpt + after(i)))
    return pl.pallas_call(
        functools.partial(_natten_kernel, rows=rows),
        grid=(b, nt),
        in_specs=[cur, prev_t, cur_t, nxt_t, prev, cur, nxt,
                  pl.BlockSpec(tabs.shape, lambda bi, i: (0, 0, 0, 0)),
                  pl.BlockSpec(g.shape, lambda bi, i: (0, 0))],
        out_specs=cur,
        out_shape=jax.ShapeDtypeStruct((b, seq, DB), BF16),
        scratch_shapes=[pltpu.VMEM((DB, NA_TILE + 2 * NA_HALO), BF16), pltpu.VMEM((NA_TILE + 2 * NA_HALO, DB), BF16),
                        pltpu.VMEM((3, H_B, NA_Q, NA_KEYS), F32), pltpu.VMEM((2, H_B, NA_Q, NA_KEYS), BF16),
                        ] + [pltpu.VMEM((2, H_B, NA_Q, LANES), F32)] * 3,
        compiler_params=_cparams(("parallel", "arbitrary")),
        name="natten",
    )(qb, kbt, kbt, kbt, vb, vb, vb, tabs, g)


def _outproj_kernel(x_ref, oa_ref, ob_ref, wo_ref, g_ref, wr_ref, x1_ref, h_ref, aff_ref):
    def residual(rows):
        return (x_ref[rows, :]
                + jnp.dot(oa_ref[rows, :], wo_ref[0:DQA, :], preferred_element_type=F32)
                + jnp.dot(ob_ref[rows, :], wo_ref[DQA:DQA + DB, :], preferred_element_type=F32))

    def route(rows, x1):
        x1_ref[rows, :] = x1
        ms = jnp.mean(x1 * x1, axis=-1, keepdims=True)
        hf = x1 * lax.rsqrt(ms + RMS_EPS) * g_ref[...]
        h_hi = hf.astype(BF16)
        h_lo = (hf - h_hi.astype(F32)).astype(BF16)
        h_ref[rows, :] = _pack_rows(hf)
        l_hi = lax.dot_general(wr_ref[...], h_hi, _NT, preferred_element_type=F32)
        l_lo = lax.dot_general(wr_ref[0:N_EXPERTS, :], h_lo, _NT, preferred_element_type=F32)
        logits = l_hi[0:N_EXPERTS] + l_hi[N_EXPERTS:2 * N_EXPERTS] + l_lo
        m = jnp.max(logits, axis=0, keepdims=True)
        e = jnp.exp(logits - m)
        aff = e / jnp.sum(e, axis=0, keepdims=True)
        for k in range(TSUB // LANES):
            aff_ref[:, rows.start // LANES + k, :] = aff[:, k * LANES:(k + 1) * LANES]

    subs = [slice(r0, r0 + TSUB) for r0 in range(0, TM, TSUB)]
    x1 = residual(subs[0])
    for k, rows in enumerate(subs):
        nxt = residual(subs[k + 1]) if k + 1 < len(subs) else None
        route(rows, x1)
        x1 = nxt


def _outproj(x2, oa, ob, wo_bf, g, wr_t):
    n = x2.shape[0]
    tok = lambda w: pl.BlockSpec((TM, w), lambda i: (i, 0))
    full = lambda a: pl.BlockSpec(a.shape, lambda i: (0,) * a.ndim)
    return pl.pallas_call(
        _outproj_kernel,
        grid=(n // TM,),
        in_specs=[tok(D_MODEL), tok(DQA), tok(DB), full(wo_bf), full(g), full(wr_t)],
        out_specs=[tok(D_MODEL), tok(HALF), pl.BlockSpec((N_EXPERTS, TM // LANES, LANES), lambda i: (0, i, 0))],
        out_shape=[jax.ShapeDtypeStruct((n, D_MODEL), F32), jax.ShapeDtypeStruct((n, HALF), U32),
                   jax.ShapeDtypeStruct((N_EXPERTS, n // LANES, LANES), F32)],
        compiler_params=_cparams(("parallel",)),
        name="outproj_router",
    )(x2, oa, ob, wo_bf, g, wr_t)


def _cumsum_tokens(x, u_ref, l_ref):
    within = jnp.dot(x.astype(BF16), u_ref[...], preferred_element_type=F32)
    rowtot = jnp.broadcast_to(within[:, LANES - 1:LANES], within.shape).astype(BF16)
    return within + jnp.dot(l_ref[...], rowtot, preferred_element_type=F32)


def _route_select_kernel(aff_ref, u_ref, l_ref, pos_ref, cexcl_ref, sincl_ref, sexcl_ref, run_ref, *, cap):
    e = pl.program_id(0)
    bits = pltpu.bitcast(aff_ref[0], jnp.int32)

    def count(mask):
        c = jnp.sum(jnp.where(mask, 1.0, 0.0), axis=0, keepdims=True)
        return jnp.sum(c, axis=1, keepdims=True)

    def step(i, t):
        cand = t | jnp.left_shift(jnp.int32(1), 30 - i)
        return jnp.where(count(bits >= cand) >= cap, cand, t)

    thr = lax.fori_loop(0, 31, step, jnp.zeros((1, 1), jnp.int32))
    gt = bits > thr
    eq = bits == thr
    need = cap - count(gt)
    eqf = jnp.where(eq, 1.0, 0.0)
    tie_rank = _cumsum_tokens(eqf, u_ref, l_ref) - eqf
    sel = jnp.where(gt | (eq & (tie_rank < need)), 1.0, 0.0)
    pos = _cumsum_tokens(sel, u_ref, l_ref)

    @pl.when(e == 0)
    def _():
        run_ref[...] = jnp.zeros_like(run_ref)

    cnt_before = run_ref[0]
    s_incl = run_ref[1] + pos
    pos_ref[0] = pos
    cexcl_ref[0] = cnt_before
    cnt = cnt_before + sel
    run_ref[0] = cnt
    run_ref[1] = s_incl
    sincl_ref[0] = s_incl
    sexcl_ref[0] = s_incl - cnt


def _route_select(aff3, cap):
    e, r, _ = aff3.shape
    u = jnp.asarray(np.triu(np.ones((LANES, LANES), np.float32)), BF16)
    lo = jnp.asarray(np.tril(np.ones((r, r), np.float32), -1), BF16)
    per_e = pl.BlockSpec((1, r, LANES), lambda ei: (ei, 0, 0))
    shared = pl.BlockSpec((1, r, LANES), lambda ei: (0, 0, 0))
    full = lambda a: pl.BlockSpec(a.shape, lambda ei: (0,) * a.ndim)
    return pl.pallas_call(
        functools.partial(_route_select_kernel, cap=cap),
        grid=(e,),
        in_specs=[per_e, full(u), full(lo)],
        out_specs=[per_e, per_e, shared, shared],
        out_shape=[jax.ShapeDtypeStruct((e, r, LANES), F32), jax.ShapeDtypeStruct((e, r, LANES), F32),
                   jax.ShapeDtypeStruct((1, r, LANES), F32), jax.ShapeDtypeStruct((1, r, LANES), F32)],
        scratch_shapes=[pltpu.VMEM((2, r, LANES), F32)],
        compiler_params=_cparams(("arbitrary",)),
        name="route_select",
    )(aff3, u, lo)


SEARCH_CHUNK = 1024
_INT3, _INT2, _INT1, _F32 = "int3", "int2", "int1", "f32"
_NPARTS = {_INT3: 3, _INT2: 2, _INT1: 1, _F32: 3}


def _bf16_parts(x, kind):
    if kind == _INT1:
        return [x]
    if kind == _INT2:
        d1 = jnp.floor(x * (1.0 / 256.0))
        return [x - d1 * 256.0, d1]
    if kind == _INT3:
        d2 = jnp.floor(x * (1.0 / 65536.0))
        r = x - d2 * 65536.0
        d1 = jnp.floor(r * (1.0 / 256.0))
        return [r - d1 * 256.0, d1, d2]
    a1 = x.astype(BF16).astype(F32)
    r1 = x - a1
    a2 = r1.astype(BF16).astype(F32)
    return [a1, a2, r1 - a2]


def _join_parts(parts, kind):
    if kind == _INT1:
        return parts[0]
    if kind == _INT2:
        return parts[0] + 256.0 * parts[1]
    if kind == _INT3:
        return parts[0] + 256.0 * parts[1] + 65536.0 * parts[2]
    return (parts[0] + parts[1]) + parts[2]


def _rank_search_kernel(*refs, kinds, count_kind, chunk):
    npay = len(kinds)
    cnt_ref, pay_refs = refs[0], refs[1:1 + npay]
    tok_ref, out_refs = refs[1 + npay], refs[2 + npay:2 + 2 * npay]
    lhs_ref = refs[-1]
    c = pl.program_id(1)
    r = cnt_ref.shape[1]

    @pl.when(c == 0)
    def _():
        counts = cnt_ref[0]
        ends = jnp.broadcast_to(counts[:, LANES - 1:LANES], counts.shape)
        first_row = lax.broadcasted_iota(jnp.int32, counts.shape, 0) == 0
        row_start = jnp.where(first_row, 0.0, pltpu.roll(ends, 1, 0))
        row = 0
        for val, kind in zip([counts - row_start] + [ref[0] for ref in pay_refs], (count_kind,) + tuple(kinds)):
            for part in _bf16_parts(val, kind):
                lhs_ref[row:row + LANES, :] = part.T.astype(BF16)
                row += LANES

    target = (c * chunk + 1 + lax.broadcasted_iota(jnp.int32, (1, chunk), 1)).astype(F32)
    row_end = cnt_ref[0, :, LANES - 1:LANES]
    before = row_end < target
    rho = jnp.sum(jnp.where(before, 1.0, 0.0), axis=0, keepdims=True).astype(jnp.int32)
    base = jnp.max(jnp.where(before, row_end, 0.0), axis=0, keepdims=True)
    onehot = jnp.where(lax.broadcasted_iota(jnp.int32, (r, chunk), 0) == rho, 1.0, 0.0).astype(BF16)
    fetched = jnp.dot(lhs_ref[...], onehot, preferred_element_type=F32)

    def take(first_part, kind):
        parts = [fetched[(first_part + k) * LANES:(first_part + k + 1) * LANES] for k in range(_NPARTS[kind])]
        return _join_parts(parts, kind)

    lam = jnp.sum(jnp.where(take(0, count_kind) < target - base, 1.0, 0.0), axis=0, keepdims=True).astype(jnp.int32)
    tok_ref[0] = rho * LANES + lam
    at_lane = lax.broadcasted_iota(jnp.int32, (LANES, chunk), 0) == lam
    first_part = _NPARTS[count_kind]
    for out_ref, kind in zip(out_refs, kinds):
        out_ref[0] = jnp.sum(jnp.where(at_lane, take(first_part, kind), 0.0), axis=0, keepdims=True)
        first_part += _NPARTS[kind]


def _rank_search(counts, per_token, nslots, payloads=()):
    g, r, _ = counts.shape
    chunk = min(SEARCH_CHUNK, nslots)
    kinds = tuple(k for _, k in payloads)
    count_kind = _INT1 if per_token * LANES < 256 else _INT2
    assert per_token * LANES < 65536
    nparts = _NPARTS[count_kind] + sum(_NPARTS[k] for k in kinds)

    def in_spec(a):
        if a.shape[0] == 1:
            return pl.BlockSpec((1, r, LANES), lambda gi, ci: (0, 0, 0))
        return pl.BlockSpec((1, r, LANES), lambda gi, ci: (gi, 0, 0))

    out_spec = pl.BlockSpec((1, 1, chunk), lambda gi, ci: (gi, 0, ci))
    outs = pl.pallas_call(
        functools.partial(_rank_search_kernel, kinds=kinds, count_kind=count_kind, chunk=chunk),
        grid=(g, nslots // chunk),
        in_specs=[in_spec(counts)] + [in_spec(a) for a, _ in payloads],
        out_specs=[out_spec] * (1 + len(kinds)),
        out_shape=[jax.ShapeDtypeStruct((g, 1, nslots), jnp.int32)]
        + [jax.ShapeDtypeStruct((g, 1, nslots), F32)] * len(kinds),
        scratch_shapes=[pltpu.VMEM((nparts * LANES, r), BF16)],
        compiler_params=_cparams(("parallel", "arbitrary")),
        name="rank_search",
    )(counts, *[a for a, _ in payloads])
    return outs[0], outs[1:]


def _sc_scatter_rows(rows, idx):
    m, w = rows.shape
    nchunks = m // GATHER_ROWS
    per_worker = nchunks // SC_WORKERS
    assert per_worker * SC_WORKERS * GATHER_ROWS == m and per_worker % 8 == 0
    mesh = plsc.VectorSubcoreMesh(core_axis_name="c", subcore_axis_name="s")

    @functools.partial(
        pl.kernel, mesh=mesh, out_type=jax.ShapeDtypeStruct((m, w), rows.dtype),
        scratch_types=[pltpu.VMEM((per_worker, GATHER_ROWS), jnp.int32),
                       pltpu.VMEM((GATHER_ROWS, w), rows.dtype),
                       pltpu.SemaphoreType.DMA])
    def scatter(rows_hbm, idx_hbm, out_hbm, idx_v, rows_v, sem):
        wid = lax.axis_index("s") * SC_CORES + lax.axis_index("c")
        first = wid * per_worker
        pltpu.sync_copy(idx_hbm.at[pl.ds(first, per_worker)], idx_v)

        @pl.loop(0, per_worker)
        def _(c):
            pltpu.sync_copy(rows_hbm.at[pl.ds((first + c) * GATHER_ROWS, GATHER_ROWS)], rows_v)
            pltpu.async_copy(rows_v, out_hbm.at[idx_v.at[c]], sem).wait()

    return scatter(rows, idx.reshape(nchunks, GATHER_ROWS))


def _moe_kernel(x_ref, gate_ref, wg_ref, wu_ref, wd_ref, o_ref):
    x = _unpack_rows(x_ref[0])
    y = jnp.zeros((x.shape[0], D_MODEL), F32)
    for c in range(D_EXPERT // FC):
        sl = slice(c * FC, (c + 1) * FC)
        a = jnp.dot(x, wg_ref[0, :, sl], preferred_element_type=F32)
        u = jnp.dot(x, wu_ref[0, :, sl], preferred_element_type=F32)
        hid = (a * jax.nn.sigmoid(a) * u).astype(BF16)
        y = y + jnp.dot(hid, wd_ref[0, sl, :], preferred_element_type=F32)
    gate = jnp.broadcast_to(gate_ref[0], (LANES, x.shape[0])).T
    o_ref[0] = _pack_rows(y * jnp.tile(gate, (1, D_MODEL // LANES)))


def _moe(xe, gate3, wg, wu, wd):
    e, cap, _ = xe.shape
    tc = min(TC, cap)
    tokb = lambda w: pl.BlockSpec((1, tc, w), lambda ei, i: (ei, i, 0))
    wspec = lambda a: pl.BlockSpec((1,) + a.shape[1:], lambda ei, i: (ei, 0, 0))
    return pl.pallas_call(
        _moe_kernel,
        grid=(e, cap // tc),
        in_specs=[tokb(HALF), pl.BlockSpec((1, 1, tc), lambda ei, i: (ei, 0, i)), wspec(wg), wspec(wu), wspec(wd)],
        out_specs=tokb(HALF),
        out_shape=jax.ShapeDtypeStruct((e, cap, HALF), U32),
        compiler_params=_cparams(("parallel", "arbitrary")),
        name="moe_ffn",
    )(xe, gate3, wg, wu, wd)


_FIRST, _LAST, _ACTIVE = 1, 2, 4


def _combine_kernel(tile_ref, blk_ref, flag_ref, tok_ref, yg_ref, x1_ref, g_ref, o_ref, acc_ref):
    s = pl.program_id(0)
    flag = flag_ref[s]

    @pl.when((flag & _FIRST) != 0)
    def _():
        acc_ref[...] = jnp.zeros_like(acc_ref)

    @pl.when((flag & _ACTIVE) != 0)
    def _():
        tok_row = tile_ref[s] * TS + lax.broadcasted_iota(jnp.int32, (TS, SB), 0)
        onehot = jnp.where(tok_ref[0] == tok_row, 1.0, 0.0).astype(BF16)
        acc_ref[...] += jnp.dot(onehot, _unpack_rows(yg_ref[...]), preferred_element_type=F32)

    @pl.when((flag & _LAST) != 0)
    def _():
        y = x1_ref[...] + acc_ref[...]
        ms = jnp.mean(y * y, axis=-1, keepdims=True)
        o_ref[...] = y * lax.rsqrt(ms + RMS_EPS) * g_ref[...]


def _combine_schedule(bounds, nslots):
    nt, nblk = bounds.shape[0] - 1, nslots // SB
    steps = nt + nblk
    lo, hi = bounds[:-1], bounds[1:]
    b_lo = jnp.minimum(lo // SB, nblk - 1)
    b_hi = jnp.where(hi > lo, (hi - 1) // SB, b_lo)
    nst = b_hi - b_lo + 1
    cum = jnp.cumsum(nst)
    start = cum - nst
    s = jnp.arange(steps, dtype=jnp.int32)
    tile = jnp.minimum(jnp.searchsorted(cum, s, side="right").astype(jnp.int32), nt - 1)
    active = s < cum[-1]
    blk = jnp.where(active, b_lo[tile] + s - start[tile], b_hi[nt - 1])
    first = active & (s == start[tile])
    last = active & (s == cum[tile] - 1)
    flag = first * _FIRST + last * _LAST + active * _ACTIVE
    return tile, blk.astype(jnp.int32), flag.astype(jnp.int32)


def _combine(tile, blk, flag, tok3, yg, x1, g):
    n = x1.shape[0]
    steps = tile.shape[0]
    grid_spec = pltpu.PrefetchScalarGridSpec(
        num_scalar_prefetch=3,
        grid=(steps,),
        in_specs=[pl.BlockSpec((1, 1, SB), lambda s, t, b, f: (b[s], 0, 0)),
                  pl.BlockSpec((SB, HALF), lambda s, t, b, f: (b[s], 0)),
                  pl.BlockSpec((TS, D_MODEL), lambda s, t, b, f: (t[s], 0)),
                  pl.BlockSpec(g.shape, lambda s, t, b, f: (0, 0))],
        out_specs=pl.BlockSpec((TS, D_MODEL), lambda s, t, b, f: (t[s], 0)),
        scratch_shapes=[pltpu.VMEM((TS, D_MODEL), F32)],
    )
    return pl.pallas_call(
        _combine_kernel,
        grid_spec=grid_spec,
        out_shape=jax.ShapeDtypeStruct((n, D_MODEL), F32),
        compiler_params=_cparams(("arbitrary",)),
        name="combine_norm",
    )(tile, blk, flag, tok3, yg, x1, g)


def _group_forward(x, p):
    b, seq, _ = x.shape
    n = b * seq
    cap = CAP_FACTOR * n // N_EXPERTS
    x2 = x.reshape(n, D_MODEL)
    cos_t, sa_t, sb_t = p["rot"]
    qa, kat, va, qb, kbt, vb = _inproj(x2, p["g_attn"], p["w_in"], cos_t[:seq], sa_t[:seq], sb_t[:seq], seq)
    r3 = lambda a: a.reshape(b, seq, a.shape[-1])
    oa = _window(r3(qa), kat, r3(va), p["sink"], p["g_out_a"])
    ob = _natten(r3(qb), kbt, r3(vb), p["na_tabs"], p["g_out_b"])
    x1, h, aff3 = _outproj(x2, oa.reshape(n, DQA), ob.reshape(n, DB), p["w_out"], p["g_ffn"], p["w_router_t"])

    nslots = N_EXPERTS * cap
    pos, c_excl, s_incl, s_excl = _route_select(aff3, cap)
    idx, (gate, s_at, c_at) = _rank_search(pos, 1, cap, [(aff3, _F32), (s_excl, _INT3), (c_excl, _INT1)])
    dest = (s_at + c_at).astype(jnp.int32).reshape(nslots)
    xe = _sc_gather_rows(h, idx.reshape(nslots)).reshape(N_EXPERTS, cap, HALF)
    ye = _moe(xe, gate, p["w_gate"], p["w_up"], p["w_down"])
    yg = _sc_scatter_rows(ye.reshape(nslots, HALF), dest)

    tok_sorted, _ = _rank_search(s_incl, N_EXPERTS, nslots)
    tile_end = s_incl.reshape(n)[TS - 1::TS].astype(jnp.int32)
    bounds = jnp.concatenate([jnp.zeros((1,), jnp.int32), tile_end])
    tile, blk, flag = _combine_schedule(bounds, nslots)
    y = _combine(tile, blk, flag, tok_sorted.reshape(-1, 1, SB), yg, x1, p["g_final"])
    return y.reshape(b, seq, D_MODEL)


def kernel(x_prompt, x_sample, g_attn, w_in, g_out_a, g_out_b, sink_a, rpb_b, w_out, g_ffn, w_router,
           w_gate, w_up, w_down, g_final):
    assert g_attn.shape[0] == 1, "single trunk layer"
    wr = w_router[0].T
    wr_hi = wr.astype(BF16)
    wr_lo = (wr - wr_hi.astype(F32)).astype(BF16)
    p = {
        "g_attn": g_attn[0][None, :], "w_in": w_in[0].astype(BF16),
        "g_out_a": g_out_a[0][None, :], "g_out_b": g_out_b[0][None, :],
        "sink": sink_a[0], "na_tabs": _natten_tables(rpb_b[0], 4 * NA_ROWS), "w_out": w_out[0].astype(BF16),
        "g_ffn": g_ffn[0][None, :], "w_router_t": jnp.concatenate([wr_hi, wr_lo], axis=0),
        "w_gate": w_gate[0].astype(BF16), "w_up": w_up[0].astype(BF16), "w_down": w_down[0].astype(BF16),
        "g_final": g_final[None, :],
        "rot": _rotary_tables(max(x_prompt.shape[1], x_sample.shape[1])),
    }
    return (_group_forward(x_prompt, p), _group_forward(x_sample, p))
```

```python
import functools

import jax
import jax.numpy as jnp
import numpy as np
from jax import lax
from jax.experimental import pallas as pl
from jax.experimental.pallas import tpu as pltpu
from jax.experimental.pallas import tpu_sc as plsc

D_MODEL = 1024
HEAD_DIM = 64
HQ_A = 8
HKV_A = 2
H_B = 8
GRP = HQ_A // HKV_A
DQA = HQ_A * HEAD_DIM
DKA = HKV_A * HEAD_DIM
DB = H_B * HEAD_DIM
D_IN = DQA + 2 * DKA + 3 * DB
WINDOW = 128
BLK = 128
ROPE_THETA = 500000.0
ROT_DIM = HEAD_DIM // 4
GRID_W = 64
NA_ROWS = 8
NA_COLS = 16
NA_QROWS = 2
NA_KROWS = NA_ROWS + NA_QROWS
NA_KEYS = NA_KROWS * GRID_W
NA_Q = NA_QROWS * GRID_W
N_EXPERTS = 16
CAP_FACTOR = 2
D_EXPERT = 2 * D_MODEL
RMS_EPS = 1e-6
NEG_INF = -1e30
SCALE = HEAD_DIM ** -0.5

LANES = 128
TM = 1024
TSUB = 512
TQ = 1024
TC = 1024
FC = 512
TS = 512
SB = 1024
VMEM_LIMIT = 56 * 1024 * 1024

BF16 = jnp.bfloat16
F32 = jnp.float32
_NT = (((1,), (1,)), ((), ()))


def _cparams(sem):
    return pltpu.CompilerParams(dimension_semantics=sem, vmem_limit_bytes=VMEM_LIMIT)


HALF = D_MODEL // 2
U32 = jnp.uint32


def _pack_rows(v):
    r = v.astype(BF16).astype(F32)
    hi = pltpu.bitcast(r[:, :HALF], U32)
    lo = pltpu.bitcast(r[:, HALF:], U32)
    return (hi & jnp.uint32(0xFFFF0000)) | (lo >> 16)


def _unpack_rows(w):
    hi = pltpu.bitcast(w & jnp.uint32(0xFFFF0000), F32)
    lo = pltpu.bitcast(w << 16, F32)
    return jnp.concatenate([hi, lo], axis=1).astype(BF16)


SC_CORES = 2
SC_SUBCORES = 16
SC_WORKERS = SC_CORES * SC_SUBCORES
GATHER_ROWS = 64


def _sc_gather_rows(table, idx):
    m, w = idx.shape[0], table.shape[1]
    nchunks = m // GATHER_ROWS
    per_worker = nchunks // SC_WORKERS
    assert per_worker * SC_WORKERS * GATHER_ROWS == m and per_worker % 8 == 0
    mesh = plsc.VectorSubcoreMesh(core_axis_name="c", subcore_axis_name="s")

    @functools.partial(
        pl.kernel, mesh=mesh, out_type=jax.ShapeDtypeStruct((m, w), table.dtype),
        scratch_types=[pltpu.VMEM((per_worker, GATHER_ROWS), jnp.int32),
                       pltpu.VMEM((GATHER_ROWS, w), table.dtype),
                       pltpu.SemaphoreType.DMA])
    def gather(tab_hbm, idx_hbm, out_hbm, idx_v, rows_v, sem):
        wid = lax.axis_index("s") * SC_CORES + lax.axis_index("c")
        first = wid * per_worker
        pltpu.sync_copy(idx_hbm.at[pl.ds(first, per_worker)], idx_v)

        @pl.loop(0, per_worker)
        def _(c):
            pltpu.async_copy(tab_hbm.at[idx_v.at[c]], rows_v, sem).wait()
            pltpu.sync_copy(rows_v, out_hbm.at[pl.ds((first + c) * GATHER_ROWS, GATHER_ROWS)])

    return gather(table, idx.reshape(nchunks, GATHER_ROWS))


def _inproj_kernel(x_ref, g_ref, w_ref, wkt_ref, cos_ref, sa_ref, sb_ref,
                   qa_ref, kat_ref, va_ref, qb_ref, kbt_ref, vb_ref):
    def normed(rows):
        x = x_ref[rows, :]
        ms = jnp.mean(x * x, axis=-1, keepdims=True)
        return (x * lax.rsqrt(ms + RMS_EPS) * g_ref[...]).astype(BF16)

    def project(rows, hn):
        c, sa, sb = cos_ref[rows, :], sa_ref[rows, :], sb_ref[rows, :]

        def proj(lo, n):
            return jnp.dot(hn, w_ref[:, lo:lo + n], preferred_element_type=F32)

        def rot(blk):
            return blk * c + pltpu.roll(blk, LANES - ROT_DIM // 2, 1) * sa + pltpu.roll(blk, ROT_DIM // 2, 1) * sb

        for j in range(DQA // LANES):
            qa_ref[rows, j * LANES:(j + 1) * LANES] = (rot(proj(j * LANES, LANES)) * SCALE).astype(BF16)
        kat_ref[:, rows] = rot(proj(DQA, DKA)).T.astype(BF16)
        va_ref[rows, :] = proj(DQA + DKA, DKA).astype(BF16)
        qb_ref[rows, :] = (proj(DQA + 2 * DKA, DB) * SCALE).astype(BF16)
        kbt_ref[:, rows] = lax.dot_general(wkt_ref[...], hn, _NT, preferred_element_type=F32).astype(BF16)
        vb_ref[rows, :] = proj(DQA + 2 * DKA + 2 * DB, DB).astype(BF16)

    subs = [slice(r0, r0 + TSUB) for r0 in range(0, TM, TSUB)]
    hn = normed(subs[0])
    for k, rows in enumerate(subs):
        nxt = normed(subs[k + 1]) if k + 1 < len(subs) else None
        project(rows, hn)
        hn = nxt


def _inproj(x2, g, w_bf, cos_t, sa_t, sb_t, seq):
    n = x2.shape[0]
    per_seq = seq // TM
    tok = lambda w: pl.BlockSpec((TM, w), lambda i: (i, 0))
    full = lambda a: pl.BlockSpec(a.shape, lambda i: (0,) * a.ndim)
    pos = pl.BlockSpec((TM, LANES), lambda i: (i % per_seq, 0))
    kb_lo = DQA + 2 * DKA + DB
    wkt = w_bf[:, kb_lo:kb_lo + DB].T
    feat = lambda w: pl.BlockSpec((w, TM), lambda i: (0, i))
    tshape = lambda w: jax.ShapeDtypeStruct((n, w), BF16)
    fshape = lambda w: jax.ShapeDtypeStruct((w, n), BF16)
    return pl.pallas_call(
        _inproj_kernel,
        grid=(n // TM,),
        in_specs=[tok(D_MODEL), full(g), full(w_bf), full(wkt), pos, pos, pos],
        out_specs=[tok(DQA), feat(DKA), tok(DKA), tok(DB), feat(DB), tok(DB)],
        out_shape=[tshape(DQA), fshape(DKA), tshape(DKA), tshape(DB), fshape(DB), tshape(DB)],
        compiler_params=_cparams(("parallel",)),
        name="inproj",
    )(x2, g, w_bf, wkt, cos_t, sa_t, sb_t)


def _rotary_tables(seq):
    half = ROT_DIM // 2
    inv_freq = jnp.float32(ROPE_THETA) ** (-(jnp.arange(half, dtype=F32) * 2.0) / ROT_DIM)
    ang = jnp.arange(seq, dtype=F32)[:, None] * inv_freq[None, :]
    cos, sin = jnp.cos(ang), jnp.sin(ang)
    ones = jnp.ones((seq, HEAD_DIM - ROT_DIM), F32)
    zeros = jnp.zeros((seq, HEAD_DIM - ROT_DIM), F32)
    zh = jnp.zeros((seq, half), F32)
    rep = LANES // HEAD_DIM
    cos_t = jnp.tile(jnp.concatenate([cos, cos, ones], axis=1), (1, rep))
    sa_t = jnp.tile(jnp.concatenate([-sin, zh, zeros], axis=1), (1, rep))
    sb_t = jnp.tile(jnp.concatenate([zh, sin, zeros], axis=1), (1, rep))
    return cos_t, sa_t, sb_t


SM_ROWS = 32


def _sm_chunks(shape):
    nh, nq, nk = shape
    chunks = [(h, slice(r0, r0 + SM_ROWS)) for h in range(nh) for r0 in range(0, nq, SM_ROWS)]
    return chunks, [slice(l0, l0 + LANES) for l0 in range(0, nk, LANES)]


def _softmax_max(score, shape, pm_ref, mb_ref, floor=None):
    chunks, lanes = _sm_chunks(shape)
    for h, rc in chunks:
        pm_ref[h, rc, :] = functools.reduce(jnp.maximum, [score(h, rc, ls) for ls in lanes])
    m = jnp.max(pm_ref[...], axis=-1, keepdims=True)
    if floor is not None:
        m = jnp.maximum(m, floor)
    mb_ref[...] = jnp.broadcast_to(m, mb_ref.shape)
    return m


def _softmax_exp(score, p_ref, mb_ref, ps_ref):
    chunks, lanes = _sm_chunks(p_ref.shape)
    for h, rc in chunks:
        mb = mb_ref[h, rc, :]
        total = None
        for ls in lanes:
            p = jnp.exp(score(h, rc, ls) - mb)
            p_ref[h, rc, ls] = p.astype(BF16)
            total = p if total is None else total + p
        ps_ref[h, rc, :] = total
    return jnp.sum(ps_ref[...], axis=-1, keepdims=True)


def _window_kernel(q_ref, kp_ref, kc_ref, kn_ref, vp_ref, vc_ref, vn_ref, sink_ref, g_ref, o_ref,
                   kt_ref, v_ref, s_ref, p_ref, pm_ref, mb_ref, ps_ref, *, nb):
    i = pl.program_id(1)
    nsub = TQ // BLK
    kt_ref[:, 0:BLK] = kp_ref[...]
    kt_ref[:, BLK:BLK + TQ] = kc_ref[...]
    kt_ref[:, BLK + TQ:TQ + 2 * BLK] = kn_ref[...]
    v_ref[0:BLK] = vp_ref[0]
    v_ref[BLK:BLK + TQ] = vc_ref[0]
    v_ref[BLK + TQ:TQ + 2 * BLK] = vn_ref[0]
    qi = lax.broadcasted_iota(jnp.int32, (BLK, 3 * BLK), 0)
    kj = lax.broadcasted_iota(jnp.int32, (BLK, 3 * BLK), 1)
    band = jnp.abs(kj - BLK - qi) <= WINDOW
    sink = sink_ref[...]

    def scores(j):
        n = i * nsub + j
        k_lo = jnp.where(n >= 1, 0, BLK)
        k_hi = jnp.where(n <= nb - 2, 3 * BLK, 2 * BLK)
        bias = jnp.where(band & (kj >= k_lo) & (kj < k_hi), 0.0, NEG_INF)
        for h in range(HKV_A):
            kh = kt_ref[h * HEAD_DIM:(h + 1) * HEAD_DIM, j * BLK:(j + 3) * BLK]
            qs = jnp.concatenate([q_ref[0, j * BLK:(j + 1) * BLK, hd * HEAD_DIM:(hd + 1) * HEAD_DIM]
                                  for hd in range(h * GRP, (h + 1) * GRP)], axis=0)
            s = jnp.dot(qs, kh, preferred_element_type=F32)
            s_ref[j % 3, h * GRP:(h + 1) * GRP] = s.reshape(GRP, BLK, 3 * BLK) + bias[None]

    def score_fn(j):
        sb = s_ref.at[j % 3]
        return lambda hd, rc, ls: sb[hd, rc, ls]

    def values(j, denom):
        outs = []
        for h in range(HKV_A):
            vh = v_ref[j * BLK:(j + 3) * BLK, h * HEAD_DIM:(h + 1) * HEAD_DIM]
            ph = p_ref[j % 2, h * GRP:(h + 1) * GRP].reshape(GRP * BLK, 3 * BLK)
            o = jnp.dot(ph, vh, preferred_element_type=F32)
            outs += [o[g * BLK:(g + 1) * BLK] / denom[h * GRP + g] for g in range(GRP)]
        ob = jnp.concatenate(outs, axis=1)
        ms = jnp.mean(ob * ob, axis=-1, keepdims=True)
        o_ref[0, j * BLK:(j + 1) * BLK, :] = (ob * lax.rsqrt(ms + RMS_EPS) * g_ref[...]).astype(BF16)

    maxima, denoms = {}, {}
    for t in range(nsub + 3):
        if t < nsub:
            scores(t)
        if 0 <= t - 2 < nsub:
            j = t - 2
            rowsum = _softmax_exp(score_fn(j), p_ref.at[j % 2], mb_ref.at[j % 2], ps_ref.at[j % 2])
            denoms[j] = rowsum + jnp.exp(sink - maxima.pop(j))
        if 0 <= t - 3 < nsub:
            values(t - 3, denoms.pop(t - 3))
        if 0 <= t - 1 < nsub:
            j = t - 1
            maxima[j] = _softmax_max(score_fn(j), s_ref.shape[1:], pm_ref.at[j % 2], mb_ref.at[j % 2], floor=sink)


def _window(qa, kat, va, sink, g):
    b, seq, _ = qa.shape
    nb = seq // BLK
    r = TQ // BLK
    nt = seq // TQ
    before = lambda i: jnp.maximum(i * r - 1, 0)
    after = lambda i: jnp.minimum((i + 1) * r, nb - 1)
    cur = lambda w: pl.BlockSpec((1, TQ, w), lambda bi, i: (bi, i, 0))
    prev = pl.BlockSpec((1, BLK, DKA), lambda bi, i: (bi, before(i), 0))
    nxt = pl.BlockSpec((1, BLK, DKA), lambda bi, i: (bi, after(i), 0))
    cur_t = pl.BlockSpec((DKA, TQ), lambda bi, i: (0, bi * nt + i))
    prev_t = pl.BlockSpec((DKA, BLK), lambda bi, i: (0, bi * nb + before(i)))
    nxt_t = pl.BlockSpec((DKA, BLK), lambda bi, i: (0, bi * nb + after(i)))
    sink3 = sink.reshape(HQ_A, 1, 1)
    return pl.pallas_call(
        functools.partial(_window_kernel, nb=nb),
        grid=(b, nt),
        in_specs=[cur(DQA), prev_t, cur_t, nxt_t, prev, cur(DKA), nxt,
                  pl.BlockSpec(sink3.shape, lambda bi, i: (0, 0, 0)), pl.BlockSpec(g.shape, lambda bi, i: (0, 0))],
        out_specs=cur(DQA),
        out_shape=jax.ShapeDtypeStruct((b, seq, DQA), BF16),
        scratch_shapes=[pltpu.VMEM((DKA, TQ + 2 * BLK), BF16), pltpu.VMEM((TQ + 2 * BLK, DKA), BF16),
                        pltpu.VMEM((3, HQ_A, BLK, 3 * BLK), F32), pltpu.VMEM((2, HQ_A, BLK, 3 * BLK), BF16),
                        ] + [pltpu.VMEM((2, HQ_A, BLK, LANES), F32)] * 3,
        compiler_params=_cparams(("parallel", "parallel")),
        name="window_attn",
    )(qa, kat, kat, kat, va, va, va, sink3, g)


NA_RB = 8
NA_TILE = NA_RB * NA_Q
NA_TILE_ROWS = NA_RB * NA_QROWS
NA_HALO_ROWS = NA_ROWS // 2
NA_HALO = NA_HALO_ROWS * GRID_W


def _natten_kernel(q_ref, kp_ref, kc_ref, kn_ref, vp_ref, vc_ref, vn_ref, tab_ref, g_ref, o_ref,
                   kt_ref, v_ref, s_ref, p_ref, pm_ref, mb_ref, ps_ref, *, rows):
    i = pl.program_id(1)
    nrb = rows // NA_QROWS
    kt_ref[:, 0:NA_HALO] = kp_ref[...]
    kt_ref[:, NA_HALO:NA_HALO + NA_TILE] = kc_ref[...]
    kt_ref[:, NA_HALO + NA_TILE:2 * NA_HALO + NA_TILE] = kn_ref[...]
    v_ref[0:NA_HALO] = vp_ref[0]
    v_ref[NA_HALO:NA_HALO + NA_TILE] = vc_ref[0]
    v_ref[NA_HALO + NA_TILE:2 * NA_HALO + NA_TILE] = vn_ref[0]

    def block(r):
        rb = i * NA_RB + r
        kstart = jnp.clip(NA_QROWS * rb - NA_ROWS // 2, 0, rows - NA_KROWS)
        off = (kstart - (i * NA_TILE_ROWS - NA_HALO_ROWS)) * GRID_W
        variant = jnp.where(rb < 2, rb, jnp.where(rb >= nrb - 2, rb - (nrb - 5), 2))
        return pl.ds(pl.multiple_of(off, LANES), NA_KEYS), variant

    def scores(r):
        keys, variant = block(r)
        for h in range(H_B):
            sl = slice(h * HEAD_DIM, (h + 1) * HEAD_DIM)
            s = jnp.dot(q_ref[0, r * NA_Q:(r + 1) * NA_Q, sl], kt_ref[sl, keys], preferred_element_type=F32)
            s_ref[r % 3, h] = s + tab_ref[variant, h]

    def score_fn(r):
        sb = s_ref.at[r % 3]
        return lambda h, rc, ls: sb[h, rc, ls]

    def values(r, denom):
        keys, _ = block(r)
        outs = []
        for h in range(H_B):
            sl = slice(h * HEAD_DIM, (h + 1) * HEAD_DIM)
            outs.append(jnp.dot(p_ref[r % 2, h], v_ref[keys, sl], preferred_element_type=F32) / denom[h])
        ob = jnp.concatenate(outs, axis=1)
        ms = jnp.mean(ob * ob, axis=-1, keepdims=True)
        o_ref[0, r * NA_Q:(r + 1) * NA_Q, :] = (ob * lax.rsqrt(ms + RMS_EPS) * g_ref[...]).astype(BF16)

    denoms = {}
    for t in range(NA_RB + 3):
        if t < NA_RB:
            scores(t)
        if 0 <= t - 2 < NA_RB:
            r = t - 2
            denoms[r] = _softmax_exp(score_fn(r), p_ref.at[r % 2], mb_ref.at[r % 2], ps_ref.at[r % 2])
        if 0 <= t - 3 < NA_RB:
            values(t - 3, denoms.pop(t - 3))
        if 0 <= t - 1 < NA_RB:
            r = t - 1
            _softmax_max(score_fn(r), s_ref.shape[1:], pm_ref.at[r % 2], mb_ref.at[r % 2])


def _natten_variant_rowblocks(nrb):
    return (0, 1, 2, nrb - 2, nrb - 1)


def _natten_tables(rpb, rows):
    nrb = rows // NA_QROWS
    qc = np.arange(GRID_W)
    kc = np.arange(GRID_W)
    cs = np.clip(qc - NA_COLS // 2, 0, GRID_W - NA_COLS)
    col_ok = (kc[None, :] >= cs[:, None]) & (kc[None, :] < cs[:, None] + NA_COLS)
    cidx = np.clip(kc[None, :] - qc[:, None] + NA_COLS - 1, 0, 2 * NA_COLS - 2)
    tabs = []
    for rb in _natten_variant_rowblocks(nrb):
        r0 = rb * NA_QROWS
        qrows = r0 + np.arange(NA_QROWS)
        rs = np.clip(qrows - NA_ROWS // 2, 0, rows - NA_ROWS)
        kstart = int(np.clip(r0 - NA_ROWS // 2, 0, rows - NA_KROWS))
        krows = kstart + np.arange(NA_KROWS)
        row_ok = (krows[None, :] >= rs[:, None]) & (krows[None, :] < rs[:, None] + NA_ROWS)
        ridx = np.clip(krows[None, :] - qrows[:, None] + NA_ROWS - 1, 0, 2 * NA_ROWS - 2)
        valid = row_ok[:, None, :, None] & col_ok[None, :, None, :]
        r1h = (ridx[:, :, None] == np.arange(2 * NA_ROWS - 1)).astype(np.float32)
        c1h = (cidx[:, :, None] == np.arange(2 * NA_COLS - 1)).astype(np.float32)
        bias = jnp.einsum("rka,hab,qcb->hrqkc", r1h, rpb.astype(F32), c1h, precision=lax.Precision.HIGHEST)
        bias = jnp.where(valid[None], bias, NEG_INF)
        tabs.append(bias.reshape(H_B, NA_Q, NA_KEYS))
    return jnp.stack(tabs)


def _natten(qb, kbt, vb, tabs, g):
    b, seq, _ = qb.shape
    rows = seq // GRID_W
    nrb = rows // NA_QROWS
    nt = seq // NA_TILE
    assert rows >= NA_KROWS and nrb >= 5 and nt * NA_TILE == seq
    hpt = NA_TILE // NA_HALO
    before = lambda i: jnp.maximum(i * hpt - 1, 0)
    after = lambda i: jnp.minimum((i + 1) * hpt, nt * hpt - 1)
    cur = pl.BlockSpec((1, NA_TILE, DB), lambda bi, i: (bi, i, 0))
    prev = pl.BlockSpec((1, NA_HALO, DB), lambda bi, i: (bi, before(i), 0))
    nxt = pl.BlockSpec((1, NA_HALO, DB), lambda bi, i: (bi, after(i), 0))
    cur_t = pl.BlockSpec((DB, NA_TILE), lambda bi, i: (0, bi * nt + i))
    prev_t = pl.BlockSpec((DB, NA_HALO), lambda bi, i: (0, bi * nt * hpt + before(i)))
    nxt_t = pl.BlockSpec((DB, NA_HALO), lambda bi, i: (0, bi * nt * hpt + after(i)))
    return pl.pallas_call(
        functools.partial(_natten_kernel, rows=rows),
        grid=(b, nt),
        in_specs=[cur, prev_t, cur_t, nxt_t, prev, cur, nxt,
                  pl.BlockSpec(tabs.shape, lambda bi, i: (0, 0, 0, 0), pipeline_mode=pl.Buffered(1)),
                  pl.BlockSpec(g.shape, lambda bi, i: (0, 0))],
        out_specs=cur,
        out_shape=jax.ShapeDtypeStruct((b, seq, DB), BF16),
        scratch_shapes=[pltpu.VMEM((DB, NA_TILE + 2 * NA_HALO), BF16), pltpu.VMEM((NA_TILE + 2 * NA_HALO, DB), BF16),
                        pltpu.VMEM((3, H_B, NA_Q, NA_KEYS), F32), pltpu.VMEM((2, H_B, NA_Q, NA_KEYS), BF16),
                        ] + [pltpu.VMEM((2, H_B, NA_Q, LANES), F32)] * 3,
        compiler_params=_cparams(("parallel", "arbitrary")),
        name="natten",
    )(qb, kbt, kbt, kbt, vb, vb, vb, tabs, g)


def _outproj_kernel(x_ref, oa_ref, ob_ref, wo_ref, g_ref, wr_ref, x1_ref, h_ref, aff_ref):
    def residual(rows):
        return (x_ref[rows, :]
                + jnp.dot(oa_ref[rows, :], wo_ref[0:DQA, :], preferred_element_type=F32)
                + jnp.dot(ob_ref[rows, :], wo_ref[DQA:DQA + DB, :], preferred_element_type=F32))

    def route(rows, x1):
        x1_ref[rows, :] = x1
        ms = jnp.mean(x1 * x1, axis=-1, keepdims=True)
        hf = x1 * lax.rsqrt(ms + RMS_EPS) * g_ref[...]
        h_hi = hf.astype(BF16)
        h_lo = (hf - h_hi.astype(F32)).astype(BF16)
        h_ref[rows, :] = _pack_rows(hf)
        l_hi = lax.dot_general(wr_ref[...], h_hi, _NT, preferred_element_type=F32)
        l_lo = lax.dot_general(wr_ref[0:N_EXPERTS, :], h_lo, _NT, preferred_element_type=F32)
        logits = l_hi[0:N_EXPERTS] + l_hi[N_EXPERTS:2 * N_EXPERTS] + l_lo
        m = jnp.max(logits, axis=0, keepdims=True)
        e = jnp.exp(logits - m)
        aff = e / jnp.sum(e, axis=0, keepdims=True)
        for k in range(TSUB // LANES):
            aff_ref[:, rows.start // LANES + k, :] = aff[:, k * LANES:(k + 1) * LANES]

    subs = [slice(r0, r0 + TSUB) for r0 in range(0, TM, TSUB)]
    x1 = residual(subs[0])
    for k, rows in enumerate(subs):
        nxt = residual(subs[k + 1]) if k + 1 < len(subs) else None
        route(rows, x1)
        x1 = nxt


def _outproj(x2, oa, ob, wo_bf, g, wr_t):
    n = x2.shape[0]
    tok = lambda w: pl.BlockSpec((TM, w), lambda i: (i, 0))
    full = lambda a: pl.BlockSpec(a.shape, lambda i: (0,) * a.ndim)
    return pl.pallas_call(
        _outproj_kernel,
        grid=(n // TM,),
        in_specs=[tok(D_MODEL), tok(DQA), tok(DB), full(wo_bf), full(g), full(wr_t)],
        out_specs=[tok(D_MODEL), tok(HALF), pl.BlockSpec((N_EXPERTS, TM // LANES, LANES), lambda i: (0, i, 0))],
        out_shape=[jax.ShapeDtypeStruct((n, D_MODEL), F32), jax.ShapeDtypeStruct((n, HALF), U32),
                   jax.ShapeDtypeStruct((N_EXPERTS, n // LANES, LANES), F32)],
        compiler_params=_cparams(("parallel",)),
        name="outproj_router",
    )(x2, oa, ob, wo_bf, g, wr_t)


def _cumsum_tokens(x, u_ref, l_ref):
    within = jnp.dot(x.astype(BF16), u_ref[...], preferred_element_type=F32)
    rowtot = jnp.broadcast_to(within[:, LANES - 1:LANES], within.shape).astype(BF16)
    return within + jnp.dot(l_ref[...], rowtot, preferred_element_type=F32)


def _route_select_kernel(aff_ref, u_ref, l_ref, pos_ref, cexcl_ref, sincl_ref, sexcl_ref, run_ref, *, cap):
    e = pl.program_id(0)
    bits = pltpu.bitcast(aff_ref[0], jnp.int32)

    def count(mask):
        c = jnp.sum(jnp.where(mask, 1.0, 0.0), axis=0, keepdims=True)
        return jnp.sum(c, axis=1, keepdims=True)

    def step(i, t):
        cand = t | jnp.left_shift(jnp.int32(1), 30 - i)
        return jnp.where(count(bits >= cand) >= cap, cand, t)

    thr = lax.fori_loop(0, 31, step, jnp.zeros((1, 1), jnp.int32))
    gt = bits > thr
    eq = bits == thr
    need = cap - count(gt)
    eqf = jnp.where(eq, 1.0, 0.0)
    tie_rank = _cumsum_tokens(eqf, u_ref, l_ref) - eqf
    sel = jnp.where(gt | (eq & (tie_rank < need)), 1.0, 0.0)
    pos = _cumsum_tokens(sel, u_ref, l_ref)

    @pl.when(e == 0)
    def _():
        run_ref[...] = jnp.zeros_like(run_ref)

    cnt_before = run_ref[0]
    s_incl = run_ref[1] + pos
    pos_ref[0] = pos
    cexcl_ref[0] = cnt_before
    cnt = cnt_before + sel
    run_ref[0] = cnt
    run_ref[1] = s_incl
    sincl_ref[0] = s_incl
    sexcl_ref[0] = s_incl - cnt


def _route_select(aff3, cap):
    e, r, _ = aff3.shape
    u = jnp.asarray(np.triu(np.ones((LANES, LANES), np.float32)), BF16)
    lo = jnp.asarray(np.tril(np.ones((r, r), np.float32), -1), BF16)
    per_e = pl.BlockSpec((1, r, LANES), lambda ei: (ei, 0, 0))
    shared = pl.BlockSpec((1, r, LANES), lambda ei: (0, 0, 0))
    full = lambda a: pl.BlockSpec(a.shape, lambda ei: (0,) * a.ndim)
    return pl.pallas_call(
        functools.partial(_route_select_kernel, cap=cap),
        grid=(e,),
        in_specs=[per_e, full(u), full(lo)],
        out_specs=[per_e, per_e, shared, shared],
        out_shape=[jax.ShapeDtypeStruct((e, r, LANES), F32), jax.ShapeDtypeStruct((e, r, LANES), F32),
                   jax.ShapeDtypeStruct((1, r, LANES), F32), jax.ShapeDtypeStruct((1, r, LANES), F32)],
        scratch_shapes=[pltpu.VMEM((2, r, LANES), F32)],
        compiler_params=_cparams(("arbitrary",)),
        name="route_select",
    )(aff3, u, lo)


SEARCH_CHUNK = 1024
_INT3, _INT2, _INT1, _F32 = "int3", "int2", "int1", "f32"
_NPARTS = {_INT3: 3, _INT2: 2, _INT1: 1, _F32: 3}


def _bf16_parts(x, kind):
    if kind == _INT1:
        return [x]
    if kind == _INT2:
        d1 = jnp.floor(x * (1.0 / 256.0))
        return [x - d1 * 256.0, d1]
    if kind == _INT3:
        d2 = jnp.floor(x * (1.0 / 65536.0))
        r = x - d2 * 65536.0
        d1 = jnp.floor(r * (1.0 / 256.0))
        return [r - d1 * 256.0, d1, d2]
    a1 = x.astype(BF16).astype(F32)
    r1 = x - a1
    a2 = r1.astype(BF16).astype(F32)
    return [a1, a2, r1 - a2]


def _join_parts(parts, kind):
    if kind == _INT1:
        return parts[0]
    if kind == _INT2:
        return parts[0] + 256.0 * parts[1]
    if kind == _INT3:
        return parts[0] + 256.0 * parts[1] + 65536.0 * parts[2]
    return (parts[0] + parts[1]) + parts[2]


def _rank_search_kernel(*refs, kinds, count_kind, chunk):
    npay = len(kinds)
    cnt_ref, pay_refs = refs[0], refs[1:1 + npay]
    tok_ref, out_refs = refs[1 + npay], refs[2 + npay:2 + 2 * npay]
    lhs_ref = refs[-1]
    c = pl.program_id(1)
    r = cnt_ref.shape[1]

    @pl.when(c == 0)
    def _():
        counts = cnt_ref[0]
        ends = jnp.broadcast_to(counts[:, LANES - 1:LANES], counts.shape)
        first_row = lax.broadcasted_iota(jnp.int32, counts.shape, 0) == 0
        row_start = jnp.where(first_row, 0.0, pltpu.roll(ends, 1, 0))
        row = 0
        for val, kind in zip([counts - row_start] + [ref[0] for ref in pay_refs], (count_kind,) + tuple(kinds)):
            for part in _bf16_parts(val, kind):
                lhs_ref[row:row + LANES, :] = part.T.astype(BF16)
                row += LANES

    target = (c * chunk + 1 + lax.broadcasted_iota(jnp.int32, (1, chunk), 1)).astype(F32)
    row_end = cnt_ref[0, :, LANES - 1:LANES]
    before = row_end < target
    rho = jnp.sum(jnp.where(before, 1.0, 0.0), axis=0, keepdims=True).astype(jnp.int32)
    base = jnp.max(jnp.where(before, row_end, 0.0), axis=0, keepdims=True)
    onehot = jnp.where(lax.broadcasted_iota(jnp.int32, (r, chunk), 0) == rho, 1.0, 0.0).astype(BF16)
    fetched = jnp.dot(lhs_ref[...], onehot, preferred_element_type=F32)

    def take(first_part, kind):
        parts = [fetched[(first_part + k) * LANES:(first_part + k + 1) * LANES] for k in range(_NPARTS[kind])]
        return _join_parts(parts, kind)

    lam = jnp.sum(jnp.where(take(0, count_kind) < target - base, 1.0, 0.0), axis=0, keepdims=True).astype(jnp.int32)
    tok_ref[0] = rho * LANES + lam
    at_lane = lax.broadcasted_iota(jnp.int32, (LANES, chunk), 0) == lam
    first_part = _NPARTS[count_kind]
    for out_ref, kind in zip(out_refs, kinds):
        out_ref[0] = jnp.sum(jnp.where(at_lane, take(first_part, kind), 0.0), axis=0, keepdims=True)
        first_part += _NPARTS[kind]


def _rank_search(counts, per_token, nslots, payloads=()):
    g, r, _ = counts.shape
    chunk = min(SEARCH_CHUNK, nslots)
    kinds = tuple(k for _, k in payloads)
    count_kind = _INT1 if per_token * LANES < 256 else _INT2
    assert per_token * LANES < 65536
    nparts = _NPARTS[count_kind] + sum(_NPARTS[k] for k in kinds)

    def in_spec(a):
        if a.shape[0] == 1:
            return pl.BlockSpec((1, r, LANES), lambda gi, ci: (0, 0, 0))
        return pl.BlockSpec((1, r, LANES), lambda gi, ci: (gi, 0, 0))

    out_spec = pl.BlockSpec((1, 1, chunk), lambda gi, ci: (gi, 0, ci))
    outs = pl.pallas_call(
        functools.partial(_rank_search_kernel, kinds=kinds, count_kind=count_kind, chunk=chunk),
        grid=(g, nslots // chunk),
        in_specs=[in_spec(counts)] + [in_spec(a) for a, _ in payloads],
        out_specs=[out_spec] * (1 + len(kinds)),
        out_shape=[jax.ShapeDtypeStruct((g, 1, nslots), jnp.int32)]
        + [jax.ShapeDtypeStruct((g, 1, nslots), F32)] * len(kinds),
        scratch_shapes=[pltpu.VMEM((nparts * LANES, r), BF16)],
        compiler_params=_cparams(("parallel", "arbitrary")),
        name="rank_search",
    )(counts, *[a for a, _ in payloads])
    return outs[0], outs[1:]


def _sc_scatter_rows(rows, idx):
    m, w = rows.shape
    nchunks = m // GATHER_ROWS
    per_worker = nchunks // SC_WORKERS
    assert per_worker * SC_WORKERS * GATHER_ROWS == m and per_worker % 8 == 0
    mesh = plsc.VectorSubcoreMesh(core_axis_name="c", subcore_axis_name="s")

    @functools.partial(
        pl.kernel, mesh=mesh, out_type=jax.ShapeDtypeStruct((m, w), rows.dtype),
        scratch_types=[pltpu.VMEM((per_worker, GATHER_ROWS), jnp.int32),
                       pltpu.VMEM((GATHER_ROWS, w), rows.dtype),
                       pltpu.SemaphoreType.DMA])
    def scatter(rows_hbm, idx_hbm, out_hbm, idx_v, rows_v, sem):
        wid = lax.axis_index("s") * SC_CORES + lax.axis_index("c")
        first = wid * per_worker
        pltpu.sync_copy(idx_hbm.at[pl.ds(first, per_worker)], idx_v)

        @pl.loop(0, per_worker)
        def _(c):
            pltpu.sync_copy(rows_hbm.at[pl.ds((first + c) * GATHER_ROWS, GATHER_ROWS)], rows_v)
            pltpu.async_copy(rows_v, out_hbm.at[idx_v.at[c]], sem).wait()

    return scatter(rows, idx.reshape(nchunks, GATHER_ROWS))


def _moe_kernel(x_ref, gate_ref, wg_ref, wu_ref, wd_ref, o_ref):
    x = _unpack_rows(x_ref[0])
    y = jnp.zeros((x.shape[0], D_MODEL), F32)
    for c in range(D_EXPERT // FC):
        sl = slice(c * FC, (c + 1) * FC)
        a = jnp.dot(x, wg_ref[0, :, sl], preferred_element_type=F32)
        u = jnp.dot(x, wu_ref[0, :, sl], preferred_element_type=F32)
        hid = (a * jax.nn.sigmoid(a) * u).astype(BF16)
        y = y + jnp.dot(hid, wd_ref[0, sl, :], preferred_element_type=F32)
    gate = jnp.broadcast_to(gate_ref[0], (LANES, x.shape[0])).T
    o_ref[0] = _pack_rows(y * jnp.tile(gate, (1, D_MODEL // LANES)))


def _moe(xe, gate3, wg, wu, wd):
    e, cap, _ = xe.shape
    tc = min(TC, cap)
    tokb = lambda w: pl.BlockSpec((1, tc, w), lambda ei, i: (ei, i, 0))
    wspec = lambda a: pl.BlockSpec((1,) + a.shape[1:], lambda ei, i: (ei, 0, 0))
    return pl.pallas_call(
        _moe_kernel,
        grid=(e, cap // tc),
        in_specs=[tokb(HALF), pl.BlockSpec((1, 1, tc), lambda ei, i: (ei, 0, i)), wspec(wg), wspec(wu), wspec(wd)],
        out_specs=tokb(HALF),
        out_shape=jax.ShapeDtypeStruct((e, cap, HALF), U32),
        compiler_params=_cparams(("parallel", "arbitrary")),
        name="moe_ffn",
    )(xe, gate3, wg, wu, wd)


_FIRST, _LAST, _ACTIVE = 1, 2, 4


def _combine_kernel(tile_ref, blk_ref, flag_ref, tok_ref, yg_ref, x1_ref, g_ref, o_ref, acc_ref):
    s = pl.program_id(0)
    flag = flag_ref[s]

    @pl.when((flag & _FIRST) != 0)
    def _():
        acc_ref[...] = jnp.zeros_like(acc_ref)

    @pl.when((flag & _ACTIVE) != 0)
    def _():
        tok_row = tile_ref[s] * TS + lax.broadcasted_iota(jnp.int32, (TS, SB), 0)
        onehot = jnp.where(tok_ref[0] == tok_row, 1.0, 0.0).astype(BF16)
        acc_ref[...] += jnp.dot(onehot, _unpack_rows(yg_ref[...]), preferred_element_type=F32)

    @pl.when((flag & _LAST) != 0)
    def _():
        y = x1_ref[...] + acc_ref[...]
        ms = jnp.mean(y * y, axis=-1, keepdims=True)
        o_ref[...] = y * lax.rsqrt(ms + RMS_EPS) * g_ref[...]


def _combine_schedule(bounds, nslots):
    nt, nblk = bounds.shape[0] - 1, nslots // SB
    steps = nt + nblk
    lo, hi = bounds[:-1], bounds[1:]
    b_lo = jnp.minimum(lo // SB, nblk - 1)
    b_hi = jnp.where(hi > lo, (hi - 1) // SB, b_lo)
    nst = b_hi - b_lo + 1
    cum = jnp.cumsum(nst)
    start = cum - nst
    s = jnp.arange(steps, dtype=jnp.int32)
    tile = jnp.minimum(jnp.searchsorted(cum, s, side="right").astype(jnp.int32), nt - 1)
    active = s < cum[-1]
    blk = jnp.where(active, b_lo[tile] + s - start[tile], b_hi[nt - 1])
    first = active & (s == start[tile])
    last = active & (s == cum[tile] - 1)
    flag = first * _FIRST + last * _LAST + active * _ACTIVE
    return tile, blk.astype(jnp.int32), flag.astype(jnp.int32)


def _combine(tile, blk, flag, tok3, yg, x1, g):
    n = x1.shape[0]
    steps = tile.shape[0]
    grid_spec = pltpu.PrefetchScalarGridSpec(
        num_scalar_prefetch=3,
        grid=(steps,),
        in_specs=[pl.BlockSpec((1, 1, SB), lambda s, t, b, f: (b[s], 0, 0)),
                  pl.BlockSpec((SB, HALF), lambda s, t, b, f: (b[s], 0)),
                  pl.BlockSpec((TS, D_MODEL), lambda s, t, b, f: (t[s], 0)),
                  pl.BlockSpec(g.shape, lambda s, t, b, f: (0, 0))],
        out_specs=pl.BlockSpec((TS, D_MODEL), lambda s, t, b, f: (t[s], 0)),
        scratch_shapes=[pltpu.VMEM((TS, D_MODEL), F32)],
    )
    return pl.pallas_call(
        _combine_kernel,
        grid_spec=grid_spec,
        out_shape=jax.ShapeDtypeStruct((n, D_MODEL), F32),
        compiler_params=_cparams(("arbitrary",)),
        name="combine_norm",
    )(tile, blk, flag, tok3, yg, x1, g)


def _group_forward(x, p):
    b, seq, _ = x.shape
    n = b * seq
    cap = CAP_FACTOR * n // N_EXPERTS
    x2 = x.reshape(n, D_MODEL)
    cos_t, sa_t, sb_t = p["rot"]
    qa, kat, va, qb, kbt, vb = _inproj(x2, p["g_attn"], p["w_in"], cos_t[:seq], sa_t[:seq], sb_t[:seq], seq)
    r3 = lambda a: a.reshape(b, seq, a.shape[-1])
    oa = _window(r3(qa), kat, r3(va), p["sink"], p["g_out_a"])
    ob = _natten(r3(qb), kbt, r3(vb), p["na_tabs"], p["g_out_b"])
    x1, h, aff3 = _outproj(x2, oa.reshape(n, DQA), ob.reshape(n, DB), p["w_out"], p["g_ffn"], p["w_router_t"])

    nslots = N_EXPERTS * cap
    pos, c_excl, s_incl, s_excl = _route_select(aff3, cap)
    idx, (gate, s_at, c_at) = _rank_search(pos, 1, cap, [(aff3, _F32), (s_excl, _INT3), (c_excl, _INT1)])
    dest = (s_at + c_at).astype(jnp.int32).reshape(nslots)
    xe = _sc_gather_rows(h, idx.reshape(nslots)).reshape(N_EXPERTS, cap, HALF)
    ye = _moe(xe, gate, p["w_gate"], p["w_up"], p["w_down"])
    yg = _sc_scatter_rows(ye.reshape(nslots, HALF), dest)

    tok_sorted, _ = _rank_search(s_incl, N_EXPERTS, nslots)
    tile_end = s_incl.reshape(n)[TS - 1::TS].astype(jnp.int32)
    bounds = jnp.concatenate([jnp.zeros((1,), jnp.int32), tile_end])
    tile, blk, flag = _combine_schedule(bounds, nslots)
    y = _combine(tile, blk, flag, tok_sorted.reshape(-1, 1, SB), yg, x1, p["g_final"])
    return y.reshape(b, seq, D_MODEL)


def kernel(x_prompt, x_sample, g_attn, w_in, g_out_a, g_out_b, sink_a, rpb_b, w_out, g_ffn, w_router,
           w_gate, w_up, w_down, g_final):
    assert g_attn.shape[0] == 1, "single trunk layer"
    wr = w_router[0].T
    wr_hi = wr.astype(BF16)
    wr_lo = (wr - wr_hi.astype(F32)).astype(BF16)
    p = {
        "g_attn": g_attn[0][None, :], "w_in": w_in[0].astype(BF16),
        "g_out_a": g_out_a[0][None, :], "g_out_b": g_out_b[0][None, :],
        "sink": sink_a[0], "na_tabs": _natten_tables(rpb_b[0], 4 * NA_ROWS), "w_out": w_out[0].astype(BF16),
        "g_ffn": g_ffn[0][None, :], "w_router_t": jnp.concatenate([wr_hi, wr_lo], axis=0),
        "w_gate": w_gate[0].astype(BF16), "w_up": w_up[0].astype(BF16), "w_down": w_down[0].astype(BF16),
        "g_final": g_final[None, :],
        "rot": _rotary_tables(max(x_prompt.shape[1], x_sample.shape[1])),
    }
    return (_group_forward(x_prompt, p), _group_forward(x_sample, p))
```
